```python
import math
import jax
import jax.numpy as jnp
from jax import lax
import numpy as np

D_MODEL = 2048
BATCH = 4
SEQ = 2048
DEPTH = 4

CTX_LEN = 256
GRID_W = 64
EPS = 1e-6
Q_BLOCK = 128
ROPE_THETA = 10000.0
MIXER_ORDER = ('gqa', 'diff', 'ssd')
N_MIXERS = len(MIXER_ORDER)

GQA_HEAD_DIM = 128
GQA_HEADS = D_MODEL // GQA_HEAD_DIM
GQA_KV_HEADS = 4
GQA_GROUP = GQA_HEADS // GQA_KV_HEADS
GQA_WIDTH = GQA_HEADS * GQA_HEAD_DIM
GQA_KV_WIDTH = GQA_KV_HEADS * GQA_HEAD_DIM
GQA_IN = 2 * GQA_WIDTH + 2 * GQA_KV_WIDTH

DIFF_HEAD_DIM = 128
DIFF_HEADS = D_MODEL // (2 * DIFF_HEAD_DIM)
DIFF_WIDTH = 2 * DIFF_HEADS * DIFF_HEAD_DIM
DIFF_IN = 4 * DIFF_WIDTH

SSM_INNER = 2 * D_MODEL
SSM_HEAD_DIM = 64
SSM_HEADS = SSM_INNER // SSM_HEAD_DIM
SSM_GROUPS = 8
SSM_STATE = 128
SSM_CONV = 4
SSM_CHUNK = 128
SSM_XBC = SSM_INNER + 2 * SSM_GROUPS * SSM_STATE
SSM_IN = SSM_INNER + SSM_XBC + 2 * SSM_HEADS

kernel_name = 'hybrid_interleaved_gqa_diff_ssd_dit'


def rms_norm(x, g):
    xf = x.astype(jnp.float32)
    y = xf * lax.rsqrt(jnp.mean(xf * xf, axis=-1, keepdims=True) + EPS)
    return (y * g.astype(jnp.float32)).astype(x.dtype)


def adaln(cvec, w_mod, b_mod, n_out):
    cols = n_out * D_MODEL
    return jnp.split(jax.nn.silu(cvec) @ w_mod[:, :cols] + b_mod[:cols], n_out, axis=-1)


def lambda_init_fn(layer_idx):
    return 0.8 - 0.6 * math.exp(-0.3 * layer_idx)


def axial_rope_tables(row_ids, col_ids, head_dim, dtype):
    d_axis = head_dim // 2
    inv_freq = jnp.power(ROPE_THETA, -jnp.arange(0, d_axis, 2, dtype=jnp.float32) / d_axis)
    ang_r = row_ids.astype(jnp.float32)[:, None] * inv_freq[None, :]
    ang_c = col_ids.astype(jnp.float32)[:, None] * inv_freq[None, :]
    return (jnp.cos(ang_r)[:, None, :].astype(dtype), jnp.sin(ang_r)[:, None, :].astype(dtype),
            jnp.cos(ang_c)[:, None, :].astype(dtype), jnp.sin(ang_c)[:, None, :].astype(dtype))


def rope_1d(x, cos, sin):
    x1, x2 = jnp.split(x, 2, axis=-1)
    return jnp.concatenate([x1 * cos - x2 * sin, x2 * cos + x1 * sin], axis=-1)


def apply_axial_rope(x, tables):
    cos_r, sin_r, cos_c, sin_c = tables
    xr, xc = jnp.split(x, 2, axis=-1)
    return jnp.concatenate([rope_1d(xr, cos_r, sin_r), rope_1d(xc, cos_c, sin_c)], axis=-1)


def sweep_query_blocks(attend, q, k, v):
    B_, L = q.shape[:2]
    n_blk = L // Q_BLOCK
    q_blocks = jnp.moveaxis(q.reshape(B_, n_blk, Q_BLOCK, *q.shape[2:]), 1, 0)
    out = lax.map(lambda qb: attend(qb, k, v), q_blocks)
    return jnp.moveaxis(out, 0, 1).reshape(B_, L, out.shape[-1])


def gqa_attend(q, k, v):
    B_, Lq = q.shape[:2]
    qg = q.reshape(B_, Lq, GQA_KV_HEADS, GQA_GROUP, GQA_HEAD_DIM)
    s = jnp.einsum('bqhgd,bkhd->bhgqk', qg, k).astype(jnp.float32) * (GQA_HEAD_DIM ** -0.5)
    p = jax.nn.softmax(s, axis=-1).astype(v.dtype)
    return jnp.einsum('bhgqk,bkhd->bqhgd', p, v).reshape(B_, Lq, GQA_WIDTH)


def gqa_mixer(h_ctx, h_lat, w_in, q_gain, k_gain, w_out, rope, need_ctx):
    B_, L = h_lat.shape[:2]
    Lc = h_ctx.shape[1]
    splits = [GQA_WIDTH, GQA_WIDTH + GQA_KV_WIDTH, GQA_WIDTH + 2 * GQA_KV_WIDTH]
    q_l, k_l, v_l, z_l = jnp.split(h_lat @ w_in, splits, axis=-1)
    q_l = apply_axial_rope(rms_norm(q_l.reshape(B_, L, GQA_HEADS, GQA_HEAD_DIM), q_gain), rope)
    k_l = apply_axial_rope(rms_norm(k_l.reshape(B_, L, GQA_KV_HEADS, GQA_HEAD_DIM), k_gain), rope)
    v_l = v_l.reshape(B_, L, GQA_KV_HEADS, GQA_HEAD_DIM)
    if need_ctx:
        q_c, k_c, v_c, z_c = jnp.split(h_ctx @ w_in, splits, axis=-1)
    else:
        k_c, v_c = jnp.split(h_ctx @ w_in[:, GQA_WIDTH:GQA_WIDTH + 2 * GQA_KV_WIDTH], 2, axis=-1)
    k_c = rms_norm(k_c.reshape(B_, Lc, GQA_KV_HEADS, GQA_HEAD_DIM), k_gain)
    v_c = v_c.reshape(B_, Lc, GQA_KV_HEADS, GQA_HEAD_DIM)
    k_all = jnp.concatenate([k_c, k_l], axis=1)
    v_all = jnp.concatenate([v_c, v_l], axis=1)
    y_l = sweep_query_blocks(gqa_attend, q_l, k_all, v_all)
    out_l = (y_l * jax.nn.silu(z_l)) @ w_out
    out_c = None
    if need_ctx:
        q_c = rms_norm(q_c.reshape(B_, Lc, GQA_HEADS, GQA_HEAD_DIM), q_gain)
        out_c = (gqa_attend(q_c, k_c, v_c) * jax.nn.silu(z_c)) @ w_out
    return out_c, out_l


def diff_attend(q, k, v, lam):
    B_, Lq = q.shape[:2]
    s = jnp.einsum('bqhtd,bkhtd->bhtqk', q, k).astype(jnp.float32) * (DIFF_HEAD_DIM ** -0.5)
    p = jax.nn.softmax(s, axis=-1)
    a = (p[:, :, 0] - lam * p[:, :, 1]).astype(v.dtype)
    return jnp.einsum('bhqk,bkhe->bqhe', a, v).reshape(B_, Lq, DIFF_WIDTH)


def diff_mixer(h_ctx, h_lat, w_in, lambda_q1, lambda_k1, lambda_q2, lambda_k2, subln, w_out,
               rope, lambda_init, need_ctx):
    f32 = jnp.float32
    lam = (jnp.exp(jnp.sum(lambda_q1.astype(f32) * lambda_k1.astype(f32)))
           - jnp.exp(jnp.sum(lambda_q2.astype(f32) * lambda_k2.astype(f32))) + lambda_init)

    def halves(t):
        return t.reshape(t.shape[0], t.shape[1], 2 * DIFF_HEADS, DIFF_HEAD_DIM)

    def pair(t):
        return t.reshape(t.shape[0], t.shape[1], DIFF_HEADS, 2, DIFF_HEAD_DIM)

    def values(t):
        return t.reshape(t.shape[0], t.shape[1], DIFF_HEADS, 2 * DIFF_HEAD_DIM)

    def finish(o, z):
        o = rms_norm(o.reshape(o.shape[0], o.shape[1], DIFF_HEADS, 2 * DIFF_HEAD_DIM), subln)
        o = o * (1.0 - lambda_init)
        return (o.reshape(z.shape) * jax.nn.silu(z)) @ w_out

    q_l, k_l, v_l, z_l = jnp.split(h_lat @ w_in, 4, axis=-1)
    q_l = pair(apply_axial_rope(halves(q_l), rope))
    k_l = pair(apply_axial_rope(halves(k_l), rope))
    v_l = values(v_l)
    if need_ctx:
        q_c, k_c, v_c, z_c = jnp.split(h_ctx @ w_in, 4, axis=-1)
    else:
        k_c, v_c = jnp.split(h_ctx @ w_in[:, DIFF_WIDTH:3 * DIFF_WIDTH], 2, axis=-1)
    k_c = pair(k_c)
    v_c = values(v_c)
    k_all = jnp.concatenate([k_c, k_l], axis=1)
    v_all = jnp.concatenate([v_c, v_l], axis=1)
    y_l = sweep_query_blocks(lambda qb, k, v: diff_attend(qb, k, v, lam), q_l, k_all, v_all)
    out_l = finish(y_l, z_l)
    out_c = None
    if need_ctx:
        out_c = finish(diff_attend(pair(q_c), k_c, v_c, lam), z_c)
    return out_c, out_l


def centred_depthwise_conv(x, w, b):
    left = (SSM_CONV - 1) // 2
    right = SSM_CONV - 1 - left
    y = lax.conv_general_dilated(x, w[:, None, :], window_strides=(1,), padding=((left, right),),
                                 dimension_numbers=('NWC', 'WIO', 'NWC'),
                                 feature_group_count=x.shape[-1])
    return y + b


def ssd_chunked(xh, dt, A, Bm, Cm, h0):
    B_, L, H, P = xh.shape
    G, N = Bm.shape[2], Bm.shape[3]
    Hg = H // G
    nc = L // SSM_CHUNK
    Q = SSM_CHUNK
    xr = xh.astype(jnp.float32).reshape(B_, nc, Q, G, Hg, P)
    dtr = dt.reshape(B_, nc, Q, G, Hg)
    Br = Bm.astype(jnp.float32).reshape(B_, nc, Q, G, N)
    Cr = Cm.astype(jnp.float32).reshape(B_, nc, Q, G, N)
    xdt = xr * dtr[..., None]
    acum = jnp.cumsum(jnp.moveaxis(dtr * A.reshape(G, Hg), 2, -1), axis=-1)
    causal = jnp.tril(jnp.ones((Q, Q), dtype=bool))
    decay = jnp.exp(jnp.where(causal, acum[..., :, None] - acum[..., None, :], -jnp.inf))
    cb = jnp.einsum('bcqgn,bckgn->bcgqk', Cr, Br)
    y_diag = jnp.einsum('bcgqk,bcghqk,bckghp->bcqghp', cb, decay, xdt)
    to_end = jnp.exp(acum[..., -1:] - acum)
    states = jnp.einsum('bckgn,bcghk,bckghp->bcghpn', Br, to_end, xdt)
    chunk_decay = jnp.exp(acum[..., -1])

    def step(h, inp):
        s, d = inp
        return h * d[..., None, None] + s, h

    h_final, h_enter = lax.scan(step, h0, (jnp.moveaxis(states, 1, 0), jnp.moveaxis(chunk_decay, 1, 0)))
    h_enter = jnp.moveaxis(h_enter, 0, 1)
    y_off = jnp.einsum('bcqgn,bcghpn,bcghq->bcqghp', Cr, h_enter, jnp.exp(acum))
    return (y_diag + y_off).reshape(B_, L, H, P), h_final


def ssd_direction(xh, dt_raw, Bm, Cm, dt_bias, A_log, h0, reverse):
    if reverse:
        xh, dt_raw, Bm, Cm = xh[:, ::-1], dt_raw[:, ::-1], Bm[:, ::-1], Cm[:, ::-1]
    dt = jax.nn.softplus(dt_raw.astype(jnp.float32) + dt_bias.astype(jnp.float32))
    A = -jnp.exp(A_log.astype(jnp.float32))
    y, h = ssd_chunked(xh, dt, A, Bm, Cm, h0)
    if reverse:
        y = y[:, ::-1]
    return y, h


def ssd_mixer(h_ctx, h_lat, w_in, conv_w, conv_b, dt_bias, A_log, D_skip, ssm_norm, w_out, need_ctx):
    def project(h):
        B_, L = h.shape[:2]
        z, xbc, dt_raw = jnp.split(h @ w_in, [SSM_INNER, SSM_INNER + SSM_XBC], axis=-1)
        xbc = jax.nn.silu(centred_depthwise_conv(xbc, conv_w, conv_b))
        xs, Bm, Cm = jnp.split(xbc, [SSM_INNER, SSM_INNER + SSM_GROUPS * SSM_STATE], axis=-1)
        return (z, xs.reshape(B_, L, SSM_HEADS, SSM_HEAD_DIM),
                Bm.reshape(B_, L, SSM_GROUPS, SSM_STATE), Cm.reshape(B_, L, SSM_GROUPS, SSM_STATE),
                dt_raw.reshape(B_, L, 2, SSM_HEADS))

    def bidir(proj, h0_f, h0_b):
        _, xs, Bm, Cm, dt_raw = proj
        y_f, h_f = ssd_direction(xs, dt_raw[:, :, 0], Bm, Cm, dt_bias[0], A_log[0], h0_f, False)
        y_b, h_b = ssd_direction(xs, dt_raw[:, :, 1], Bm, Cm, dt_bias[1], A_log[1], h0_b, True)
        y = y_f + y_b + D_skip.astype(jnp.float32)[:, None] * xs.astype(jnp.float32)
        return y, h_f, h_b

    def finish(y, z):
        B_, L = z.shape[:2]
        g = (y.astype(z.dtype).reshape(B_, L, SSM_INNER) * jax.nn.silu(z))
        g = rms_norm(g.reshape(B_, L, SSM_GROUPS, SSM_INNER // SSM_GROUPS),
                     ssm_norm.reshape(SSM_GROUPS, SSM_INNER // SSM_GROUPS))
        return g.reshape(B_, L, SSM_INNER) @ w_out

    p_c = project(h_ctx)
    p_l = project(h_lat)
    B_ = h_lat.shape[0]
    h0 = jnp.zeros((B_, SSM_GROUPS, SSM_HEADS // SSM_GROUPS, SSM_HEAD_DIM, SSM_STATE), jnp.float32)
    y_c, hf_c, hb_c = bidir(p_c, h0, h0)
    y_l, _, _ = bidir(p_l, hf_c, hb_c)
    out_c = finish(y_c, p_c[0]) if need_ctx else None
    return out_c, finish(y_l, p_l[0])


def setup_inputs(seed: int = 0) -> dict:
    key = jax.random.key(seed)
    keys = iter(jax.random.split(key, 96))

    def normal(shape, scale):
        return jax.random.normal(next(keys), shape, jnp.float32) * scale

    def gain(n):
        return 1.0 + normal((n,), 0.05)

    D = D_MODEL
    inp = {
        'x': normal((BATCH, SEQ, D), 1.0),
        'c': normal((BATCH, D), 1.0),
        'ctx': normal((BATCH, CTX_LEN, D), 1.0),
        'c_ctx': normal((D,), 1.0),
    }
    for i in range(DEPTH):
        p = 'l%d_' % i
        kind = MIXER_ORDER[i % N_MIXERS]
        inp[p + 'w_mod'] = normal((D, 3 * D), D ** -0.5)
        inp[p + 'b_mod'] = normal((3 * D,), 0.02)
        inp[p + 'norm'] = gain(D)
        if kind == 'gqa':
            inp[p + 'w_in'] = normal((D, GQA_IN), D ** -0.5)
            inp[p + 'q_gain'] = gain(GQA_HEAD_DIM)
            inp[p + 'k_gain'] = gain(GQA_HEAD_DIM)
            inp[p + 'w_out'] = normal((GQA_WIDTH, D), GQA_WIDTH ** -0.5)
        elif kind == 'diff':
            inp[p + 'w_in'] = normal((D, DIFF_IN), D ** -0.5)
            inp[p + 'lambda_q1'] = normal((DIFF_HEAD_DIM,), 0.1)
            inp[p + 'lambda_k1'] = normal((DIFF_HEAD_DIM,), 0.1)
            inp[p + 'lambda_q2'] = normal((DIFF_HEAD_DIM,), 0.1)
            inp[p + 'lambda_k2'] = normal((DIFF_HEAD_DIM,), 0.1)
            inp[p + 'subln'] = gain(2 * DIFF_HEAD_DIM)
            inp[p + 'w_out'] = normal((DIFF_WIDTH, D), DIFF_WIDTH ** -0.5)
        else:
            inp[p + 'w_in'] = normal((D, SSM_IN), D ** -0.5)
            inp[p + 'conv_w'] = normal((SSM_CONV, SSM_XBC), SSM_CONV ** -0.5)
            inp[p + 'conv_b'] = normal((SSM_XBC,), 0.02)
            dt0 = jnp.exp(jax.random.uniform(next(keys), (2, SSM_HEADS), jnp.float32,
                                             minval=math.log(1e-3), maxval=math.log(1e-1)))
            inp[p + 'dt_bias'] = dt0 + jnp.log(-jnp.expm1(-dt0))
            inp[p + 'A_log'] = jnp.log(jax.random.uniform(next(keys), (2, SSM_HEADS), jnp.float32,
                                                           minval=1.0, maxval=16.0))
            inp[p + 'D'] = gain(SSM_HEADS)
            inp[p + 'ssm_norm'] = gain(SSM_INNER)
            inp[p + 'w_out'] = normal((SSM_INNER, D), SSM_INNER ** -0.5)
    inp['final_norm'] = gain(D)
    return inp


def reference(x, c, ctx, c_ctx,
              l0_w_mod, l0_b_mod, l0_norm, l0_w_in, l0_q_gain, l0_k_gain, l0_w_out,
              l1_w_mod, l1_b_mod, l1_norm, l1_w_in, l1_lambda_q1, l1_lambda_k1, l1_lambda_q2,
              l1_lambda_k2, l1_subln, l1_w_out,
              l2_w_mod, l2_b_mod, l2_norm, l2_w_in, l2_conv_w, l2_conv_b, l2_dt_bias, l2_A_log,
              l2_D, l2_ssm_norm, l2_w_out,
              l3_w_mod, l3_b_mod, l3_norm, l3_w_in, l3_q_gain, l3_k_gain, l3_w_out,
              final_norm):
    layers = (
        ((l0_w_mod, l0_b_mod, l0_norm), (l0_w_in, l0_q_gain, l0_k_gain, l0_w_out)),
        ((l1_w_mod, l1_b_mod, l1_norm), (l1_w_in, l1_lambda_q1, l1_lambda_k1, l1_lambda_q2,
                                         l1_lambda_k2, l1_subln, l1_w_out)),
        ((l2_w_mod, l2_b_mod, l2_norm), (l2_w_in, l2_conv_w, l2_conv_b, l2_dt_bias, l2_A_log,
                                         l2_D, l2_ssm_norm, l2_w_out)),
        ((l3_w_mod, l3_b_mod, l3_norm), (l3_w_in, l3_q_gain, l3_k_gain, l3_w_out)),
    )
    n_tok = x.shape[1]
    n_rows = n_tok // GRID_W
    row_ids = jnp.repeat(jnp.arange(n_rows, dtype=jnp.int32), GRID_W)
    col_ids = jnp.tile(jnp.arange(GRID_W, dtype=jnp.int32), n_rows)
    rope_gqa = axial_rope_tables(row_ids, col_ids, GQA_HEAD_DIM, x.dtype)
    rope_diff = axial_rope_tables(row_ids, col_ids, DIFF_HEAD_DIM, x.dtype)

    for i in range(DEPTH):
        (w_mod, b_mod, norm), params = layers[i]
        kind = MIXER_ORDER[i % N_MIXERS]
        need_ctx = i < DEPTH - 1
        shift, scale, gate = adaln(c, w_mod, b_mod, 3)
        h_lat = rms_norm(x, norm) * (1 + scale[:, None, :]) + shift[:, None, :]
        mods_ctx = adaln(c_ctx, w_mod, b_mod, 3 if need_ctx else 2)
        h_ctx = rms_norm(ctx, norm) * (1 + mods_ctx[1]) + mods_ctx[0]
        if kind == 'gqa':
            out_ctx, out_lat = gqa_mixer(h_ctx, h_lat, *params, rope_gqa, need_ctx)
        elif kind == 'diff':
            out_ctx, out_lat = diff_mixer(h_ctx, h_lat, *params, rope_diff, lambda_init_fn(i), need_ctx)
        else:
            out_ctx, out_lat = ssd_mixer(h_ctx, h_lat, *params, need_ctx)
        x = x + gate[:, None, :] * out_lat
        if need_ctx:
            ctx = ctx + mods_ctx[2] * out_ctx
    return rms_norm(x, final_norm)
```

```python
import collections
import functools
import math

import jax
import jax.numpy as jnp
from jax import lax
from jax.experimental import pallas as pl
from jax.experimental.pallas import tpu as pltpu

F32 = jnp.float32
BF16 = jnp.bfloat16

EPS = 1e-6
ROPE_THETA = 10000.0
HEAD_DIM = 128
GQA_KV_HEADS = 4
SSM_HEAD_DIM = 64
SSM_GROUPS = 8
SSM_STATE = 128
SSM_CONV = 4
SSM_CHUNK = 128
GRID_W = 64
DEPTH = 4
N_MODS = 3
MOD_ROWS = 8
NORM_SLAB = 64
BF16_SUBLANES = 16
VMEM_LIMIT = 56 * 1024 * 1024

Dims = collections.namedtuple("Dims", "batch seq ctx d")


def _rows(dm):
    return dm.ctx + dm.seq


def _silu(v):
    return v * (1.0 / (1.0 + jnp.exp(-v)))


def _params(semantics):
    return pltpu.CompilerParams(dimension_semantics=semantics, vmem_limit_bytes=VMEM_LIMIT)


def _mod_kernel(c_ref, w_ref, b_ref, o_ref):
    s = _silu(c_ref[...]).astype(BF16)
    o_ref[...] = jnp.dot(s, w_ref[...].astype(BF16), preferred_element_type=F32) + b_ref[...]


def _modulation(cvec, w_mod, b_mod, tn=512):
    d, n = w_mod.shape
    return pl.pallas_call(
        _mod_kernel,
        grid=(n // tn,),
        in_specs=[pl.BlockSpec((MOD_ROWS, d), lambda j: (0, 0)),
                  pl.BlockSpec((d, tn), lambda j: (0, j)),
                  pl.BlockSpec((1, tn), lambda j: (0, j))],
        out_specs=pl.BlockSpec((MOD_ROWS, tn), lambda j: (0, j)),
        out_shape=jax.ShapeDtypeStruct((MOD_ROWS, n), F32),
        compiler_params=_params(("arbitrary",)),
        name="adaln_mod",
    )(cvec, w_mod, b_mod.reshape(1, n))


def _rope_swap(v):
    quarter = HEAD_DIM // 4
    lane = lax.broadcasted_iota(jnp.int32, v.shape, 1)
    first = (lane % (2 * quarter)) < quarter
    return jnp.where(first, pltpu.roll(v, HEAD_DIM - quarter, 1), pltpu.roll(v, quarter, 1))


def _in_proj_kernel(*refs, tm, tn, tiles_per_batch, ctx_len, segs, has_rope, has_gain, has_dt):
    it = iter(refs)
    x_ref, g_ref, shl_ref, scl_ref, shc_ref, scc_ref, w_ref = (next(it) for _ in range(7))
    cos_ref = sin_ref = gain_ref = wdt_ref = wdtT_ref = dt_ref = dtT_ref = None
    if has_rope:
        cos_ref, sin_ref = next(it), next(it)
    if has_gain:
        gain_ref = next(it)
    if has_dt:
        wdt_ref, wdtT_ref = next(it), next(it)
    o_ref = next(it)
    if has_dt:
        dt_ref, dtT_ref = next(it), next(it)
    h_ref = next(it)

    i = pl.program_id(0)
    j = pl.program_id(1)

    @pl.when(j == 0)
    def _():
        def slab(r, carry):
            rows = pl.ds(pl.multiple_of(r * NORM_SLAB, NORM_SLAB), NORM_SLAB)
            x = x_ref[rows, :]
            y = x * lax.rsqrt(jnp.mean(x * x, axis=-1, keepdims=True) + EPS) * g_ref[...]
            row = ((i % tiles_per_batch) * tm + r * NORM_SLAB
                   + lax.broadcasted_iota(jnp.int32, (NORM_SLAB, 1), 0))
            is_ctx = row < ctx_len
            scale = jnp.where(is_ctx, scc_ref[0], scl_ref[0])
            shift = jnp.where(is_ctx, shc_ref[0], shl_ref[0])
            h_ref[rows, :] = (y * (1.0 + scale) + shift).astype(BF16)
            return carry

        lax.fori_loop(0, tm // NORM_SLAB, slab, 0)
        h = h_ref[...]
        if has_dt:
            dt_ref[...] = jnp.dot(h, wdt_ref[...], preferred_element_type=F32)
            dtT_ref[...] = lax.dot_general(wdtT_ref[...], h, (((1,), (1,)), ((), ())),
                                           preferred_element_type=F32)

    def finish(gain_idx, rope, scale):
        acc = jnp.dot(h_ref[...], w_ref[...], preferred_element_type=F32)
        if gain_idx is None and not rope and scale is None:
            o_ref[...] = acc.astype(o_ref.dtype)
            return
        for hh in range(tn // HEAD_DIM):
            cols = slice(hh * HEAD_DIM, (hh + 1) * HEAD_DIM)
            v = acc[:, cols]
            if gain_idx is not None:
                v = v * lax.rsqrt(jnp.mean(v * v, axis=-1, keepdims=True) + EPS)
                v = v * gain_ref[gain_idx:gain_idx + 1, :]
            if rope:
                v = v * cos_ref[...] + _rope_swap(v) * sin_ref[...]
            if scale is not None:
                v = v * scale
            o_ref[:, cols] = v.astype(o_ref.dtype)

    for (lo, hi, gain_idx, rope, scale) in segs:
        pl.when((j >= lo) & (j < hi))(functools.partial(finish, gain_idx, rope, scale))


def _in_proj(xc, norm_g, mods, w, dm, *, tm, tn, segs, rope=None, gains=None, w_dt=None):
    m, d = xc.shape
    n = w.shape[1]
    tpb = _rows(dm) // tm
    mods3 = mods.reshape(MOD_ROWS, 1, N_MODS * d)
    has_rope, has_gain, has_dt = rope is not None, gains is not None, w_dt is not None

    def lat(col):
        return pl.BlockSpec((1, 1, d), lambda i, j: (i // tpb, 0, col))

    def ctx(col):
        return pl.BlockSpec((1, 1, d), lambda i, j: (dm.batch, 0, col))

    args = [xc, norm_g.reshape(1, d), mods3, mods3, mods3, mods3, w]
    in_specs = [pl.BlockSpec((tm, d), lambda i, j: (i, 0)),
                pl.BlockSpec((1, d), lambda i, j: (0, 0)),
                lat(0), lat(1), ctx(0), ctx(1),
                pl.BlockSpec((d, tn), lambda i, j: (0, j))]
    if has_rope:
        args += list(rope)
        in_specs += [pl.BlockSpec((tm, HEAD_DIM), lambda i, j: (i % tpb, 0))] * 2
    if has_gain:
        args.append(gains)
        in_specs.append(pl.BlockSpec(gains.shape, lambda i, j: (0, 0)))
    out_shape = [jax.ShapeDtypeStruct((m, n), BF16)]
    out_specs = [pl.BlockSpec((tm, tn), lambda i, j: (i, j))]
    if has_dt:
        n_dt = w_dt.shape[1]
        args += [w_dt, w_dt.T]
        in_specs += [pl.BlockSpec((d, n_dt), lambda i, j: (0, 0)),
                     pl.BlockSpec((n_dt, d), lambda i, j: (0, 0))]
        out_shape += [jax.ShapeDtypeStruct((m, n_dt), F32), jax.ShapeDtypeStruct((n_dt, m), F32)]
        out_specs += [pl.BlockSpec((tm, n_dt), lambda i, j: (i, 0)),
                      pl.BlockSpec((n_dt, tm), lambda i, j: (0, i))]
    kern = functools.partial(_in_proj_kernel, tm=tm, tn=tn, tiles_per_batch=tpb, ctx_len=dm.ctx,
                             segs=segs, has_rope=has_rope, has_gain=has_gain, has_dt=has_dt)
    out = pl.pallas_call(
        kern,
        grid=(m // tm, n // tn),
        in_specs=in_specs,
        out_specs=out_specs,
        out_shape=out_shape,
        scratch_shapes=[pltpu.VMEM((tm, d), BF16)],
        compiler_params=_params(("parallel", "arbitrary")),
        name="in_proj",
    )(*args)
    return out if has_dt else out[0]


def _out_proj_kernel(g_ref, w_ref, x_ref, gl_ref, gc_ref, o_ref, *, tm, tiles_per_batch, ctx_len):
    i = pl.program_id(0)
    acc = jnp.dot(g_ref[...], w_ref[...], preferred_element_type=F32)
    row = (i % tiles_per_batch) * tm + lax.broadcasted_iota(jnp.int32, (tm, 1), 0)
    gate = jnp.where(row < ctx_len, gc_ref[0], gl_ref[0])
    o_ref[...] = x_ref[...] + gate * acc


def _out_proj(g, w, xc, mods, dm, *, tm, tn):
    m, k = g.shape
    d = w.shape[1]
    tpb = _rows(dm) // tm
    tiles_n = d // tn
    mods3 = mods.reshape(MOD_ROWS, 1, N_MODS * d)
    gate_col = 2 * tiles_n
    kern = functools.partial(_out_proj_kernel, tm=tm, tiles_per_batch=tpb, ctx_len=dm.ctx)
    return pl.pallas_call(
        kern,
        grid=(m // tm, tiles_n),
        in_specs=[pl.BlockSpec((tm, k), lambda i, j: (i, 0)),
                  pl.BlockSpec((k, tn), lambda i, j: (0, j)),
                  pl.BlockSpec((tm, tn), lambda i, j: (i, j)),
                  pl.BlockSpec((1, 1, tn), lambda i, j: (i // tpb, 0, gate_col + j)),
                  pl.BlockSpec((1, 1, tn), lambda i, j: (dm.batch, 0, gate_col + j))],
        out_specs=pl.BlockSpec((tm, tn), lambda i, j: (i, j)),
        out_shape=jax.ShapeDtypeStruct((m, d), F32),
        compiler_params=_params(("parallel", "arbitrary")),
        name="out_proj",
    )(g, w, xc, mods3, mods3)


def _gqa_kernel(q_ref, k_ref, v_ref, z_ref, o_ref, s_ref, m_ref, l_ref, acc_ref, *, tq, tk, group, n_chunks):
    qi = pl.program_id(2)
    chunks = jnp.where(qi == 0, 1, n_chunks)
    q = jnp.concatenate([q_ref[:, g * HEAD_DIM:(g + 1) * HEAD_DIM] for g in range(group)], axis=0)

    m_ref[...] = jnp.full(m_ref.shape, -jnp.inf, F32)
    l_ref[...] = jnp.zeros(l_ref.shape, F32)
    acc_ref[...] = jnp.zeros(acc_ref.shape, F32)

    def scores(c, carry):
        kc = k_ref[pl.ds(pl.multiple_of(c * tk, tk), tk), :]
        s = lax.dot_general(q, kc, (((1,), (1,)), ((), ())), preferred_element_type=F32)
        s_ref[c] = s
        part = s[:, :HEAD_DIM]
        for t in range(1, tk // HEAD_DIM):
            part = jnp.maximum(part, s[:, t * HEAD_DIM:(t + 1) * HEAD_DIM])
        m_ref[...] = jnp.maximum(m_ref[...], part)
        return carry

    lax.fori_loop(0, chunks, scores, 0)
    m = jnp.max(m_ref[...], axis=1, keepdims=True)

    def weighted(c, carry):
        p = jnp.exp(s_ref[c] - m)
        part = p[:, :HEAD_DIM]
        for t in range(1, tk // HEAD_DIM):
            part = part + p[:, t * HEAD_DIM:(t + 1) * HEAD_DIM]
        l_ref[...] += part
        vc = v_ref[pl.ds(pl.multiple_of(c * tk, tk), tk), :]
        acc_ref[...] += jnp.dot(p.astype(BF16), vc, preferred_element_type=F32)
        return carry

    lax.fori_loop(0, chunks, weighted, 0)
    inv = 1.0 / jnp.sum(l_ref[...], axis=1, keepdims=True)
    o = acc_ref[...] * inv
    for g in range(group):
        cols = slice(g * HEAD_DIM, (g + 1) * HEAD_DIM)
        z = z_ref[:, cols].astype(F32)
        o_ref[:, cols] = (o[g * tq:(g + 1) * tq] * _silu(z)).astype(o_ref.dtype)


def _gqa_attention(proj, dm, *, tq, tk):
    rows = _rows(dm)
    m = proj.shape[0]
    width = dm.d
    heads = width // HEAD_DIM
    group = heads // GQA_KV_HEADS
    gw = group * HEAD_DIM
    qt = rows // tq
    n_chunks = rows // tk
    k_col0 = width // HEAD_DIM
    v_col0 = k_col0 + GQA_KV_HEADS
    z_col0 = (width + 2 * GQA_KV_HEADS * HEAD_DIM) // gw
    kern = functools.partial(_gqa_kernel, tq=tq, tk=tk, group=group, n_chunks=n_chunks)
    return pl.pallas_call(
        kern,
        grid=(dm.batch, GQA_KV_HEADS, qt),
        in_specs=[pl.BlockSpec((tq, gw), lambda b, h, i: (b * qt + i, h)),
                  pl.BlockSpec((rows, HEAD_DIM), lambda b, h, i: (b, k_col0 + h)),
                  pl.BlockSpec((rows, HEAD_DIM), lambda b, h, i: (b, v_col0 + h)),
                  pl.BlockSpec((tq, gw), lambda b, h, i: (b * qt + i, z_col0 + h))],
        out_specs=pl.BlockSpec((tq, gw), lambda b, h, i: (b * qt + i, h)),
        out_shape=jax.ShapeDtypeStruct((m, width), BF16),
        scratch_shapes=[pltpu.VMEM((n_chunks, group * tq, tk), F32),
                        pltpu.VMEM((group * tq, HEAD_DIM), F32),
                        pltpu.VMEM((group * tq, HEAD_DIM), F32),
                        pltpu.VMEM((group * tq, HEAD_DIM), F32)],
        compiler_params=_params(("parallel", "parallel", "arbitrary")),
        name="gqa_attention",
    )(proj, proj, proj, proj)


def _diff_kernel(lam_ref, q_ref, k_ref, v_ref, z_ref, subln_ref, o_ref,
                 s_ref, m_ref, l_ref, acc_ref, *, tq, tk, n_chunks, lambda_init):
    qi = pl.program_id(2)
    chunks = jnp.where(qi == 0, 1, n_chunks)
    lv = lam_ref[...]
    lam = (jnp.exp(jnp.sum(lv[0:1] * lv[1:2], axis=1, keepdims=True))
           - jnp.exp(jnp.sum(lv[2:3] * lv[3:4], axis=1, keepdims=True)) + lambda_init)
    m_ref[...] = jnp.full(m_ref.shape, -jnp.inf, F32)
    l_ref[...] = jnp.zeros(l_ref.shape, F32)
    acc_ref[...] = jnp.zeros(acc_ref.shape, F32)

    def scores(c, carry):
        kc = k_ref[pl.ds(pl.multiple_of(c * tk, tk), tk), :]
        for t in range(2):
            cols = slice(t * HEAD_DIM, (t + 1) * HEAD_DIM)
            s = lax.dot_general(q_ref[:, cols], kc[:, cols], (((1,), (1,)), ((), ())),
                                preferred_element_type=F32)
            s_ref[c, t * tq:(t + 1) * tq, :] = s
            part = s[:, :HEAD_DIM]
            for u in range(1, tk // HEAD_DIM):
                part = jnp.maximum(part, s[:, u * HEAD_DIM:(u + 1) * HEAD_DIM])
            m_ref[t * tq:(t + 1) * tq, :] = jnp.maximum(m_ref[t * tq:(t + 1) * tq, :], part)
        return carry

    lax.fori_loop(0, chunks, scores, 0)
    m = jnp.max(m_ref[...], axis=1, keepdims=True)

    def weighted(c, carry):
        p = jnp.exp(s_ref[c] - m)
        part = p[:, :HEAD_DIM]
        for u in range(1, tk // HEAD_DIM):
            part = part + p[:, u * HEAD_DIM:(u + 1) * HEAD_DIM]
        l_ref[...] += part
        vc = v_ref[pl.ds(pl.multiple_of(c * tk, tk), tk), :]
        acc_ref[...] += jnp.dot(p.astype(BF16), vc, preferred_element_type=F32)
        return carry

    lax.fori_loop(0, chunks, weighted, 0)
    inv = 1.0 / jnp.sum(l_ref[...], axis=1, keepdims=True)
    o = acc_ref[...] * inv
    o = o[:tq] - lam * o[tq:]
    o = o * lax.rsqrt(jnp.mean(o * o, axis=-1, keepdims=True) + EPS) * subln_ref[...]
    o = o * (1.0 - lambda_init)
    o_ref[...] = (o * _silu(z_ref[...].astype(F32))).astype(o_ref.dtype)


def _diff_attention(proj, lam_vecs, subln, dm, *, tq, tk, lambda_init):
    rows = _rows(dm)
    m = proj.shape[0]
    width = dm.d
    hw = 2 * HEAD_DIM
    heads = width // hw
    qt = rows // tq
    n_chunks = rows // tk
    kern = functools.partial(_diff_kernel, tq=tq, tk=tk, n_chunks=n_chunks, lambda_init=lambda_init)
    return pl.pallas_call(
        kern,
        grid=(dm.batch, heads, qt),
        in_specs=[pl.BlockSpec(lam_vecs.shape, lambda b, h, i: (0, 0)),
                  pl.BlockSpec((tq, hw), lambda b, h, i: (b * qt + i, h)),
                  pl.BlockSpec((rows, hw), lambda b, h, i: (b, heads + h)),
                  pl.BlockSpec((rows, hw), lambda b, h, i: (b, 2 * heads + h)),
                  pl.BlockSpec((tq, hw), lambda b, h, i: (b * qt + i, 3 * heads + h)),
                  pl.BlockSpec((1, hw), lambda b, h, i: (0, 0))],
        out_specs=pl.BlockSpec((tq, hw), lambda b, h, i: (b * qt + i, h)),
        out_shape=jax.ShapeDtypeStruct((m, width), BF16),
        scratch_shapes=[pltpu.VMEM((n_chunks, 2 * tq, tk), F32),
                        pltpu.VMEM((2 * tq, HEAD_DIM), F32),
                        pltpu.VMEM((2 * tq, HEAD_DIM), F32),
                        pltpu.VMEM((2 * tq, hw), F32)],
        compiler_params=_params(("parallel", "parallel", "arbitrary")),
        name="diff_attention",
    )(lam_vecs, proj, proj, proj, proj, subln.reshape(1, hw))


def _conv_kernel(cur_ref, prev_ref, next_ref, w_ref, b_ref, o_ref, *, tr, tiles_per_batch, ctx_tiles):
    r = pl.program_id(1)
    x = cur_ref[...].astype(F32)
    row = lax.broadcasted_iota(jnp.int32, x.shape, 0)
    has_prev = (r != 0) & (r != ctx_tiles)
    has_next = (r != ctx_tiles - 1) & (r != tiles_per_batch - 1)
    prev = jnp.where(has_prev, prev_ref[BF16_SUBLANES - 1:BF16_SUBLANES, :].astype(F32), 0.0)
    nxt0 = jnp.where(has_next, next_ref[0:1, :].astype(F32), 0.0)
    nxt1 = jnp.where(has_next, next_ref[1:2, :].astype(F32), 0.0)
    xm1 = jnp.where(row == 0, prev, pltpu.roll(x, 1, 0))
    xp1 = jnp.where(row == tr - 1, nxt0, pltpu.roll(x, tr - 1, 0))
    xp2 = jnp.where(row == tr - 1, nxt1, jnp.where(row == tr - 2, nxt0, pltpu.roll(x, tr - 2, 0)))
    y = (w_ref[0:1, :] * xm1 + w_ref[1:2, :] * x + w_ref[2:3, :] * xp1 + w_ref[3:4, :] * xp2
         + b_ref[...])
    o_ref[...] = _silu(y).astype(o_ref.dtype)


def _ssd_conv(proj, conv_w, conv_b, dm, *, col0, tr, tn):
    rows = _rows(dm)
    m = proj.shape[0]
    n = conv_w.shape[1]
    tpb = rows // tr
    sub = tr // BF16_SUBLANES
    last = m // BF16_SUBLANES - 1
    c0 = col0 // tn
    kern = functools.partial(_conv_kernel, tr=tr, tiles_per_batch=tpb, ctx_tiles=dm.ctx // tr)
    return pl.pallas_call(
        kern,
        grid=(dm.batch, tpb, n // tn),
        in_specs=[pl.BlockSpec((tr, tn), lambda b, r, j: (b * tpb + r, c0 + j)),
                  pl.BlockSpec((BF16_SUBLANES, tn),
                               lambda b, r, j: (jnp.maximum((b * tpb + r) * sub - 1, 0), c0 + j)),
                  pl.BlockSpec((BF16_SUBLANES, tn),
                               lambda b, r, j: (jnp.minimum((b * tpb + r + 1) * sub, last), c0 + j)),
                  pl.BlockSpec((SSM_CONV, tn), lambda b, r, j: (0, j)),
                  pl.BlockSpec((1, tn), lambda b, r, j: (0, j))],
        out_specs=pl.BlockSpec((tr, tn), lambda b, r, j: (b * tpb + r, j)),
        out_shape=jax.ShapeDtypeStruct((m, n), BF16),
        compiler_params=_params(("parallel", "parallel", "parallel")),
        name="ssd_conv",
    )(proj, proj, proj, conv_w, conv_b.reshape(1, n))


def _split_dot(a, b_hi_exact, dims):
    hi = a.astype(BF16)
    lo = (a - hi.astype(F32)).astype(BF16)
    if dims == "ab":
        return (jnp.dot(hi, b_hi_exact, preferred_element_type=F32)
                + jnp.dot(lo, b_hi_exact, preferred_element_type=F32))
    return (jnp.dot(b_hi_exact, hi, preferred_element_type=F32)
            + jnp.dot(b_hi_exact, lo, preferred_element_type=F32))


def _ssd_kernel(x_ref, b_ref, c_ref, dt_ref, dtT_ref, bias_ref, biasT_ref, alog_ref, alogT_ref,
                expand_ref, y_ref, state_ref, *, heads, reverse):
    q = SSM_CHUNK
    hpg = heads // SSM_GROUPS
    gw = hpg * SSM_HEAD_DIM
    step = pl.program_id(1)

    @pl.when(step == 0)
    def _():
        state_ref[...] = jnp.zeros(state_ref.shape, F32)

    def softplus(v):
        return jnp.maximum(v, 0.0) + jnp.log(1.0 + jnp.exp(-jnp.abs(v)))

    dcol = heads if reverse else 0
    dt = softplus(dt_ref[:, dcol:dcol + heads] + bias_ref[...])
    dtT = softplus(dtT_ref[...] + biasT_ref[...])
    a = dt * -jnp.exp(alog_ref[...])
    aT = dtT * -jnp.exp(alogT_ref[...])
    ri = lax.broadcasted_iota(jnp.int32, (q, q), 0)
    ci = lax.broadcasted_iota(jnp.int32, (q, q), 1)
    before = (ci >= ri) if reverse else (ci <= ri)
    tri = before.astype(BF16)
    triT = ((ri >= ci) if reverse else (ri <= ci)).astype(BF16)
    cum = _split_dot(a, tri, "ba")
    cumT = _split_dot(aT, triT, "ab")
    end = 0 if reverse else q - 1
    total = cum[end:end + 1, :]
    expand = expand_ref[...]
    w_full = _split_dot(jnp.exp(total - cum) * dt, expand, "ab")
    e_full = _split_dot(jnp.exp(cum), expand, "ab")
    cd_full = _split_dot(jnp.broadcast_to(jnp.exp(total), (8, heads)), expand, "ab")[0:1]

    for g in range(SSM_GROUPS):
        gcols = slice(g * gw, (g + 1) * gw)
        ncols = slice(g * SSM_STATE, (g + 1) * SSM_STATE)
        bg = b_ref[:, ncols]
        cg = c_ref[:, ncols]
        xg = x_ref[:, gcols]
        cb = lax.dot_general(cg, bg, (((1,), (1,)), ((), ())), preferred_element_type=F32)
        st = state_ref[:, gcols]
        y_off = jnp.dot(cg, st.astype(BF16), preferred_element_type=F32) * e_full[:, gcols]
        for hh in range(hpg):
            h = g * hpg + hh
            diff = cum[:, h:h + 1] - cumT[h:h + 1, :]
            decay = jnp.exp(jnp.where(before, diff, -jnp.inf))
            mat = (cb * decay * dtT[h:h + 1, :]).astype(BF16)
            hcols = slice(hh * SSM_HEAD_DIM, (hh + 1) * SSM_HEAD_DIM)
            y_h = jnp.dot(mat, xg[:, hcols], preferred_element_type=F32)
            y_ref[:, g * gw + hh * SSM_HEAD_DIM:g * gw + (hh + 1) * SSM_HEAD_DIM] = (
                y_h + y_off[:, hcols]).astype(y_ref.dtype)
        xw = (xg.astype(F32) * w_full[:, gcols]).astype(BF16)
        bgT = bg.astype(F32).T.astype(BF16)
        state_ref[:, gcols] = st * cd_full[:, gcols] + jnp.dot(bgT, xw, preferred_element_type=F32)


def _ssd_scan(xbc, dt, dtT, dt_bias, a_log, dm, *, inner, reverse):
    rows = _rows(dm)
    m = xbc.shape[0]
    q = SSM_CHUNK
    heads = inner // SSM_HEAD_DIM
    nc = rows // q
    ctx_chunks = dm.ctx // q
    sw = SSM_GROUPS * SSM_STATE
    d = 1 if reverse else 0

    def chunk(b, s):
        if reverse:
            s = jnp.where(s < ctx_chunks, ctx_chunks - 1 - s, nc - 1 - (s - ctx_chunks))
        return b * nc + s

    expand = jnp.repeat(jnp.eye(heads, dtype=BF16), SSM_HEAD_DIM, axis=1)
    kern = functools.partial(_ssd_kernel, heads=heads, reverse=reverse)
    return pl.pallas_call(
        kern,
        grid=(dm.batch, nc),
        in_specs=[pl.BlockSpec((q, inner), lambda b, s: (chunk(b, s), 0)),
                  pl.BlockSpec((q, sw), lambda b, s: (chunk(b, s), inner // sw)),
                  pl.BlockSpec((q, sw), lambda b, s: (chunk(b, s), inner // sw + 1)),
                  pl.BlockSpec((q, 2 * heads), lambda b, s: (chunk(b, s), 0)),
                  pl.BlockSpec((heads, q), lambda b, s: (d, chunk(b, s))),
                  pl.BlockSpec((1, heads), lambda b, s: (0, 0)),
                  pl.BlockSpec((heads, 1), lambda b, s: (0, 0)),
                  pl.BlockSpec((1, heads), lambda b, s: (0, 0)),
                  pl.BlockSpec((heads, 1), lambda b, s: (0, 0)),
                  pl.BlockSpec((heads, inner), lambda b, s: (0, 0))],
        out_specs=pl.BlockSpec((q, inner), lambda b, s: (chunk(b, s), 0)),
        out_shape=jax.ShapeDtypeStruct((m, inner), BF16),
        scratch_shapes=[pltpu.VMEM((SSM_STATE, inner), F32)],
        compiler_params=_params(("parallel", "arbitrary")),
        name="ssd_scan_bwd" if reverse else "ssd_scan_fwd",
    )(xbc, xbc, xbc, dt, dtT, dt_bias[d][None, :], dt_bias[d][:, None], a_log[d][None, :],
      a_log[d][:, None], expand)


def _ssd_finish_kernel(yf_ref, yb_ref, x_ref, z_ref, d_ref, g_ref, o_ref, *, gw):
    for g in range(SSM_GROUPS):
        cols = slice(g * gw, (g + 1) * gw)
        y = (yf_ref[:, cols].astype(F32) + yb_ref[:, cols].astype(F32)
             + d_ref[:, cols] * x_ref[:, cols].astype(F32))
        vg = y * _silu(z_ref[:, cols].astype(F32))
        vg = vg * lax.rsqrt(jnp.mean(vg * vg, axis=-1, keepdims=True) + EPS) * g_ref[:, cols]
        o_ref[:, cols] = vg.astype(o_ref.dtype)


def _ssd_finish(y_f, y_b, xbc, proj, d_skip, ssm_norm, *, tr):
    m, inner = y_f.shape
    d_full = jnp.repeat(d_skip.astype(F32), SSM_HEAD_DIM).reshape(1, inner)
    row_spec = pl.BlockSpec((tr, inner), lambda i: (i, 0))
    vec_spec = pl.BlockSpec((1, inner), lambda i: (0, 0))
    kern = functools.partial(_ssd_finish_kernel, gw=inner // SSM_GROUPS)
    return pl.pallas_call(
        kern,
        grid=(m // tr,),
        in_specs=[row_spec, row_spec, row_spec, row_spec, vec_spec, vec_spec],
        out_specs=row_spec,
        out_shape=jax.ShapeDtypeStruct((m, inner), BF16),
        compiler_params=_params(("parallel",)),
        name="ssd_finish",
    )(y_f, y_b, xbc, proj, d_full, ssm_norm.reshape(1, inner))


def _final_norm_kernel(x_ref, g_ref, o_ref):
    x = x_ref[...]
    o_ref[...] = x * lax.rsqrt(jnp.mean(x * x, axis=-1, keepdims=True) + EPS) * g_ref[...]


def _final_norm(xc, gain, dm, *, tr):
    rows = _rows(dm)
    tpb = rows // tr
    lat_tiles = dm.seq // tr
    ctx_tiles = dm.ctx // tr
    return pl.pallas_call(
        _final_norm_kernel,
        grid=(dm.batch, lat_tiles),
        in_specs=[pl.BlockSpec((tr, dm.d), lambda b, i: (b * tpb + ctx_tiles + i, 0)),
                  pl.BlockSpec((1, dm.d), lambda b, i: (0, 0))],
        out_specs=pl.BlockSpec((tr, dm.d), lambda b, i: (b * lat_tiles + i, 0)),
        out_shape=jax.ShapeDtypeStruct((dm.batch * dm.seq, dm.d), F32),
        compiler_params=_params(("parallel", "parallel")),
        name="final_norm",
    )(xc, gain.reshape(1, dm.d))


def _rope_tables(dm):
    d_axis = HEAD_DIM // 2
    t = jnp.arange(dm.seq, dtype=jnp.int32)
    inv_freq = jnp.power(ROPE_THETA, -jnp.arange(0, d_axis, 2, dtype=F32) / d_axis)
    ang_r = (t // GRID_W).astype(F32)[:, None] * inv_freq[None, :]
    ang_c = (t % GRID_W).astype(F32)[:, None] * inv_freq[None, :]
    cos = jnp.concatenate([jnp.cos(ang_r), jnp.cos(ang_r), jnp.cos(ang_c), jnp.cos(ang_c)], axis=1)
    sin = jnp.concatenate([-jnp.sin(ang_r), jnp.sin(ang_r), -jnp.sin(ang_c), jnp.sin(ang_c)], axis=1)
    cos = jnp.concatenate([jnp.ones((dm.ctx, HEAD_DIM), F32), cos], axis=0)
    sin = jnp.concatenate([jnp.zeros((dm.ctx, HEAD_DIM), F32), sin], axis=0)
    return cos, sin


def _lambda_init(layer_idx):
    return 0.8 - 0.6 * math.exp(-0.3 * layer_idx)


def _tiles(dm):
    rows = _rows(dm)
    tm = rows // 2 if (rows // 2) % 128 == 0 else rows
    return dict(tm=tm, tn=min(512, dm.d), tq=dm.ctx, tk=dm.ctx)


def _forward(dm, x, c, ctx, c_ctx, layers, final_norm):
    d = dm.d
    til = _tiles(dm)
    tm, tn, tq, tk = til["tm"], til["tn"], til["tq"], til["tk"]
    scale = HEAD_DIM ** -0.5
    xc = jnp.concatenate([ctx, x], axis=1).reshape(dm.batch * _rows(dm), d)
    cvec = jnp.concatenate([c, c_ctx[None, :], jnp.zeros((MOD_ROWS - dm.batch - 1, d), F32)], axis=0)
    rope = _rope_tables(dm)
    q_tiles = d // tn

    for i, (kind, (w_mod, b_mod, norm), p) in enumerate(layers):
        mods = _modulation(cvec, w_mod, b_mod)
        if kind == "gqa":
            w_in, q_gain, k_gain, w_out = p
            kv_tiles = GQA_KV_HEADS * HEAD_DIM // tn
            n_tiles = w_in.shape[1] // tn
            segs = ((0, q_tiles, 0, True, scale),
                    (q_tiles, q_tiles + kv_tiles, 1, True, None),
                    (q_tiles + kv_tiles, n_tiles, None, False, None))
            proj = _in_proj(xc, norm, mods, w_in.astype(BF16), dm, tm=tm, tn=tn, segs=segs,
                            rope=rope, gains=jnp.stack([q_gain, k_gain]).astype(F32))
            g = _gqa_attention(proj, dm, tq=tq, tk=tk)
        elif kind == "diff":
            w_in, lq1, lk1, lq2, lk2, subln, w_out = p
            n_tiles = w_in.shape[1] // tn
            segs = ((0, q_tiles, None, True, scale),
                    (q_tiles, 2 * q_tiles, None, True, None),
                    (2 * q_tiles, n_tiles, None, False, None))
            proj = _in_proj(xc, norm, mods, w_in.astype(BF16), dm, tm=tm, tn=tn, segs=segs, rope=rope)
            lam_vecs = jnp.stack([lq1, lk1, lq2, lk2]).astype(F32)
            g = _diff_attention(proj, lam_vecs, subln, dm, tq=tq, tk=tk, lambda_init=_lambda_init(i))
        else:
            w_in, conv_w, conv_b, dt_bias, a_log, d_skip, ssm_norm, w_out = p
            inner = w_out.shape[0]
            n_main = 2 * inner + 2 * SSM_GROUPS * SSM_STATE
            segs = ((0, n_main // tn, None, False, None),)
            proj, dt, dtT = _in_proj(xc, norm, mods, w_in[:, :n_main].astype(BF16), dm, tm=tm, tn=tn,
                                     segs=segs, w_dt=w_in[:, n_main:].astype(BF16))
            xbc = _ssd_conv(proj, conv_w, conv_b, dm, col0=inner, tr=tq, tn=tn)
            y_f = _ssd_scan(xbc, dt, dtT, dt_bias, a_log, dm, inner=inner, reverse=False)
            y_b = _ssd_scan(xbc, dt, dtT, dt_bias, a_log, dm, inner=inner, reverse=True)
            g = _ssd_finish(y_f, y_b, xbc, proj, d_skip, ssm_norm, tr=tq)
        xc = _out_proj(g, w_out.astype(BF16), xc, mods, dm, tm=tm, tn=tn)

    out = _final_norm(xc, final_norm, dm, tr=tq)
    return out.reshape(dm.batch, dm.seq, d)


def kernel(x, c, ctx, c_ctx, l0_w_mod, l0_b_mod, l0_norm, l0_w_in, l0_q_gain, l0_k_gain, l0_w_out, l1_w_mod, l1_b_mod, l1_norm, l1_w_in, l1_lambda_q1, l1_lambda_k1, l1_lambda_q2, l1_lambda_k2, l1_subln, l1_w_out, l2_w_mod, l2_b_mod, l2_norm, l2_w_in, l2_conv_w, l2_conv_b, l2_dt_bias, l2_A_log, l2_D, l2_ssm_norm, l2_w_out, l3_w_mod, l3_b_mod, l3_norm, l3_w_in, l3_q_gain, l3_k_gain, l3_w_out, final_norm):
    dm = Dims(batch=x.shape[0], seq=x.shape[1], ctx=ctx.shape[1], d=x.shape[2])
    layers = (
        ("gqa", (l0_w_mod, l0_b_mod, l0_norm), (l0_w_in, l0_q_gain, l0_k_gain, l0_w_out)),
        ("diff", (l1_w_mod, l1_b_mod, l1_norm), (l1_w_in, l1_lambda_q1, l1_lambda_k1, l1_lambda_q2,
                                                 l1_lambda_k2, l1_subln, l1_w_out)),
        ("ssd", (l2_w_mod, l2_b_mod, l2_norm), (l2_w_in, l2_conv_w, l2_conv_b, l2_dt_bias, l2_A_log,
                                                l2_D, l2_ssm_norm, l2_w_out)),
        ("gqa", (l3_w_mod, l3_b_mod, l3_norm), (l3_w_in, l3_q_gain, l3_k_gain, l3_w_out)),
    )
    return _forward(dm, x, c, ctx, c_ctx, layers, final_norm)
```

```python
import collections
import functools
import math

import jax
import jax.numpy as jnp
from jax import lax
from jax.experimental import pallas as pl
from jax.experimental.pallas import tpu as pltpu

F32 = jnp.float32
BF16 = jnp.bfloat16

EPS = 1e-6
ROPE_THETA = 10000.0
HEAD_DIM = 128
GQA_KV_HEADS = 4
SSM_HEAD_DIM = 64
SSM_GROUPS = 8
SSM_STATE = 128
SSM_CONV = 4
SSM_CHUNK = 128
GRID_W = 64
DEPTH = 4
N_MODS = 3
MOD_ROWS = 8
NORM_SLAB = 64
BF16_SUBLANES = 16
VMEM_LIMIT = 56 * 1024 * 1024

Dims = collections.namedtuple("Dims", "batch seq ctx d")


def _rows(dm):
    return dm.ctx + dm.seq


def _silu(v):
    return v * (1.0 / (1.0 + jnp.exp(-v)))


def _params(semantics):
    return pltpu.CompilerParams(dimension_semantics=semantics, vmem_limit_bytes=VMEM_LIMIT)


def _mod_kernel(c_ref, w_ref, b_ref, o_ref):
    s = _silu(c_ref[...]).astype(BF16)
    o_ref[...] = jnp.dot(s, w_ref[...].astype(BF16), preferred_element_type=F32) + b_ref[...]


def _modulation(cvec, w_mod, b_mod, tn=512):
    d, n = w_mod.shape
    return pl.pallas_call(
        _mod_kernel,
        grid=(n // tn,),
        in_specs=[pl.BlockSpec((MOD_ROWS, d), lambda j: (0, 0)),
                  pl.BlockSpec((d, tn), lambda j: (0, j)),
                  pl.BlockSpec((1, tn), lambda j: (0, j))],
        out_specs=pl.BlockSpec((MOD_ROWS, tn), lambda j: (0, j)),
        out_shape=jax.ShapeDtypeStruct((MOD_ROWS, n), F32),
        compiler_params=_params(("arbitrary",)),
        name="adaln_mod",
    )(cvec, w_mod, b_mod.reshape(1, n))


def _rope_layout():
    quarter = HEAD_DIM // 4
    idx = jnp.arange(HEAD_DIM).reshape(4, quarter)
    return idx[jnp.array([0, 2, 1, 3])].reshape(HEAD_DIM)


def _head_sums(sq, tn):
    r = lax.broadcasted_iota(jnp.int32, (tn, tn), 0) // HEAD_DIM
    c = lax.broadcasted_iota(jnp.int32, (tn, tn), 1) // HEAD_DIM
    return jnp.dot(sq.astype(BF16), (r == c).astype(BF16), preferred_element_type=F32)


def _in_proj_kernel(*refs, tm, tn, tiles_per_batch, ctx_len, segs, has_rope, has_gain, has_dt):
    it = iter(refs)
    x_ref, g_ref, shl_ref, scl_ref, shc_ref, scc_ref, w_ref = (next(it) for _ in range(7))
    cos_ref = sin_ref = gain_ref = wdt_ref = wdtT_ref = dt_ref = dtT_ref = None
    if has_rope:
        cos_ref, sin_ref = next(it), next(it)
    if has_gain:
        gain_ref = next(it)
    if has_dt:
        wdt_ref, wdtT_ref = next(it), next(it)
    o_ref = next(it)
    if has_dt:
        dt_ref, dtT_ref = next(it), next(it)
    h_ref = next(it)

    i = pl.program_id(0)
    j = pl.program_id(1)

    @pl.when(j == 0)
    def _():
        def slab(r, carry):
            rows = pl.ds(pl.multiple_of(r * NORM_SLAB, NORM_SLAB), NORM_SLAB)
            x = x_ref[rows, :]
            y = x * lax.rsqrt(jnp.mean(x * x, axis=-1, keepdims=True) + EPS) * g_ref[...]
            row = ((i % tiles_per_batch) * tm + r * NORM_SLAB
                   + lax.broadcasted_iota(jnp.int32, (NORM_SLAB, 1), 0))
            is_ctx = row < ctx_len
            scale = jnp.where(is_ctx, scc_ref[0], scl_ref[0])
            shift = jnp.where(is_ctx, shc_ref[0], shl_ref[0])
            h_ref[rows, :] = (y * (1.0 + scale) + shift).astype(BF16)
            return carry

        lax.fori_loop(0, tm // NORM_SLAB, slab, 0)
        h = h_ref[...]
        if has_dt:
            dt_ref[...] = jnp.dot(h, wdt_ref[...], preferred_element_type=F32)
            dtT_ref[...] = lax.dot_general(wdtT_ref[...], h, (((1,), (1,)), ((), ())),
                                           preferred_element_type=F32)

    def finish(gain_idx, rope, scale):
        acc = jnp.dot(h_ref[...], w_ref[...].astype(BF16), preferred_element_type=F32)
        if gain_idx is None and not rope and scale is None:
            o_ref[...] = acc.astype(o_ref.dtype)
            return
        if gain_idx is not None:
            acc = acc * lax.rsqrt(_head_sums(acc * acc, tn) * (1.0 / HEAD_DIM) + EPS)
        for hh in range(tn // HEAD_DIM):
            cols = slice(hh * HEAD_DIM, (hh + 1) * HEAD_DIM)
            v = acc[:, cols]
            if gain_idx is not None:
                v = v * gain_ref[gain_idx:gain_idx + 1, :]
            if rope:
                v = v * cos_ref[...] + pltpu.roll(v, HEAD_DIM // 2, 1) * sin_ref[...]
            if scale is not None:
                v = v * scale
            o_ref[:, cols] = v.astype(o_ref.dtype)

    for (lo, hi, gain_idx, rope, scale) in segs:
        pl.when((j >= lo) & (j < hi))(functools.partial(finish, gain_idx, rope, scale))


def _in_proj(xc, norm_g, mods, w, dm, *, tm, tn, segs, rope=None, gains=None, w_dt=None):
    m, d = xc.shape
    n = segs[-1][1] * tn
    tpb = _rows(dm) // tm
    mods3 = mods.reshape(MOD_ROWS, 1, N_MODS * d)
    has_rope, has_gain, has_dt = rope is not None, gains is not None, w_dt is not None

    def lat(col):
        return pl.BlockSpec((1, 1, d), lambda i, j: (i // tpb, 0, col))

    def ctx(col):
        return pl.BlockSpec((1, 1, d), lambda i, j: (dm.batch, 0, col))

    args = [xc, norm_g.reshape(1, d), mods3, mods3, mods3, mods3, w]
    in_specs = [pl.BlockSpec((tm, d), lambda i, j: (i, 0)),
                pl.BlockSpec((1, d), lambda i, j: (0, 0)),
                lat(0), lat(1), ctx(0), ctx(1),
                pl.BlockSpec((d, tn), lambda i, j: (0, j))]
    if has_rope:
        args += list(rope)
        in_specs += [pl.BlockSpec((tm, HEAD_DIM), lambda i, j: (i % tpb, 0))] * 2
    if has_gain:
        args.append(gains)
        in_specs.append(pl.BlockSpec(gains.shape, lambda i, j: (0, 0)))
    out_shape = [jax.ShapeDtypeStruct((m, n), BF16)]
    out_specs = [pl.BlockSpec((tm, tn), lambda i, j: (i, j))]
    if has_dt:
        n_dt = w_dt.shape[1]
        args += [w_dt, w_dt.T]
        in_specs += [pl.BlockSpec((d, n_dt), lambda i, j: (0, 0)),
                     pl.BlockSpec((n_dt, d), lambda i, j: (0, 0))]
        out_shape += [jax.ShapeDtypeStruct((m, n_dt), F32), jax.ShapeDtypeStruct((n_dt, m), F32)]
        out_specs += [pl.BlockSpec((tm, n_dt), lambda i, j: (i, 0)),
                      pl.BlockSpec((n_dt, tm), lambda i, j: (0, i))]
    kern = functools.partial(_in_proj_kernel, tm=tm, tn=tn, tiles_per_batch=tpb, ctx_len=dm.ctx,
                             segs=segs, has_rope=has_rope, has_gain=has_gain, has_dt=has_dt)
    out = pl.pallas_call(
        kern,
        grid=(m // tm, n // tn),
        in_specs=in_specs,
        out_specs=out_specs,
        out_shape=out_shape,
        scratch_shapes=[pltpu.VMEM((tm, d), BF16)],
        compiler_params=_params(("parallel", "arbitrary")),
        name="in_proj",
    )(*args)
    return out if has_dt else out[0]


def _out_proj_kernel(g_ref, w_ref, x_ref, gl_ref, gc_ref, o_ref, *, tm, tiles_per_batch, ctx_len):
    i = pl.program_id(0)
    acc = jnp.dot(g_ref[...], w_ref[...].astype(BF16), preferred_element_type=F32)
    row = (i % tiles_per_batch) * tm + lax.broadcasted_iota(jnp.int32, (tm, 1), 0)
    gate = jnp.where(row < ctx_len, gc_ref[0], gl_ref[0])
    o_ref[...] = x_ref[...] + gate * acc


def _out_proj(g, w, xc, mods, dm, *, tm, tn):
    m, k = g.shape
    d = w.shape[1]
    tpb = _rows(dm) // tm
    tiles_n = d // tn
    mods3 = mods.reshape(MOD_ROWS, 1, N_MODS * d)
    gate_col = 2 * tiles_n
    kern = functools.partial(_out_proj_kernel, tm=tm, tiles_per_batch=tpb, ctx_len=dm.ctx)
    return pl.pallas_call(
        kern,
        grid=(m // tm, tiles_n),
        in_specs=[pl.BlockSpec((tm, k), lambda i, j: (i, 0)),
                  pl.BlockSpec((k, tn), lambda i, j: (0, j)),
                  pl.BlockSpec((tm, tn), lambda i, j: (i, j)),
                  pl.BlockSpec((1, 1, tn), lambda i, j: (i // tpb, 0, gate_col + j)),
                  pl.BlockSpec((1, 1, tn), lambda i, j: (dm.batch, 0, gate_col + j))],
        out_specs=pl.BlockSpec((tm, tn), lambda i, j: (i, j)),
        out_shape=jax.ShapeDtypeStruct((m, d), F32),
        compiler_params=_params(("parallel", "arbitrary")),
        name="out_proj",
    )(g, w, xc, mods3, mods3)


def _lane_tile_reduce(v, op):
    part = v[:, :HEAD_DIM]
    for t in range(1, v.shape[1] // HEAD_DIM):
        part = op(part, v[:, t * HEAD_DIM:(t + 1) * HEAD_DIM])
    return part


def _scores(pairs, k_ref, krows):
    blocks = []
    for q, kcols in pairs:
        half = q.shape[0] // 2
        for qq in (q[:half], q[half:]):
            blocks.append(lax.dot_general(qq, k_ref[krows, kcols], (((1,), (1,)), ((), ())),
                                          preferred_element_type=F32))
    return jnp.concatenate(blocks, axis=0)


def _weighted(p, v_ref, krows):
    half = p.shape[0] // 2
    pb = p.astype(BF16)
    return jnp.concatenate([jnp.dot(pp, v_ref[krows, :], preferred_element_type=F32)
                            for pp in (pb[:half], pb[half:])], axis=0)


def _attend_in_place(pairs, k_ref, v_ref, key_rows, tk):
    chunks = [slice(c * tk, (c + 1) * tk) for c in range(key_rows // tk)]
    s = [_scores(pairs, k_ref, krows) for krows in chunks]
    m_part = functools.reduce(jnp.maximum, [_lane_tile_reduce(sc, jnp.maximum) for sc in s])
    m = jnp.max(m_part, axis=1, keepdims=True)
    p = [jnp.exp2(sc - m) for sc in s]
    l_part = functools.reduce(jnp.add, [_lane_tile_reduce(pc, jnp.add) for pc in p])
    acc = functools.reduce(jnp.add, [_weighted(pc, v_ref, krows) for pc, krows in zip(p, chunks)])
    return acc * (1.0 / jnp.sum(l_part, axis=1, keepdims=True))


def _attend_pipelined(pairs_next, s_next_ref, s_cur_ref, k_ref, v_ref, m_ref, key_rows, tk):
    n_chunks = key_rows // tk
    have_cur = s_cur_ref is not None
    m_part = l_part = acc = None
    for c in range(n_chunks):
        krows = slice(c * tk, (c + 1) * tk)
        if pairs_next is not None:
            s = _scores(pairs_next, k_ref, krows)
            s_next_ref[c] = s
            part = _lane_tile_reduce(s, jnp.maximum)
            m_part = part if m_part is None else jnp.maximum(m_part, part)
        if have_cur:
            p = jnp.concatenate(
                [jnp.exp2(s_cur_ref[c, :, t * HEAD_DIM:(t + 1) * HEAD_DIM] - m_ref[...])
                 for t in range(tk // HEAD_DIM)], axis=1)
            part = _lane_tile_reduce(p, jnp.add)
            l_part = part if l_part is None else l_part + part
            pv = _weighted(p, v_ref, krows)
            acc = pv if acc is None else acc + pv
    if pairs_next is not None:
        m_ref[...] = jnp.broadcast_to(jnp.max(m_part, axis=1, keepdims=True), m_ref.shape)
    if not have_cur:
        return None
    return acc * (1.0 / jnp.sum(l_part, axis=1, keepdims=True))


def _attention_steps(step, queries, finish, k_ref, v_ref, s_refs, m_ref, *, tq, tk, ctx_len, rows):
    n_lat = (rows - ctx_len) // tq

    def lat_row0(t):
        return pl.multiple_of(ctx_len + (t - 1) * tq, math.gcd(ctx_len, tq))

    def pipe(t_next, t_cur, cur_parity):
        pairs = None if t_next is None else queries(lat_row0(t_next), tq)
        o = _attend_pipelined(pairs, None if t_next is None else s_refs[1 - cur_parity],
                              None if t_cur is None else s_refs[cur_parity],
                              k_ref, v_ref, m_ref, rows, tk)
        if t_cur is not None:
            finish(lat_row0(t_cur), tq, o)

    @pl.when(step == 0)
    def _():
        finish(0, ctx_len, _attend_in_place(queries(0, ctx_len), k_ref, v_ref, ctx_len, tk))
        pipe(1, None, 0)

    if n_lat > 1:
        for parity in range(2):
            pl.when((step >= 1) & (step < n_lat) & (step % 2 == parity))(
                functools.partial(pipe, step + 1, step, parity))
    pl.when(step == n_lat)(functools.partial(pipe, None, step, n_lat % 2))


def _gqa_kernel(q_ref, k_ref, v_ref, z_ref, o_ref, s0_ref, s1_ref, m_ref, *, tq, tk, group, ctx_len, rows):
    def queries(row0, nq):
        q = jnp.concatenate([q_ref[pl.ds(row0, nq), g * HEAD_DIM:(g + 1) * HEAD_DIM]
                             for g in range(group)], axis=0)
        return [(q, slice(0, HEAD_DIM))]

    def finish(row0, nq, o):
        qrows = pl.ds(row0, nq)
        for g in range(group):
            cols = slice(g * HEAD_DIM, (g + 1) * HEAD_DIM)
            z = z_ref[qrows, cols].astype(F32)
            o_ref[qrows, cols] = (o[g * nq:(g + 1) * nq] * _silu(z)).astype(o_ref.dtype)

    _attention_steps(pl.program_id(2), queries, finish, k_ref, v_ref, (s0_ref, s1_ref), m_ref,
                     tq=tq, tk=tk, ctx_len=ctx_len, rows=rows)


def _gqa_attention(proj, dm, *, tq, tk):
    rows = _rows(dm)
    m = proj.shape[0]
    width = dm.d
    heads = width // HEAD_DIM
    group = heads // GQA_KV_HEADS
    gw = group * HEAD_DIM
    k_col0 = width // HEAD_DIM
    v_col0 = k_col0 + GQA_KV_HEADS
    z_col0 = (width + 2 * GQA_KV_HEADS * HEAD_DIM) // gw
    kern = functools.partial(_gqa_kernel, tq=tq, tk=tk, group=group, ctx_len=dm.ctx, rows=rows)
    return pl.pallas_call(
        kern,
        grid=(dm.batch, GQA_KV_HEADS, 1 + dm.seq // tq),
        in_specs=[pl.BlockSpec((rows, gw), lambda b, h, i: (b, h)),
                  pl.BlockSpec((rows, HEAD_DIM), lambda b, h, i: (b, k_col0 + h)),
                  pl.BlockSpec((rows, HEAD_DIM), lambda b, h, i: (b, v_col0 + h)),
                  pl.BlockSpec((rows, gw), lambda b, h, i: (b, z_col0 + h))],
        out_specs=pl.BlockSpec((rows, gw), lambda b, h, i: (b, h)),
        out_shape=jax.ShapeDtypeStruct((m, width), BF16),
        scratch_shapes=[pltpu.VMEM((rows // tk, group * tq, tk), F32),
                        pltpu.VMEM((rows // tk, group * tq, tk), F32),
                        pltpu.VMEM((group * tq, HEAD_DIM), F32)],
        compiler_params=_params(("parallel", "parallel", "arbitrary")),
        name="gqa_attention",
    )(proj, proj, proj, proj)


def _diff_kernel(lam_ref, q_ref, k_ref, v_ref, z_ref, subln_ref, o_ref, s0_ref, s1_ref, m_ref,
                 *, tq, tk, ctx_len, rows, lambda_init):
    lv = lam_ref[...]
    lam = (jnp.exp(jnp.sum(lv[0:1] * lv[1:2], axis=1, keepdims=True))
           - jnp.exp(jnp.sum(lv[2:3] * lv[3:4], axis=1, keepdims=True)) + lambda_init)

    def queries(row0, nq):
        halves = [slice(t * HEAD_DIM, (t + 1) * HEAD_DIM) for t in range(2)]
        return [(q_ref[pl.ds(row0, nq), cols], cols) for cols in halves]

    def finish(row0, nq, o):
        qrows = pl.ds(row0, nq)
        o = o[:nq] - lam * o[nq:]
        o = o * lax.rsqrt(jnp.mean(o * o, axis=-1, keepdims=True) + EPS) * subln_ref[...]
        o = o * (1.0 - lambda_init)
        o_ref[qrows, :] = (o * _silu(z_ref[qrows, :].astype(F32))).astype(o_ref.dtype)

    _attention_steps(pl.program_id(2), queries, finish, k_ref, v_ref, (s0_ref, s1_ref), m_ref,
                     tq=tq, tk=tk, ctx_len=ctx_len, rows=rows)


def _diff_attention(proj, lam_vecs, subln, dm, *, tq, tk, lambda_init):
    rows = _rows(dm)
    m = proj.shape[0]
    width = dm.d
    hw = 2 * HEAD_DIM
    heads = width // hw
    kern = functools.partial(_diff_kernel, tq=tq, tk=tk, ctx_len=dm.ctx, rows=rows, lambda_init=lambda_init)

    def head_block(first):
        return pl.BlockSpec((rows, hw), lambda b, h, i: (b, first + h))

    return pl.pallas_call(
        kern,
        grid=(dm.batch, heads, 1 + dm.seq // tq),
        in_specs=[pl.BlockSpec(lam_vecs.shape, lambda b, h, i: (0, 0)),
                  head_block(0), head_block(heads), head_block(2 * heads), head_block(3 * heads),
                  pl.BlockSpec((1, hw), lambda b, h, i: (0, 0))],
        out_specs=head_block(0),
        out_shape=jax.ShapeDtypeStruct((m, width), BF16),
        scratch_shapes=[pltpu.VMEM((rows // tk, 2 * tq, tk), F32),
                        pltpu.VMEM((rows // tk, 2 * tq, tk), F32),
                        pltpu.VMEM((2 * tq, HEAD_DIM), F32)],
        compiler_params=_params(("parallel", "parallel", "arbitrary")),
        name="diff_attention",
    )(lam_vecs, proj, proj, proj, proj, subln.reshape(1, hw))


def _conv_kernel(cur_ref, prev_ref, next_ref, w_ref, b_ref, o_ref, *, tr, tiles_per_batch, ctx_tiles):
    r = pl.program_id(1)
    x = cur_ref[...].astype(F32)
    row = lax.broadcasted_iota(jnp.int32, x.shape, 0)
    has_prev = (r != 0) & (r != ctx_tiles)
    has_next = (r != ctx_tiles - 1) & (r != tiles_per_batch - 1)
    prev = jnp.where(has_prev, prev_ref[BF16_SUBLANES - 1:BF16_SUBLANES, :].astype(F32), 0.0)
    nxt0 = jnp.where(has_next, next_ref[0:1, :].astype(F32), 0.0)
    nxt1 = jnp.where(has_next, next_ref[1:2, :].astype(F32), 0.0)
    xm1 = jnp.where(row == 0, prev, pltpu.roll(x, 1, 0))
    xp1 = jnp.where(row == tr - 1, nxt0, pltpu.roll(x, tr - 1, 0))
    xp2 = jnp.where(row == tr - 1, nxt1, jnp.where(row == tr - 2, nxt0, pltpu.roll(x, tr - 2, 0)))
    y = (w_ref[0:1, :] * xm1 + w_ref[1:2, :] * x + w_ref[2:3, :] * xp1 + w_ref[3:4, :] * xp2
         + b_ref[...])
    o_ref[...] = _silu(y).astype(o_ref.dtype)


def _ssd_conv(proj, conv_w, conv_b, dm, *, col0, tr, tn):
    rows = _rows(dm)
    m = proj.shape[0]
    n = conv_w.shape[1]
    tpb = rows // tr
    sub = tr // BF16_SUBLANES
    last = m // BF16_SUBLANES - 1
    c0 = col0 // tn
    kern = functools.partial(_conv_kernel, tr=tr, tiles_per_batch=tpb, ctx_tiles=dm.ctx // tr)
    return pl.pallas_call(
        kern,
        grid=(dm.batch, tpb, n // tn),
        in_specs=[pl.BlockSpec((tr, tn), lambda b, r, j: (b * tpb + r, c0 + j)),
                  pl.BlockSpec((BF16_SUBLANES, tn),
                               lambda b, r, j: (jnp.maximum((b * tpb + r) * sub - 1, 0), c0 + j)),
                  pl.BlockSpec((BF16_SUBLANES, tn),
                               lambda b, r, j: (jnp.minimum((b * tpb + r + 1) * sub, last), c0 + j)),
                  pl.BlockSpec((SSM_CONV, tn), lambda b, r, j: (0, j)),
                  pl.BlockSpec((1, tn), lambda b, r, j: (0, j))],
        out_specs=pl.BlockSpec((tr, tn), lambda b, r, j: (b * tpb + r, j)),
        out_shape=jax.ShapeDtypeStruct((m, n), BF16),
        compiler_params=_params(("parallel", "parallel", "parallel")),
        name="ssd_conv",
    )(proj, proj, proj, conv_w, conv_b.reshape(1, n))


def _split_dot(a, b_hi_exact, dims):
    hi = a.astype(BF16)
    lo = (a - hi.astype(F32)).astype(BF16)
    if dims == "ab":
        return (jnp.dot(hi, b_hi_exact, preferred_element_type=F32)
                + jnp.dot(lo, b_hi_exact, preferred_element_type=F32))
    return (jnp.dot(b_hi_exact, hi, preferred_element_type=F32)
            + jnp.dot(b_hi_exact, lo, preferred_element_type=F32))


def _ssd_kernel(x_ref, b_ref, c_ref, dt_ref, dtT_ref, bias_ref, biasT_ref, alog_ref, alogT_ref,
                expand_ref, y_ref, state_ref, *, heads, reverse):
    q = SSM_CHUNK
    hpg = heads // SSM_GROUPS
    gw = hpg * SSM_HEAD_DIM
    step = pl.program_id(1)

    @pl.when(step == 0)
    def _():
        state_ref[...] = jnp.zeros(state_ref.shape, F32)

    def softplus(v):
        return jnp.maximum(v, 0.0) + jnp.log(1.0 + jnp.exp(-jnp.abs(v)))

    dcol = heads if reverse else 0
    dt = softplus(dt_ref[:, dcol:dcol + heads] + bias_ref[...])
    dtT = softplus(dtT_ref[...] + biasT_ref[...])
    a = dt * -jnp.exp(alog_ref[...])
    aT = dtT * -jnp.exp(alogT_ref[...])
    ri = lax.broadcasted_iota(jnp.int32, (q, q), 0)
    ci = lax.broadcasted_iota(jnp.int32, (q, q), 1)
    before = (ci >= ri) if reverse else (ci <= ri)
    tri = before.astype(BF16)
    triT = ((ri >= ci) if reverse else (ri <= ci)).astype(BF16)
    cum = _split_dot(a, tri, "ba")
    cumT = _split_dot(aT, triT, "ab")
    end = 0 if reverse else q - 1
    total = cum[end:end + 1, :]
    expand = expand_ref[...]
    w_full = _split_dot(jnp.exp(total - cum) * dt, expand, "ab")
    e_full = _split_dot(jnp.exp(cum), expand, "ab")
    cd_full = _split_dot(jnp.broadcast_to(jnp.exp(total), (8, heads)), expand, "ab")[0:1]

    for g in range(SSM_GROUPS):
        gcols = slice(g * gw, (g + 1) * gw)
        ncols = slice(g * SSM_STATE, (g + 1) * SSM_STATE)
        bg = b_ref[:, ncols]
        cg = c_ref[:, ncols]
        xg = x_ref[:, gcols]
        cb = lax.dot_general(cg, bg, (((1,), (1,)), ((), ())), preferred_element_type=F32)
        st = state_ref[:, gcols]
        y_off = jnp.dot(cg, st.astype(BF16), preferred_element_type=F32) * e_full[:, gcols]
        for hh in range(hpg):
            h = g * hpg + hh
            diff = cum[:, h:h + 1] - cumT[h:h + 1, :]
            decay = jnp.exp(jnp.where(before, diff, -jnp.inf))
            mat = (cb * decay * dtT[h:h + 1, :]).astype(BF16)
            hcols = slice(hh * SSM_HEAD_DIM, (hh + 1) * SSM_HEAD_DIM)
            y_h = jnp.dot(mat, xg[:, hcols], preferred_element_type=F32)
            y_ref[:, g * gw + hh * SSM_HEAD_DIM:g * gw + (hh + 1) * SSM_HEAD_DIM] = (
                y_h + y_off[:, hcols]).astype(y_ref.dtype)
        xw = (xg.astype(F32) * w_full[:, gcols]).astype(BF16)
        bgT = bg.astype(F32).T.astype(BF16)
        state_ref[:, gcols] = st * cd_full[:, gcols] + jnp.dot(bgT, xw, preferred_element_type=F32)


def _ssd_scan(xbc, dt, dtT, dt_bias, a_log, dm, *, inner, reverse):
    rows = _rows(dm)
    m = xbc.shape[0]
    q = SSM_CHUNK
    heads = inner // SSM_HEAD_DIM
    nc = rows // q
    ctx_chunks = dm.ctx // q
    sw = SSM_GROUPS * SSM_STATE
    d = 1 if reverse else 0

    def chunk(b, s):
        if reverse:
            s = jnp.where(s < ctx_chunks, ctx_chunks - 1 - s, nc - 1 - (s - ctx_chunks))
        return b * nc + s

    expand = jnp.repeat(jnp.eye(heads, dtype=BF16), SSM_HEAD_DIM, axis=1)
    kern = functools.partial(_ssd_kernel, heads=heads, reverse=reverse)
    return pl.pallas_call(
        kern,
        grid=(dm.batch, nc),
        in_specs=[pl.BlockSpec((q, inner), lambda b, s: (chunk(b, s), 0)),
                  pl.BlockSpec((q, sw), lambda b, s: (chunk(b, s), inner // sw)),
                  pl.BlockSpec((q, sw), lambda b, s: (chunk(b, s), inner // sw + 1)),
                  pl.BlockSpec((q, 2 * heads), lambda b, s: (chunk(b, s), 0)),
                  pl.BlockSpec((heads, q), lambda b, s: (d, chunk(b, s))),
                  pl.BlockSpec((1, heads), lambda b, s: (0, 0)),
                  pl.BlockSpec((heads, 1), lambda b, s: (0, 0)),
                  pl.BlockSpec((1, heads), lambda b, s: (0, 0)),
                  pl.BlockSpec((heads, 1), lambda b, s: (0, 0)),
                  pl.BlockSpec((heads, inner), lambda b, s: (0, 0))],
        out_specs=pl.BlockSpec((q, inner), lambda b, s: (chunk(b, s), 0)),
        out_shape=jax.ShapeDtypeStruct((m, inner), BF16),
        scratch_shapes=[pltpu.VMEM((SSM_STATE, inner), F32)],
        compiler_params=_params(("parallel", "arbitrary")),
        name="ssd_scan_bwd" if reverse else "ssd_scan_fwd",
    )(xbc, xbc, xbc, dt, dtT, dt_bias[d][None, :], dt_bias[d][:, None], a_log[d][None, :],
      a_log[d][:, None], expand)


def _ssd_finish_kernel(yf_ref, yb_ref, x_ref, z_ref, d_ref, g_ref, o_ref, *, gw):
    for g in range(SSM_GROUPS):
        cols = slice(g * gw, (g + 1) * gw)
        y = (yf_ref[:, cols].astype(F32) + yb_ref[:, cols].astype(F32)
             + d_ref[:, cols] * x_ref[:, cols].astype(F32))
        vg = y * _silu(z_ref[:, cols].astype(F32))
        vg = vg * lax.rsqrt(jnp.mean(vg * vg, axis=-1, keepdims=True) + EPS) * g_ref[:, cols]
        o_ref[:, cols] = vg.astype(o_ref.dtype)


def _ssd_finish(y_f, y_b, xbc, proj, d_skip, ssm_norm, *, tr):
    m, inner = y_f.shape
    d_full = jnp.repeat(d_skip.astype(F32), SSM_HEAD_DIM).reshape(1, inner)
    row_spec = pl.BlockSpec((tr, inner), lambda i: (i, 0))
    vec_spec = pl.BlockSpec((1, inner), lambda i: (0, 0))
    kern = functools.partial(_ssd_finish_kernel, gw=inner // SSM_GROUPS)
    return pl.pallas_call(
        kern,
        grid=(m // tr,),
        in_specs=[row_spec, row_spec, row_spec, row_spec, vec_spec, vec_spec],
        out_specs=row_spec,
        out_shape=jax.ShapeDtypeStruct((m, inner), BF16),
        compiler_params=_params(("parallel",)),
        name="ssd_finish",
    )(y_f, y_b, xbc, proj, d_full, ssm_norm.reshape(1, inner))


def _final_norm_kernel(x_ref, g_ref, o_ref):
    x = x_ref[...]
    o_ref[...] = x * lax.rsqrt(jnp.mean(x * x, axis=-1, keepdims=True) + EPS) * g_ref[...]


def _final_norm(xc, gain, dm, *, tr):
    rows = _rows(dm)
    tpb = rows // tr
    lat_tiles = dm.seq // tr
    ctx_tiles = dm.ctx // tr
    return pl.pallas_call(
        _final_norm_kernel,
        grid=(dm.batch, lat_tiles),
        in_specs=[pl.BlockSpec((tr, dm.d), lambda b, i: (b * tpb + ctx_tiles + i, 0)),
                  pl.BlockSpec((1, dm.d), lambda b, i: (0, 0))],
        out_specs=pl.BlockSpec((tr, dm.d), lambda b, i: (b * lat_tiles + i, 0)),
        out_shape=jax.ShapeDtypeStruct((dm.batch * dm.seq, dm.d), F32),
        compiler_params=_params(("parallel", "parallel")),
        name="final_norm",
    )(xc, gain.reshape(1, dm.d))


def _rope_tables(dm):
    d_axis = HEAD_DIM // 2
    t = jnp.arange(dm.seq, dtype=jnp.int32)
    inv_freq = jnp.power(ROPE_THETA, -jnp.arange(0, d_axis, 2, dtype=F32) / d_axis)
    ang_r = (t // GRID_W).astype(F32)[:, None] * inv_freq[None, :]
    ang_c = (t % GRID_W).astype(F32)[:, None] * inv_freq[None, :]
    cos = jnp.concatenate([jnp.cos(ang_r), jnp.cos(ang_c), jnp.cos(ang_r), jnp.cos(ang_c)], axis=1)
    sin = jnp.concatenate([-jnp.sin(ang_r), -jnp.sin(ang_c), jnp.sin(ang_r), jnp.sin(ang_c)], axis=1)
    cos = jnp.concatenate([jnp.ones((dm.ctx, HEAD_DIM), F32), cos], axis=0)
    sin = jnp.concatenate([jnp.zeros((dm.ctx, HEAD_DIM), F32), sin], axis=0)
    return cos, sin


def _qk_weights(w_in, qk_cols):
    d = w_in.shape[0]
    qk = w_in[:, :qk_cols].reshape(d, qk_cols // HEAD_DIM, HEAD_DIM)[:, :, _rope_layout()]
    return jnp.concatenate([qk.reshape(d, qk_cols).astype(BF16), w_in[:, qk_cols:].astype(BF16)], axis=1)


def _lambda_init(layer_idx):
    return 0.8 - 0.6 * math.exp(-0.3 * layer_idx)


def _tiles(dm):
    rows = _rows(dm)
    tm = rows // 2 if (rows // 2) % 128 == 0 else rows
    return dict(tm=tm, tn=min(512, dm.d), tq=dm.ctx, tk=dm.ctx)


def _forward(dm, x, c, ctx, c_ctx, layers, final_norm):
    d = dm.d
    til = _tiles(dm)
    tm, tn, tq, tk = til["tm"], til["tn"], til["tq"], til["tk"]
    scale = HEAD_DIM ** -0.5 * math.log2(math.e)
    xc =jnp.concatenate([ctx, x], axis=1).reshape(dm.batch * _rows(dm), d)
    cvec = jnp.concatenate([c, c_ctx[None, :], jnp.zeros((MOD_ROWS - dm.batch - 1, d), F32)], axis=0)
    rope = _rope_tables(dm)
    q_tiles = d // tn

    for i, (kind, (w_mod, b_mod, norm), p) in enumerate(layers):
        mods = _modulation(cvec, w_mod, b_mod)
        if kind == "gqa":
            w_in, q_gain, k_gain, w_out = p
            kv_tiles = GQA_KV_HEADS * HEAD_DIM // tn
            n_tiles = w_in.shape[1] // tn
            segs = ((0, q_tiles, 0, True, scale),
                    (q_tiles, q_tiles + kv_tiles, 1, True, None),
                    (q_tiles + kv_tiles, n_tiles, None, False, None))
            w_qk = _qk_weights(w_in, (q_tiles + kv_tiles) * tn)
            gains = jnp.stack([q_gain, k_gain]).astype(F32)[:, _rope_layout()]
            proj = _in_proj(xc, norm, mods, w_qk, dm, tm=tm, tn=tn, segs=segs, rope=rope, gains=gains)
            g = _gqa_attention(proj, dm, tq=tq, tk=tk)
        elif kind == "diff":
            w_in, lq1, lk1, lq2, lk2, subln, w_out = p
            n_tiles = w_in.shape[1] // tn
            segs = ((0, q_tiles, None, True, scale),
                    (q_tiles, 2 * q_tiles, None, True, None),
                    (2 * q_tiles, n_tiles, None, False, None))
            proj = _in_proj(xc, norm, mods, _qk_weights(w_in, 2 * q_tiles * tn), dm, tm=tm, tn=tn,
                            segs=segs, rope=rope)
            lam_vecs = jnp.stack([lq1, lk1, lq2, lk2]).astype(F32)
            g = _diff_attention(proj, lam_vecs, subln, dm, tq=2 * tq, tk=tk, lambda_init=_lambda_init(i))
        else:
            w_in, conv_w, conv_b, dt_bias, a_log, d_skip, ssm_norm, w_out = p
            inner = w_out.shape[0]
            n_main = 2 * inner + 2 * SSM_GROUPS * SSM_STATE
            segs = ((0, n_main // tn, None, False, None),)
            proj, dt, dtT = _in_proj(xc, norm, mods, w_in, dm, tm=tm, tn=tn,
                                     segs=segs, w_dt=w_in[:, n_main:].astype(BF16))
            xbc = _ssd_conv(proj, conv_w, conv_b, dm, col0=inner, tr=tq, tn=tn)
            y_f = _ssd_scan(xbc, dt, dtT, dt_bias, a_log, dm, inner=inner, reverse=False)
            y_b = _ssd_scan(xbc, dt, dtT, dt_bias, a_log, dm, inner=inner, reverse=True)
            g = _ssd_finish(y_f, y_b, xbc, proj, d_skip, ssm_norm, tr=tq)
        xc = _out_proj(g, w_out, xc, mods, dm, tm=tm, tn=tn)

    out = _final_norm(xc, final_norm, dm, tr=tq)
    return out.reshape(dm.batch, dm.seq, d)


def kernel(x, c, ctx, c_ctx, l0_w_mod, l0_b_mod, l0_norm, l0_w_in, l0_q_gain, l0_k_gain, l0_w_out, l1_w_mod, l1_b_mod, l1_norm, l1_w_in, l1_lambda_q1, l1_lambda_k1, l1_lambda_q2, l1_lambda_k2, l1_subln, l1_w_out, l2_w_mod, l2_b_mod, l2_norm, l2_w_in, l2_conv_w, l2_conv_b, l2_dt_bias, l2_A_log, l2_D, l2_ssm_norm, l2_w_out, l3_w_mod, l3_b_mod, l3_norm, l3_w_in, l3_q_gain, l3_k_gain, l3_w_out, final_norm):
    dm = Dims(batch=x.shape[0], seq=x.shape[1], ctx=ctx.shape[1], d=x.shape[2])
    layers = (
        ("gqa", (l0_w_mod, l0_b_mod, l0_norm), (l0_w_in, l0_q_gain, l0_k_gain, l0_w_out)),
        ("diff", (l1_w_mod, l1_b_mod, l1_norm), (l1_w_in, l1_lambda_q1, l1_lambda_k1, l1_lambda_q2,
                                                 l1_lambda_k2, l1_subln, l1_w_out)),
        ("ssd", (l2_w_mod, l2_b_mod, l2_norm), (l2_w_in, l2_conv_w, l2_conv_b, l2_dt_bias, l2_A_log,
                                                l2_D, l2_ssm_norm, l2_w_out)),
        ("gqa", (l3_w_mod, l3_b_mod, l3_norm), (l3_w_in, l3_q_gain, l3_k_gain, l3_w_out)),
    )
    return _forward(dm, x, c, ctx, c_ctx, layers, final_norm)
```

```python
import collections
import functools
import math

import jax
import jax.numpy as jnp
from jax import lax
from jax.experimental import pallas as pl
from jax.experimental.pallas import tpu as pltpu

F32 = jnp.float32
BF16 = jnp.bfloat16

EPS = 1e-6
ROPE_THETA = 10000.0
HEAD_DIM = 128
GQA_KV_HEADS = 4
SSM_HEAD_DIM = 64
SSM_GROUPS = 8
SSM_STATE = 128
SSM_CONV = 4
SSM_CHUNK = 128
GRID_W = 64
DEPTH = 4
N_MODS = 3
MOD_ROWS = 8
NORM_SLAB = 16
NORM_UNROLL = 4
BF16_SUBLANES = 16
VMEM_LIMIT = 56 * 1024 * 1024

Dims = collections.namedtuple("Dims", "batch seq ctx d")


def _rows(dm):
    return dm.ctx + dm.seq


def _silu(v):
    return v * (1.0 / (1.0 + jnp.exp2(v * -math.log2(math.e))))


def _params(semantics):
    return pltpu.CompilerParams(dimension_semantics=semantics, vmem_limit_bytes=VMEM_LIMIT)


def _mod_kernel(c_ref, w_ref, b_ref, o_ref):
    s = _silu(c_ref[...]).astype(BF16)
    o_ref[...] = jnp.dot(s, w_ref[...].astype(BF16), preferred_element_type=F32) + b_ref[...]


def _modulation(cvec, w_mod, b_mod, tn=512):
    d, n = w_mod.shape
    return pl.pallas_call(
        _mod_kernel,
        grid=(n // tn,),
        in_specs=[pl.BlockSpec((MOD_ROWS, d), lambda j: (0, 0)),
                  pl.BlockSpec((d, tn), lambda j: (0, j)),
                  pl.BlockSpec((1, tn), lambda j: (0, j))],
        out_specs=pl.BlockSpec((MOD_ROWS, tn), lambda j: (0, j)),
        out_shape=jax.ShapeDtypeStruct((MOD_ROWS, n), F32),
        compiler_params=_params(("arbitrary",)),
        name="adaln_mod",
    )(cvec, w_mod, b_mod.reshape(1, n))


def _rope_layout():
    quarter = HEAD_DIM // 4
    idx = jnp.arange(HEAD_DIM).reshape(4, quarter)
    return idx[jnp.array([0, 2, 1, 3])].reshape(HEAD_DIM)


def _head_sums(sq, tn):
    r = lax.broadcasted_iota(jnp.int32, (tn, tn), 0) // HEAD_DIM
    c = lax.broadcasted_iota(jnp.int32, (tn, tn), 1) // HEAD_DIM
    return jnp.dot(sq.astype(BF16), (r == c).astype(BF16), preferred_element_type=F32)


def _in_proj_kernel(*refs, tm, tn, tiles_per_batch, ctx_len, segs, has_rope, has_gain, has_dt):
    it = iter(refs)
    x_ref, g_ref, shl_ref, scl_ref, shc_ref, scc_ref, w_ref = (next(it) for _ in range(7))
    cos_ref = sin_ref = wqk_ref = gain_ref = wdt_ref = wdtT_ref = dt_ref = dtT_ref = None
    if has_rope:
        cos_ref, sin_ref, wqk_ref = next(it), next(it), next(it)
    if has_gain:
        gain_ref = next(it)
    if has_dt:
        wdt_ref, wdtT_ref = next(it), next(it)
    o_ref = next(it)
    if has_dt:
        dt_ref, dtT_ref = next(it), next(it)
    h_ref, mul_ref, add_ref = next(it), next(it), next(it)

    i = pl.program_id(0)
    j = pl.program_id(1)

    @pl.when(j == 0)
    def _():
        d = g_ref.shape[1]
        g = g_ref[...]
        for k, (sc_ref, sh_ref) in enumerate(((scl_ref, shl_ref), (scc_ref, shc_ref))):
            mul_ref[k] = jnp.broadcast_to(g * (1.0 + sc_ref[0]), (NORM_SLAB, d))
            add_ref[k] = jnp.broadcast_to(sh_ref[0], (NORM_SLAB, d))

        def slab(r, carry):
            rows = pl.ds(pl.multiple_of(r * NORM_SLAB, NORM_SLAB), NORM_SLAB)
            x = x_ref[rows, :]
            y = x * lax.rsqrt(jnp.mean(x * x, axis=-1, keepdims=True) + EPS)
            k = ((i % tiles_per_batch) * tm + r * NORM_SLAB < ctx_len).astype(jnp.int32)
            h_ref[rows, :] = (y * mul_ref[k] + add_ref[k]).astype(BF16)
            return carry

        lax.fori_loop(0, tm // NORM_SLAB, slab, 0, unroll=NORM_UNROLL)
        h = h_ref[...]
        if has_dt:
            dt_ref[...] = jnp.dot(h, wdt_ref[...], preferred_element_type=F32)
            dtT_ref[...] = lax.dot_general(wdtT_ref[...], h, (((1,), (1,)), ((), ())),
                                           preferred_element_type=F32)

    def finish(gain_idx, rope, scale):
        w = wqk_ref[...] if rope else w_ref[...].astype(BF16)
        acc = jnp.dot(h_ref[...], w, preferred_element_type=F32)
        if gain_idx is None and not rope and scale is None:
            o_ref[...] = acc.astype(o_ref.dtype)
            return
        if gain_idx is not None:
            acc = acc * lax.rsqrt(_head_sums(acc * acc, tn) * (1.0 / HEAD_DIM) + EPS)
        for hh in range(tn // HEAD_DIM):
            cols = slice(hh * HEAD_DIM, (hh + 1) * HEAD_DIM)
            v = acc[:, cols]
            if gain_idx is not None:
                v = v * gain_ref[gain_idx:gain_idx + 1, :]
            if rope:
                v = v * cos_ref[...] + pltpu.roll(v, HEAD_DIM // 2, 1) * sin_ref[...]
            if scale is not None:
                v = v * scale
            o_ref[:, cols] = v.astype(o_ref.dtype)

    for (lo, hi, gain_idx, rope, scale) in segs:
        pl.when((j >= lo) & (j < hi))(functools.partial(finish, gain_idx, rope, scale))


def _in_proj(xc, norm_g, mods, w, dm, *, tm, tn, segs, rope=None, w_qk=None, gains=None, w_dt=None):
    m, d = xc.shape
    n = segs[-1][1] * tn
    tpb = _rows(dm) // tm
    mods3 = mods.reshape(MOD_ROWS, 1, N_MODS * d)
    has_rope, has_gain, has_dt = rope is not None, gains is not None, w_dt is not None
    qk_tiles = w_qk.shape[1] // tn if has_rope else 0

    def lat(col):
        return pl.BlockSpec((1, 1, d), lambda i, j: (i // tpb, 0, col))

    def ctx(col):
        return pl.BlockSpec((1, 1, d), lambda i, j: (dm.batch, 0, col))

    args = [xc, norm_g.reshape(1, d), mods3, mods3, mods3, mods3, w]
    in_specs = [pl.BlockSpec((tm, d), lambda i, j: (i, 0)),
                pl.BlockSpec((1, d), lambda i, j: (0, 0)),
                lat(0), lat(1), ctx(0), ctx(1),
                pl.BlockSpec((d, tn), lambda i, j: (0, jnp.maximum(j, qk_tiles)))]
    if has_rope:
        args += list(rope) + [w_qk]
        in_specs += [pl.BlockSpec((tm, HEAD_DIM), lambda i, j: (i % tpb, 0))] * 2
        in_specs.append(pl.BlockSpec((d, tn), lambda i, j: (0, jnp.minimum(j, qk_tiles - 1))))
    if has_gain:
        args.append(gains)
        in_specs.append(pl.BlockSpec(gains.shape, lambda i, j: (0, 0)))
    out_shape = [jax.ShapeDtypeStruct((m, n), BF16)]
    out_specs = [pl.BlockSpec((tm, tn), lambda i, j: (i, j))]
    if has_dt:
        n_dt = w_dt.shape[1]
        args += [w_dt, w_dt.T]
        in_specs += [pl.BlockSpec((d, n_dt), lambda i, j: (0, 0)),
                     pl.BlockSpec((n_dt, d), lambda i, j: (0, 0))]
        out_shape += [jax.ShapeDtypeStruct((m, n_dt), F32), jax.ShapeDtypeStruct((n_dt, m), F32)]
        out_specs += [pl.BlockSpec((tm, n_dt), lambda i, j: (i, 0)),
                      pl.BlockSpec((n_dt, tm), lambda i, j: (0, i))]
    kern = functools.partial(_in_proj_kernel, tm=tm, tn=tn, tiles_per_batch=tpb, ctx_len=dm.ctx,
                             segs=segs, has_rope=has_rope, has_gain=has_gain, has_dt=has_dt)
    out = pl.pallas_call(
        kern,
        grid=(m // tm, n // tn),
        in_specs=in_specs,
        out_specs=out_specs,
        out_shape=out_shape,
        scratch_shapes=[pltpu.VMEM((tm, d), BF16),
                        pltpu.VMEM((2, NORM_SLAB, d), F32),
                        pltpu.VMEM((2, NORM_SLAB, d), F32)],
        compiler_params=_params(("parallel", "arbitrary")),
        name="in_proj",
    )(*args)
    return out if has_dt else out[0]


def _out_proj_kernel(g_ref, w_ref, x_ref, gl_ref, gc_ref, o_ref, *, tm, tiles_per_batch, ctx_len):
    i = pl.program_id(0)
    acc = jnp.dot(g_ref[...], w_ref[...].astype(BF16), preferred_element_type=F32)
    row = (i % tiles_per_batch) * tm + lax.broadcasted_iota(jnp.int32, (tm, 1), 0)
    gate = jnp.where(row < ctx_len, gc_ref[0], gl_ref[0])
    o_ref[...] = x_ref[...] + gate * acc


def _out_proj(g, w, xc, mods, dm, *, tm, tn):
    m, k = g.shape
    d = w.shape[1]
    tpb = _rows(dm) // tm
    tiles_n = d // tn
    mods3 = mods.reshape(MOD_ROWS, 1, N_MODS * d)
    gate_col = 2 * tiles_n
    kern = functools.partial(_out_proj_kernel, tm=tm, tiles_per_batch=tpb, ctx_len=dm.ctx)
    return pl.pallas_call(
        kern,
        grid=(m // tm, tiles_n),
        in_specs=[pl.BlockSpec((tm, k), lambda i, j: (i, 0)),
                  pl.BlockSpec((k, tn), lambda i, j: (0, j)),
                  pl.BlockSpec((tm, tn), lambda i, j: (i, j)),
                  pl.BlockSpec((1, 1, tn), lambda i, j: (i // tpb, 0, gate_col + j)),
                  pl.BlockSpec((1, 1, tn), lambda i, j: (dm.batch, 0, gate_col + j))],
        out_specs=pl.BlockSpec((tm, tn), lambda i, j: (i, j)),
        out_shape=jax.ShapeDtypeStruct((m, d), F32),
        compiler_params=_params(("parallel", "arbitrary")),
        name="out_proj",
    )(g, w, xc, mods3, mods3)


def _lane_tile_reduce(v, op):
    part = v[:, :HEAD_DIM]
    for t in range(1, v.shape[1] // HEAD_DIM):
        part = op(part, v[:, t * HEAD_DIM:(t + 1) * HEAD_DIM])
    return part


def _scores(pairs, k_ref, krows):
    blocks = []
    for q, kcols in pairs:
        half = q.shape[0] // 2
        for qq in (q[:half], q[half:]):
            blocks.append(lax.dot_general(qq, k_ref[krows, kcols], (((1,), (1,)), ((), ())),
                                          preferred_element_type=F32))
    return jnp.concatenate(blocks, axis=0)


def _weighted(p, v_ref, krows, with_sums=False):
    half = p.shape[0] // 2
    pb = p.astype(BF16)
    v = v_ref[krows, :]
    if with_sums:
        v = jnp.concatenate([v, jnp.ones_like(v)], axis=1)
    return jnp.concatenate([jnp.dot(pp, v, preferred_element_type=F32)
                            for pp in (pb[:half], pb[half:])], axis=0)


def _attend_in_place(pairs, k_ref, v_ref, key_rows, tk):
    chunks = [slice(c * tk, (c + 1) * tk) for c in range(key_rows // tk)]
    s = [_scores(pairs, k_ref, krows) for krows in chunks]
    m_part = functools.reduce(jnp.maximum, [_lane_tile_reduce(sc, jnp.maximum) for sc in s])
    m = jnp.max(m_part, axis=1, keepdims=True)
    p = [jnp.exp2(sc - m) for sc in s]
    l_part = functools.reduce(jnp.add, [_lane_tile_reduce(pc, jnp.add) for pc in p])
    acc = functools.reduce(jnp.add, [_weighted(pc, v_ref, krows) for pc, krows in zip(p, chunks)])
    return acc * (1.0 / jnp.sum(l_part, axis=1, keepdims=True))


def _attend_pipelined(pairs_next, s_next_ref, s_cur_ref, k_ref, v_ref, m_ref, key_rows, tk):
    n_chunks = key_rows // tk
    have_cur = s_cur_ref is not None
    dv = v_ref.shape[1]
    mxu_sums = dv == HEAD_DIM
    m_part = acc = None
    for c in range(n_chunks):
        krows = slice(c * tk, (c + 1) * tk)
        if pairs_next is not None:
            s = _scores(pairs_next, k_ref, krows)
            s_next_ref[c] = s
            part = _lane_tile_reduce(s, jnp.maximum)
            m_part = part if m_part is None else jnp.maximum(m_part, part)
        if have_cur:
            p = jnp.concatenate(
                [jnp.exp2(s_cur_ref[c, :, t * HEAD_DIM:(t + 1) * HEAD_DIM] - m_ref[0])
                 for t in range(tk // HEAD_DIM)], axis=1)
            if not mxu_sums:
                part = _lane_tile_reduce(p, jnp.add)
                m_ref[1] = part if c == 0 else m_ref[1] + part
            pv = _weighted(p, v_ref, krows, with_sums=mxu_sums)
            acc = pv if acc is None else acc + pv
    if pairs_next is not None:
        m_ref[0] = jnp.broadcast_to(jnp.max(m_part, axis=1, keepdims=True), m_ref.shape[1:])
    if not have_cur:
        return None
    if mxu_sums:
        return acc[:, :dv] * (1.0 / acc[:, dv:])
    return acc * (1.0 / jnp.sum(m_ref[1], axis=1, keepdims=True))


def _attention_steps(step, queries, finish, k_ref, v_ref, s_refs, m_ref, *, tq, tk, ctx_len, rows):
    n_lat = (rows - ctx_len) // tq

    def lat_row0(t):
        return pl.multiple_of(ctx_len + (t - 1) * tq, math.gcd(ctx_len, tq))

    def pipe(t_next, t_cur, cur_parity):
        pairs = None if t_next is None else queries(lat_row0(t_next), tq)
        o = _attend_pipelined(pairs, None if t_next is None else s_refs[1 - cur_parity],
                              None if t_cur is None else s_refs[cur_parity],
                              k_ref, v_ref, m_ref, rows, tk)
        if t_cur is not None:
            finish(lat_row0(t_cur), tq, o)

    @pl.when(step == 0)
    def _():
        finish(0, ctx_len, _attend_in_place(queries(0, ctx_len), k_ref, v_ref, ctx_len, tk))
        pipe(1, None, 0)

    if n_lat > 1:
        for parity in range(2):
            pl.when((step >= 1) & (step < n_lat) & (step % 2 == parity))(
                functools.partial(pipe, step + 1, step, parity))
    pl.when(step == n_lat)(functools.partial(pipe, None, step, n_lat % 2))


def _gqa_kernel(q_ref, k_ref, v_ref, z_ref, o_ref, s0_ref, s1_ref, m_ref, *, tq, tk, group, ctx_len, rows):
    def queries(row0, nq):
        q = jnp.concatenate([q_ref[pl.ds(row0, nq), g * HEAD_DIM:(g + 1) * HEAD_DIM]
                             for g in range(group)], axis=0)
        return [(q, slice(0, HEAD_DIM))]

    def finish(row0, nq, o):
        qrows = pl.ds(row0, nq)
        for g in range(group):
            cols = slice(g * HEAD_DIM, (g + 1) * HEAD_DIM)
            z = z_ref[qrows, cols].astype(F32)
            o_ref[qrows, cols] = (o[g * nq:(g + 1) * nq] * _silu(z)).astype(o_ref.dtype)

    _attention_steps(pl.program_id(2), queries, finish, k_ref, v_ref, (s0_ref, s1_ref), m_ref,
                     tq=tq, tk=tk, ctx_len=ctx_len, rows=rows)


def _gqa_attention(proj, dm, *, tq, tk):
    rows = _rows(dm)
    m = proj.shape[0]
    width = dm.d
    heads = width // HEAD_DIM
    group = heads // GQA_KV_HEADS
    gw = group * HEAD_DIM
    k_col0 = width // HEAD_DIM
    v_col0 = k_col0 + GQA_KV_HEADS
    z_col0 = (width + 2 * GQA_KV_HEADS * HEAD_DIM) // gw
    kern = functools.partial(_gqa_kernel, tq=tq, tk=tk, group=group, ctx_len=dm.ctx, rows=rows)
    return pl.pallas_call(
        kern,
        grid=(dm.batch, GQA_KV_HEADS, 1 + dm.seq // tq),
        in_specs=[pl.BlockSpec((rows, gw), lambda b, h, i: (b, h)),
                  pl.BlockSpec((rows, HEAD_DIM), lambda b, h, i: (b, k_col0 + h)),
                  pl.BlockSpec((rows, HEAD_DIM), lambda b, h, i: (b, v_col0 + h)),
                  pl.BlockSpec((rows, gw), lambda b, h, i: (b, z_col0 + h))],
        out_specs=pl.BlockSpec((rows, gw), lambda b, h, i: (b, h)),
        out_shape=jax.ShapeDtypeStruct((m, width), BF16),
        scratch_shapes=[pltpu.VMEM((rows // tk, group * tq, tk), F32),
                        pltpu.VMEM((rows // tk, group * tq, tk), F32),
                        pltpu.VMEM((2, group * tq, HEAD_DIM), F32)],
        compiler_params=_params(("parallel", "parallel", "arbitrary")),
        name="gqa_attention",
    )(proj, proj, proj, proj)


def _diff_kernel(lam_ref, q_ref, k_ref, v_ref, z_ref, subln_ref, o_ref, s0_ref, s1_ref, m_ref,
                 *, tq, tk, ctx_len, rows, lambda_init):
    lv = lam_ref[...]
    lam = (jnp.exp(jnp.sum(lv[0:1] * lv[1:2], axis=1, keepdims=True))
           - jnp.exp(jnp.sum(lv[2:3] * lv[3:4], axis=1, keepdims=True)) + lambda_init)

    def queries(row0, nq):
        halves = [slice(t * HEAD_DIM, (t + 1) * HEAD_DIM) for t in range(2)]
        return [(q_ref[pl.ds(row0, nq), cols], cols) for cols in halves]

    def finish(row0, nq, o):
        qrows = pl.ds(row0, nq)
        o = o[:nq] - lam * o[nq:]
        o = o * lax.rsqrt(jnp.mean(o * o, axis=-1, keepdims=True) + EPS) * subln_ref[...]
        o = o * (1.0 - lambda_init)
        o_ref[qrows, :] = (o * _silu(z_ref[qrows, :].astype(F32))).astype(o_ref.dtype)

    _attention_steps(pl.program_id(2), queries, finish, k_ref, v_ref, (s0_ref, s1_ref), m_ref,
                     tq=tq, tk=tk, ctx_len=ctx_len, rows=rows)


def _diff_attention(proj, lam_vecs, subln, dm, *, tq, tk, lambda_init):
    rows = _rows(dm)
    m = proj.shape[0]
    width = dm.d
    hw = 2 * HEAD_DIM
    heads = width // hw
    kern = functools.partial(_diff_kernel, tq=tq, tk=tk, ctx_len=dm.ctx, rows=rows, lambda_init=lambda_init)

    def head_block(first):
        return pl.BlockSpec((rows, hw), lambda b, h, i: (b, first + h))

    return pl.pallas_call(
        kern,
        grid=(dm.batch, heads, 1 + dm.seq // tq),
        in_specs=[pl.BlockSpec(lam_vecs.shape, lambda b, h, i: (0, 0)),
                  head_block(0), head_block(heads), head_block(2 * heads), head_block(3 * heads),
                  pl.BlockSpec((1, hw), lambda b, h, i: (0, 0))],
        out_specs=head_block(0),
        out_shape=jax.ShapeDtypeStruct((m, width), BF16),
        scratch_shapes=[pltpu.VMEM((rows // tk, 2 * tq, tk), F32),
                        pltpu.VMEM((rows // tk, 2 * tq, tk), F32),
                        pltpu.VMEM((2, 2 * tq, HEAD_DIM), F32)],
        compiler_params=_params(("parallel", "parallel", "arbitrary")),
        name="diff_attention",
    )(lam_vecs, proj, proj, proj, proj, subln.reshape(1, hw))


CONV_TAPS = (-1, 0, 1, 2)
CONV_STRIP = 256


def _conv_kernel(cur_ref, prev_ref, next_ref, shift_ref, w_ref, b_ref, o_ref, *, tr, tiles_per_batch,
                 ctx_tiles):
    r = pl.program_id(1)
    edge = BF16_SUBLANES
    centre = CONV_TAPS.index(0)
    has_prev = (r != 0) & (r != ctx_tiles)
    has_next = (r != ctx_tiles - 1) & (r != tiles_per_batch - 1)
    row = lax.broadcasted_iota(jnp.int32, (edge, 1), 0)
    for c0 in range(0, cur_ref.shape[1], CONV_STRIP):
        cols = slice(c0, c0 + CONV_STRIP)
        xb = cur_ref[:, cols]
        w = [w_ref[t:t + 1, cols] for t in range(SSM_CONV)]
        y = b_ref[:, cols] + w[centre] * xb.astype(F32)
        for k, tap in enumerate(t for t in range(SSM_CONV) if t != centre):
            y = y + w[tap] * jnp.dot(shift_ref[k], xb, preferred_element_type=F32)
        o_ref[edge:tr - edge, cols] = _silu(y[edge:tr - edge]).astype(o_ref.dtype)
        prev = jnp.where(has_prev, prev_ref[edge - 1:edge, cols].astype(F32), 0.0)
        nxt0 = jnp.where(has_next, next_ref[0:1, cols].astype(F32), 0.0)
        nxt1 = jnp.where(has_next, next_ref[1:2, cols].astype(F32), 0.0)
        top = y[:edge] + jnp.where(row == 0, w[0] * prev, 0.0)
        o_ref[:edge, cols] = _silu(top).astype(o_ref.dtype)
        bottom = (y[tr - edge:] + jnp.where(row == edge - 2, w[3] * nxt0, 0.0)
                  + jnp.where(row == edge - 1, w[2] * nxt0 + w[3] * nxt1, 0.0))
        o_ref[tr - edge:, cols] = _silu(bottom).astype(o_ref.dtype)


def _ssd_conv(proj, conv_w, conv_b, dm, *, col0, tr, tn):
    rows = _rows(dm)
    m = proj.shape[0]
    n = conv_w.shape[1]
    tpb = rows // tr
    sub = tr // BF16_SUBLANES
    last = m // BF16_SUBLANES - 1
    c0 = col0 // tn
    shifts = jnp.stack([jnp.eye(tr, k=off, dtype=BF16) for off in CONV_TAPS if off != 0])
    kern = functools.partial(_conv_kernel, tr=tr, tiles_per_batch=tpb, ctx_tiles=dm.ctx // tr)
    return pl.pallas_call(
        kern,
        grid=(dm.batch, tpb, n // tn),
        in_specs=[pl.BlockSpec((tr, tn), lambda b, r, j: (b * tpb + r, c0 + j)),
                  pl.BlockSpec((BF16_SUBLANES, tn),
                               lambda b, r, j: (jnp.maximum((b * tpb + r) * sub - 1, 0), c0 + j)),
                  pl.BlockSpec((BF16_SUBLANES, tn),
                               lambda b, r, j: (jnp.minimum((b * tpb + r + 1) * sub, last), c0 + j)),
                  pl.BlockSpec(shifts.shape, lambda b, r, j: (0, 0, 0)),
                  pl.BlockSpec((SSM_CONV, tn), lambda b, r, j: (0, j)),
                  pl.BlockSpec((1, tn), lambda b, r, j: (0, j))],
        out_specs=pl.BlockSpec((tr, tn), lambda b, r, j: (b * tpb + r, j)),
        out_shape=jax.ShapeDtypeStruct((m, n), BF16),
        compiler_params=_params(("parallel", "parallel", "parallel")),
        name="ssd_conv",
    )(proj, proj, proj, shifts, conv_w, conv_b.reshape(1, n))


def _split_dot(a, b_hi_exact, dims):
    hi = a.astype(BF16)
    lo = (a - hi.astype(F32)).astype(BF16)
    if dims == "ab":
        return (jnp.dot(hi, b_hi_exact, preferred_element_type=F32)
                + jnp.dot(lo, b_hi_exact, preferred_element_type=F32))
    return (jnp.dot(b_hi_exact, hi, preferred_element_type=F32)
            + jnp.dot(b_hi_exact, lo, preferred_element_type=F32))


def _ssd_kernel(*refs, heads, reverse, finish):
    it = iter(refs)
    (x_ref, b_ref, c_ref, dt_ref, dtT_ref, bias_ref, biasT_ref, alog_ref, alogT_ref,
     expand_ref) = (next(it) for _ in range(10))
    other_ref = z_ref = dskip_ref = gain_ref = None
    if finish:
        other_ref, z_ref, dskip_ref, gain_ref = (next(it) for _ in range(4))
    y_ref, state_ref = next(it), next(it)
    q = SSM_CHUNK
    hpg = heads // SSM_GROUPS
    gw = hpg * SSM_HEAD_DIM
    pw = 2 * SSM_HEAD_DIM
    step = pl.program_id(1)

    @pl.when(step == 0)
    def _():
        state_ref[...] = jnp.zeros(state_ref.shape, F32)

    def softplus(v):
        return jnp.maximum(v, 0.0) + jnp.log(1.0 + jnp.exp(-jnp.abs(v)))

    dcol = heads if reverse else 0
    dt = softplus(dt_ref[:, dcol:dcol + heads] + bias_ref[...])
    dtT = softplus(dtT_ref[...] + biasT_ref[...])
    a = dt * -jnp.exp(alog_ref[...])
    aT = dtT * -jnp.exp(alogT_ref[...])
    ri = lax.broadcasted_iota(jnp.int32, (q, q), 0)
    ci = lax.broadcasted_iota(jnp.int32, (q, q), 1)
    before = (ci >= ri) if reverse else (ci <= ri)
    tri = before.astype(BF16)
    triT = ((ri >= ci) if reverse else (ri <= ci)).astype(BF16)
    cum = _split_dot(a, tri, "ba")
    cumT = _split_dot(aT, triT, "ab")
    end = 0 if reverse else q - 1
    total = cum[end:end + 1, :]
    expand = expand_ref[...]
    w_full = jnp.dot((jnp.exp(total - cum) * dt).astype(BF16), expand,
                     preferred_element_type=F32)
    e_full = jnp.dot(jnp.exp(cum).astype(BF16), expand, preferred_element_type=F32)
    cd_full = _split_dot(jnp.broadcast_to(jnp.exp(total), (8, heads)), expand, "ab")[0:1]
    lcumT = cumT - jnp.log(dtT)
    low = lax.broadcasted_iota(jnp.int32, (q, pw), 1) < SSM_HEAD_DIM

    for g in range(SSM_GROUPS):
        gcols = slice(g * gw, (g + 1) * gw)
        ncols = slice(g * SSM_STATE, (g + 1) * SSM_STATE)
        bg = b_ref[:, ncols]
        cg = c_ref[:, ncols]
        cb = lax.dot_general(cg, bg, (((1,), (1,)), ((), ())), preferred_element_type=F32)
        st = state_ref[:, gcols]
        y_off = jnp.dot(cg, st.astype(BF16), preferred_element_type=F32) * e_full[:, gcols]
        ys = []
        for j in range(hpg // 2):
            xp = x_ref[:, g * gw + j * pw:g * gw + (j + 1) * pw].astype(F32)
            xbd = jnp.concatenate([jnp.where(low, xp, 0.0), jnp.where(low, 0.0, xp)], axis=0).astype(BF16)
            mats = []
            for hh in range(2):
                h = g * hpg + 2 * j + hh
                diff = cum[:, h:h + 1] - lcumT[h:h + 1, :]
                mats.append((cb * jnp.exp(jnp.where(before, diff, -jnp.inf))).astype(BF16))
            ys.append(jnp.dot(jnp.concatenate(mats, axis=1), xbd, preferred_element_type=F32)
                      + y_off[:, j * pw:(j + 1) * pw])
        xg = x_ref[:, gcols].astype(F32)
        if finish:
            y = jnp.concatenate(ys, axis=1) + other_ref[:, gcols].astype(F32) + dskip_ref[:, gcols] * xg
            v = y * _silu(z_ref[:, gcols].astype(F32))
            v = v * lax.rsqrt(jnp.mean(v * v, axis=-1, keepdims=True) + EPS) * gain_ref[:, gcols]
            y_ref[:, gcols] = v.astype(y_ref.dtype)
        else:
            for j, y in enumerate(ys):
                y_ref[:, g * gw + j * pw:g * gw + (j + 1) * pw] = y.astype(y_ref.dtype)
        xw = (xg * w_full[:, gcols]).astype(BF16)
        bgT = bg.astype(F32).T.astype(BF16)
        state_ref[:, gcols] = st * cd_full[:, gcols] + jnp.dot(bgT, xw, preferred_element_type=F32)


def _ssd_scan(xbc, dt, dtT, dt_bias, a_log, dm, *, inner, reverse, tail=None):
    rows = _rows(dm)
    m = xbc.shape[0]
    q = SSM_CHUNK
    heads = inner // SSM_HEAD_DIM
    nc = rows // q
    ctx_chunks = dm.ctx // q
    sw = SSM_GROUPS * SSM_STATE
    d = 1 if reverse else 0

    def chunk(b, s):
        if reverse:
            s = jnp.where(s < ctx_chunks, ctx_chunks - 1 - s, nc - 1 - (s - ctx_chunks))
        return b * nc + s

    expand = jnp.repeat(jnp.eye(heads, dtype=BF16), SSM_HEAD_DIM, axis=1)
    row_spec = pl.BlockSpec((q, inner), lambda b, s: (chunk(b, s), 0))
    vec_spec = pl.BlockSpec((1, inner), lambda b, s: (0, 0))
    args = [xbc, xbc, xbc, dt, dtT, dt_bias[d][None, :], dt_bias[d][:, None], a_log[d][None, :],
            a_log[d][:, None], expand]
    in_specs = [row_spec,
                pl.BlockSpec((q, sw), lambda b, s: (chunk(b, s), inner // sw)),
                pl.BlockSpec((q, sw), lambda b, s: (chunk(b, s), inner // sw + 1)),
                pl.BlockSpec((q, 2 * heads), lambda b, s: (chunk(b, s), 0)),
                pl.BlockSpec((heads, q), lambda b, s: (d, chunk(b, s))),
                pl.BlockSpec((1, heads), lambda b, s: (0, 0)),
                pl.BlockSpec((heads, 1), lambda b, s: (0, 0)),
                pl.BlockSpec((1, heads), lambda b, s: (0, 0)),
                pl.BlockSpec((heads, 1), lambda b, s: (0, 0)),
                pl.BlockSpec((heads, inner), lambda b, s: (0, 0))]
    if tail is not None:
        y_other, proj, d_skip, gain = tail
        args += [y_other, proj, jnp.repeat(d_skip.astype(F32), SSM_HEAD_DIM).reshape(1, inner),
                 gain.reshape(1, inner)]
        in_specs += [row_spec, row_spec, vec_spec, vec_spec]
    kern = functools.partial(_ssd_kernel, heads=heads, reverse=reverse, finish=tail is not None)
    return pl.pallas_call(
        kern,
        grid=(dm.batch, nc),
        in_specs=in_specs,
        out_specs=row_spec,
        out_shape=jax.ShapeDtypeStruct((m, inner), BF16),
        scratch_shapes=[pltpu.VMEM((SSM_STATE, inner), F32)],
        compiler_params=_params(("parallel", "arbitrary")),
        name="ssd_scan_bwd" if reverse else "ssd_scan_fwd",
    )(*args)


def _final_norm_kernel(x_ref, g_ref, o_ref):
    x = x_ref[...]
    o_ref[...] = x * lax.rsqrt(jnp.mean(x * x, axis=-1, keepdims=True) + EPS) * g_ref[...]


def _final_norm(xc, gain, dm, *, tr):
    rows = _rows(dm)
    tpb = rows // tr
    lat_tiles = dm.seq // tr
    ctx_tiles = dm.ctx // tr
    return pl.pallas_call(
        _final_norm_kernel,
        grid=(dm.batch, lat_tiles),
        in_specs=[pl.BlockSpec((tr, dm.d), lambda b, i: (b * tpb + ctx_tiles + i, 0)),
                  pl.BlockSpec((1, dm.d), lambda b, i: (0, 0))],
        out_specs=pl.BlockSpec((tr, dm.d), lambda b, i: (b * lat_tiles + i, 0)),
        out_shape=jax.ShapeDtypeStruct((dm.batch * dm.seq, dm.d), F32),
        compiler_params=_params(("parallel", "parallel")),
        name="final_norm",
    )(xc, gain.reshape(1, dm.d))


def _rope_tables(dm):
    d_axis = HEAD_DIM // 2
    t = jnp.arange(dm.seq, dtype=jnp.int32)
    inv_freq = jnp.power(ROPE_THETA, -jnp.arange(0, d_axis, 2, dtype=F32) / d_axis)
    ang_r = (t // GRID_W).astype(F32)[:, None] * inv_freq[None, :]
    ang_c = (t % GRID_W).astype(F32)[:, None] * inv_freq[None, :]
    cos = jnp.concatenate([jnp.cos(ang_r), jnp.cos(ang_c), jnp.cos(ang_r), jnp.cos(ang_c)], axis=1)
    sin = jnp.concatenate([-jnp.sin(ang_r), -jnp.sin(ang_c), jnp.sin(ang_r), jnp.sin(ang_c)], axis=1)
    cos = jnp.concatenate([jnp.ones((dm.ctx, HEAD_DIM), F32), cos], axis=0)
    sin = jnp.concatenate([jnp.zeros((dm.ctx, HEAD_DIM), F32), sin], axis=0)
    return cos, sin


def _qk_weights(w_in, qk_cols):
    d = w_in.shape[0]
    quarter = HEAD_DIM // 4
    qk = w_in[:, :qk_cols].reshape(d, qk_cols // HEAD_DIM, 2, 2, quarter)
    return jnp.swapaxes(qk, 2, 3).reshape(d, qk_cols).astype(BF16)


def _lambda_init(layer_idx):
    return 0.8 - 0.6 * math.exp(-0.3 * layer_idx)


def _tiles(dm):
    rows = _rows(dm)
    tm = rows // 2 if (rows // 2) % 128 == 0 else rows
    return dict(tm=tm, tn=min(512, dm.d), tq=dm.ctx, tk=dm.ctx)


def _forward(dm, x, c, ctx, c_ctx, layers, final_norm):
    d = dm.d
    til = _tiles(dm)
    tm, tn, tq, tk = til["tm"], til["tn"], til["tq"], til["tk"]
    scale = HEAD_DIM ** -0.5 * math.log2(math.e)
    xc =jnp.concatenate([ctx, x], axis=1).reshape(dm.batch * _rows(dm), d)
    cvec = jnp.concatenate([c, c_ctx[None, :], jnp.zeros((MOD_ROWS - dm.batch - 1, d), F32)], axis=0)
    rope = _rope_tables(dm)
    q_tiles = d // tn

    for i, (kind, (w_mod, b_mod, norm), p) in enumerate(layers):
        mods = _modulation(cvec, w_mod, b_mod)
        if kind == "gqa":
            w_in, q_gain, k_gain, w_out = p
            kv_tiles = GQA_KV_HEADS * HEAD_DIM // tn
            n_tiles = w_in.shape[1] // tn
            segs = ((0, q_tiles, 0, True, scale),
                    (q_tiles, q_tiles + kv_tiles, 1, True, None),
                    (q_tiles + kv_tiles, n_tiles, None, False, None))
            w_qk = _qk_weights(w_in, (q_tiles + kv_tiles) * tn)
            gains = jnp.stack([q_gain, k_gain]).astype(F32)[:, _rope_layout()]
            proj = _in_proj(xc, norm, mods, w_in, dm, tm=tm, tn=tn, segs=segs, rope=rope, w_qk=w_qk,
                            gains=gains)
            g = _gqa_attention(proj, dm, tq=tq, tk=tk)
        elif kind == "diff":
            w_in, lq1, lk1, lq2, lk2, subln, w_out = p
            n_tiles = w_in.shape[1] // tn
            segs = ((0, q_tiles, None, True, scale),
                    (q_tiles, 2 * q_tiles, None, True, None),
                    (2 * q_tiles, n_tiles, None, False, None))
            proj = _in_proj(xc, norm, mods, w_in, dm, tm=tm, tn=tn, segs=segs, rope=rope,
                            w_qk=_qk_weights(w_in, 2 * q_tiles * tn))
            lam_vecs = jnp.stack([lq1, lk1, lq2, lk2]).astype(F32)
            g = _diff_attention(proj, lam_vecs, subln, dm, tq=2 * tq, tk=tk, lambda_init=_lambda_init(i))
        else:
            w_in, conv_w, conv_b, dt_bias, a_log, d_skip, ssm_norm, w_out = p
            inner = w_out.shape[0]
            n_main = 2 * inner + 2 * SSM_GROUPS * SSM_STATE
            segs = ((0, n_main // tn, None, False, None),)
            proj, dt, dtT = _in_proj(xc, norm, mods, w_in, dm, tm=tm, tn=tn,
                                     segs=segs, w_dt=w_in[:, n_main:].astype(BF16))
            xbc = _ssd_conv(proj, conv_w, conv_b, dm, col0=inner, tr=tq,
                            tn=math.gcd(inner, conv_w.shape[1]))
            y_f = _ssd_scan(xbc, dt, dtT, dt_bias, a_log, dm, inner=inner, reverse=False)
            g = _ssd_scan(xbc, dt, dtT, dt_bias, a_log, dm, inner=inner, reverse=True,
                          tail=(y_f, proj, d_skip, ssm_norm))
        wide = w_out.shape[0] > d
        xc = _out_proj(g, w_out, xc, mods, dm, tm=tm if wide else _rows(dm), tn=tn if wide else tn // 2)

    out = _final_norm(xc, final_norm, dm, tr=tq)
    return out.reshape(dm.batch, dm.seq, d)


def kernel(x, c, ctx, c_ctx, l0_w_mod, l0_b_mod, l0_norm, l0_w_in, l0_q_gain, l0_k_gain, l0_w_out, l1_w_mod, l1_b_mod, l1_norm, l1_w_in, l1_lambda_q1, l1_lambda_k1, l1_lambda_q2, l1_lambda_k2, l1_subln, l1_w_out, l2_w_mod, l2_b_mod, l2_norm, l2_w_in, l2_conv_w, l2_conv_b, l2_dt_bias, l2_A_log, l2_D, l2_ssm_norm, l2_w_out, l3_w_mod, l3_b_mod, l3_norm, l3_w_in, l3_q_gain, l3_k_gain, l3_w_out, final_norm):
    dm = Dims(batch=x.shape[0], seq=x.shape[1], ctx=ctx.shape[1], d=x.shape[2])
    layers = (
        ("gqa", (l0_w_mod, l0_b_mod, l0_norm), (l0_w_in, l0_q_gain, l0_k_gain, l0_w_out)),
        ("diff", (l1_w_mod, l1_b_mod, l1_norm), (l1_w_in, l1_lambda_q1, l1_lambda_k1, l1_lambda_q2,
                                                 l1_lambda_k2, l1_subln, l1_w_out)),
        ("ssd", (l2_w_mod, l2_b_mod, l2_norm), (l2_w_in, l2_conv_w, l2_conv_b, l2_dt_bias, l2_A_log,
                                                l2_D, l2_ssm_norm, l2_w_out)),
        ("gqa", (l3_w_mod, l3_b_mod, l3_norm), (l3_w_in, l3_q_gain, l3_k_gain, l3_w_out)),
    )
    return _forward(dm, x, c, ctx, c_ctx, layers, final_norm)
```

```python
import collections
import functools
import math

import jax
import jax.numpy as jnp
from jax import lax
from jax.experimental import pallas as pl
from jax.experimental.pallas import tpu as pltpu

F32 = jnp.float32
BF16 = jnp.bfloat16

EPS = 1e-6
ROPE_THETA = 10000.0
HEAD_DIM = 128
GQA_KV_HEADS = 4
SSM_HEAD_DIM = 64
SSM_GROUPS = 8
SSM_STATE = 128
SSM_CONV = 4
SSM_CHUNK = 128
GRID_W = 64
DEPTH = 4
N_MODS = 3
MOD_ROWS = 8
MOD_TILE = 1024
NORM_SLAB = 16
NORM_UNROLL = 8
BF16_SUBLANES = 16
VMEM_LIMIT = 56 * 1024 * 1024

Dims = collections.namedtuple("Dims", "batch seq ctx d")


def _rows(dm):
    return dm.ctx + dm.seq


def _silu(v):
    return v * (1.0 / (1.0 + jnp.exp2(v * -math.log2(math.e))))


def _params(semantics):
    return pltpu.CompilerParams(dimension_semantics=semantics, vmem_limit_bytes=VMEM_LIMIT)


def _mod_kernel(c_ref, w_ref, b_ref, o_ref):
    s = _silu(c_ref[...]).astype(BF16)
    o_ref[...] = jnp.dot(s, w_ref[...].astype(BF16), preferred_element_type=F32) + b_ref[...]


def _modulation(cvec, w_mod, b_mod):
    d, n = w_mod.shape
    tn = math.gcd(n, MOD_TILE)
    return pl.pallas_call(
        _mod_kernel,
        grid=(n // tn,),
        in_specs=[pl.BlockSpec((MOD_ROWS, d), lambda j: (0, 0)),
                  pl.BlockSpec((d, tn), lambda j: (0, j)),
                  pl.BlockSpec((1, tn), lambda j: (0, j))],
        out_specs=pl.BlockSpec((MOD_ROWS, tn), lambda j: (0, j)),
        out_shape=jax.ShapeDtypeStruct((MOD_ROWS, n), F32),
        compiler_params=_params(("arbitrary",)),
        name="adaln_mod",
    )(cvec, w_mod, b_mod.reshape(1, n))


def _rope_layout():
    quarter = HEAD_DIM // 4
    idx = jnp.arange(HEAD_DIM).reshape(4, quarter)
    return idx[jnp.array([0, 2, 1, 3])].reshape(HEAD_DIM)


def _head_sums(sq, tn):
    r = lax.broadcasted_iota(jnp.int32, (tn, tn), 0) // HEAD_DIM
    c = lax.broadcasted_iota(jnp.int32, (tn, tn), 1) // HEAD_DIM
    return jnp.dot(sq.astype(BF16), (r == c).astype(BF16), preferred_element_type=F32)


def _in_proj_kernel(*refs, tm, tn, tiles_per_batch, ctx_len, segs, has_rope, has_gain, has_dt):
    it = iter(refs)
    x_ref, g_ref, shl_ref, scl_ref, shc_ref, scc_ref, w_ref = (next(it) for _ in range(7))
    cos_ref = sin_ref = wqk_ref = gain_ref = wdt_ref = wdtT_ref = dt_ref = dtT_ref = None
    if has_rope:
        cos_ref, sin_ref, wqk_ref = next(it), next(it), next(it)
    if has_gain:
        gain_ref = next(it)
    if has_dt:
        wdt_ref, wdtT_ref = next(it), next(it)
    o_ref = next(it)
    if has_dt:
        dt_ref, dtT_ref = next(it), next(it)
    h_ref, mul_ref, add_ref = next(it), next(it), next(it)

    i = pl.program_id(0)
    j = pl.program_id(1)

    @pl.when(j == 0)
    def _():
        d = g_ref.shape[1]
        g = g_ref[...]
        for k, (sc_ref, sh_ref) in enumerate(((scl_ref, shl_ref), (scc_ref, shc_ref))):
            mul_ref[k] = jnp.broadcast_to(g * (1.0 + sc_ref[0]), (NORM_SLAB, d))
            add_ref[k] = jnp.broadcast_to(sh_ref[0], (NORM_SLAB, d))

        def slab(r, carry):
            rows = pl.ds(pl.multiple_of(r * NORM_SLAB, NORM_SLAB), NORM_SLAB)
            x = x_ref[rows, :]
            y = x * lax.rsqrt(jnp.mean(x * x, axis=-1, keepdims=True) + EPS)
            k = ((i % tiles_per_batch) * tm + r * NORM_SLAB < ctx_len).astype(jnp.int32)
            h_ref[rows, :] = (y * mul_ref[k] + add_ref[k]).astype(BF16)
            return carry

        lax.fori_loop(0, tm // NORM_SLAB, slab, 0, unroll=NORM_UNROLL)
        h = h_ref[...]
        if has_dt:
            dt_ref[...] = jnp.dot(h, wdt_ref[...], preferred_element_type=F32)
            dtT_ref[...] = lax.dot_general(wdtT_ref[...], h, (((1,), (1,)), ((), ())),
                                           preferred_element_type=F32)

    def finish(gain_idx, rope, scale):
        w = wqk_ref[...] if rope else w_ref[...].astype(BF16)
        acc = jnp.dot(h_ref[...], w, preferred_element_type=F32)
        if gain_idx is None and not rope and scale is None:
            o_ref[...] = acc.astype(o_ref.dtype)
            return
        if gain_idx is not None:
            acc = acc * lax.rsqrt(_head_sums(acc * acc, tn) * (1.0 / HEAD_DIM) + EPS)
        for hh in range(tn // HEAD_DIM):
            cols = slice(hh * HEAD_DIM, (hh + 1) * HEAD_DIM)
            v = acc[:, cols]
            if gain_idx is not None:
                v = v * gain_ref[gain_idx:gain_idx + 1, :]
            if rope:
                v = v * cos_ref[...] + pltpu.roll(v, HEAD_DIM // 2, 1) * sin_ref[...]
            if scale is not None:
                v = v * scale
            o_ref[:, cols] = v.astype(o_ref.dtype)

    for (lo, hi, gain_idx, rope, scale) in segs:
        pl.when((j >= lo) & (j < hi))(functools.partial(finish, gain_idx, rope, scale))


def _in_proj(xc, norm_g, mods, w, dm, *, tm, tn, segs, rope=None, w_qk=None, gains=None, w_dt=None):
    m, d = xc.shape
    n = segs[-1][1] * tn
    tpb = _rows(dm) // tm
    mods3 = mods.reshape(MOD_ROWS, 1, N_MODS * d)
    has_rope, has_gain, has_dt = rope is not None, gains is not None, w_dt is not None
    qk_tiles = w_qk.shape[1] // tn if has_rope else 0

    def lat(col):
        return pl.BlockSpec((1, 1, d), lambda i, j: (i // tpb, 0, col))

    def ctx(col):
        return pl.BlockSpec((1, 1, d), lambda i, j: (dm.batch, 0, col))

    args = [xc, norm_g.reshape(1, d), mods3, mods3, mods3, mods3, w]
    in_specs = [pl.BlockSpec((tm, d), lambda i, j: (i, 0)),
                pl.BlockSpec((1, d), lambda i, j: (0, 0)),
                lat(0), lat(1), ctx(0), ctx(1),
                pl.BlockSpec((d, tn), lambda i, j: (0, jnp.maximum(j, qk_tiles)))]
    if has_rope:
        args += list(rope) + [w_qk]
        in_specs += [pl.BlockSpec((tm, HEAD_DIM), lambda i, j: (i % tpb, 0))] * 2
        in_specs.append(pl.BlockSpec((d, tn), lambda i, j: (0, jnp.minimum(j, qk_tiles - 1))))
    if has_gain:
        args.append(gains)
        in_specs.append(pl.BlockSpec(gains.shape, lambda i, j: (0, 0)))
    out_shape = [jax.ShapeDtypeStruct((m, n), BF16)]
    out_specs = [pl.BlockSpec((tm, tn), lambda i, j: (i, j))]
    if has_dt:
        n_dt = w_dt.shape[1]
        args += [w_dt, w_dt.T]
        in_specs += [pl.BlockSpec((d, n_dt), lambda i, j: (0, 0)),
                     pl.BlockSpec((n_dt, d), lambda i, j: (0, 0))]
        out_shape += [jax.ShapeDtypeStruct((m, n_dt), F32), jax.ShapeDtypeStruct((n_dt, m), F32)]
        out_specs += [pl.BlockSpec((tm, n_dt), lambda i, j: (i, 0)),
                      pl.BlockSpec((n_dt, tm), lambda i, j: (0, i))]
    kern = functools.partial(_in_proj_kernel, tm=tm, tn=tn, tiles_per_batch=tpb, ctx_len=dm.ctx,
                             segs=segs, has_rope=has_rope, has_gain=has_gain, has_dt=has_dt)
    out = pl.pallas_call(
        kern,
        grid=(m // tm, n // tn),
        in_specs=in_specs,
        out_specs=out_specs,
        out_shape=out_shape,
        scratch_shapes=[pltpu.VMEM((tm, d), BF16),
                        pltpu.VMEM((2, NORM_SLAB, d), F32),
                        pltpu.VMEM((2, NORM_SLAB, d), F32)],
        compiler_params=_params(("parallel", "arbitrary")),
        name="in_proj",
    )(*args)
    return out if has_dt else out[0]


def _out_proj_kernel(g_ref, w_ref, x_ref, gl_ref, gc_ref, o_ref, *, tm, tiles_per_batch, ctx_len):
    i = pl.program_id(0)
    acc = jnp.dot(g_ref[...], w_ref[...].astype(BF16), preferred_element_type=F32)
    row = (i % tiles_per_batch) * tm + lax.broadcasted_iota(jnp.int32, (tm, 1), 0)
    gate = jnp.where(row < ctx_len, gc_ref[0], gl_ref[0])
    o_ref[...] = x_ref[...] + gate * acc


def _out_proj(g, w, xc, mods, dm, *, tm, tn):
    m, k = g.shape
    d = w.shape[1]
    tpb = _rows(dm) // tm
    tiles_n = d // tn
    mods3 = mods.reshape(MOD_ROWS, 1, N_MODS * d)
    gate_col = 2 * tiles_n
    kern = functools.partial(_out_proj_kernel, tm=tm, tiles_per_batch=tpb, ctx_len=dm.ctx)
    return pl.pallas_call(
        kern,
        grid=(m // tm, tiles_n),
        in_specs=[pl.BlockSpec((tm, k), lambda i, j: (i, 0)),
                  pl.BlockSpec((k, tn), lambda i, j: (0, j)),
                  pl.BlockSpec((tm, tn), lambda i, j: (i, j)),
                  pl.BlockSpec((1, 1, tn), lambda i, j: (i // tpb, 0, gate_col + j)),
                  pl.BlockSpec((1, 1, tn), lambda i, j: (dm.batch, 0, gate_col + j))],
        out_specs=pl.BlockSpec((tm, tn), lambda i, j: (i, j)),
        out_shape=jax.ShapeDtypeStruct((m, d), F32),
        compiler_params=_params(("parallel", "arbitrary")),
        name="out_proj",
    )(g, w, xc, mods3, mods3)


def _lane_tile_reduce(v, op):
    part = v[:, :HEAD_DIM]
    for t in range(1, v.shape[1] // HEAD_DIM):
        part = op(part, v[:, t * HEAD_DIM:(t + 1) * HEAD_DIM])
    return part


def _scores(pairs, k_ref, krows):
    blocks = []
    for q, kcols in pairs:
        half = q.shape[0] // 2
        for qq in (q[:half], q[half:]):
            blocks.append(lax.dot_general(qq, k_ref[krows, kcols], (((1,), (1,)), ((), ())),
                                          preferred_element_type=F32))
    return jnp.concatenate(blocks, axis=0)


def _weighted(p, v_ref, krows, with_sums=False):
    half = p.shape[0] // 2
    pb = p.astype(BF16)
    v = v_ref[krows, :]
    if with_sums:
        v = jnp.concatenate([v, jnp.ones_like(v)], axis=1)
    return jnp.concatenate([jnp.dot(pp, v, preferred_element_type=F32)
                            for pp in (pb[:half], pb[half:])], axis=0)


def _attend_in_place(pairs, k_ref, v_ref, key_rows, tk):
    chunks = [slice(c * tk, (c + 1) * tk) for c in range(key_rows // tk)]
    s = [_scores(pairs, k_ref, krows) for krows in chunks]
    m_part = functools.reduce(jnp.maximum, [_lane_tile_reduce(sc, jnp.maximum) for sc in s])
    m = jnp.max(m_part, axis=1, keepdims=True)
    p = [jnp.exp2(sc - m) for sc in s]
    l_part = functools.reduce(jnp.add, [_lane_tile_reduce(pc, jnp.add) for pc in p])
    acc = functools.reduce(jnp.add, [_weighted(pc, v_ref, krows) for pc, krows in zip(p, chunks)])
    return acc * (1.0 / jnp.sum(l_part, axis=1, keepdims=True))


def _attend_pipelined(pairs_next, s_next_ref, s_cur_ref, k_ref, v_ref, m_ref, key_rows, tk):
    n_chunks = key_rows // tk
    have_cur = s_cur_ref is not None
    dv = v_ref.shape[1]
    mxu_sums = dv == HEAD_DIM
    m_part = acc = None
    for c in range(n_chunks):
        krows = slice(c * tk, (c + 1) * tk)
        if pairs_next is not None:
            s = _scores(pairs_next, k_ref, krows)
            s_next_ref[c] = s
            part = _lane_tile_reduce(s, jnp.maximum)
            m_part = part if m_part is None else jnp.maximum(m_part, part)
        if have_cur:
            p = jnp.concatenate(
                [jnp.exp2(s_cur_ref[c, :, t * HEAD_DIM:(t + 1) * HEAD_DIM] - m_ref[0])
                 for t in range(tk // HEAD_DIM)], axis=1)
            if not mxu_sums:
                part = _lane_tile_reduce(p, jnp.add)
                m_ref[1] = part if c == 0 else m_ref[1] + part
            pv = _weighted(p, v_ref, krows, with_sums=mxu_sums)
            acc = pv if acc is None else acc + pv
    if pairs_next is not None:
        m_ref[0] = jnp.broadcast_to(jnp.max(m_part, axis=1, keepdims=True), m_ref.shape[1:])
    if not have_cur:
        return None
    if mxu_sums:
        return acc[:, :dv] * (1.0 / acc[:, dv:])
    return acc * (1.0 / jnp.sum(m_ref[1], axis=1, keepdims=True))


def _attention_steps(step, queries, finish, k_ref, v_ref, o_ref, s_refs, m_ref, *, tq, tk, ctx_len, rows,
                     ctx_out):
    n_lat = (rows - ctx_len) // tq

    def lat_row0(t):
        return pl.multiple_of(ctx_len + (t - 1) * tq, math.gcd(ctx_len, tq))

    def pipe(t_next, t_cur, cur_parity):
        pairs = None if t_next is None else queries(lat_row0(t_next), tq)
        o = _attend_pipelined(pairs, None if t_next is None else s_refs[1 - cur_parity],
                              None if t_cur is None else s_refs[cur_parity],
                              k_ref, v_ref, m_ref, rows, tk)
        if t_cur is not None:
            finish(lat_row0(t_cur), tq, o)

    @pl.when(step == 0)
    def _():
        if ctx_out:
            finish(0, ctx_len, _attend_in_place(queries(0, ctx_len), k_ref, v_ref, ctx_len, tk))
        else:
            o_ref[:ctx_len, :] = jnp.zeros((ctx_len, o_ref.shape[1]), o_ref.dtype)
        pipe(1, None, 0)

    if n_lat > 1:
        for parity in range(2):
            pl.when((step >= 1) & (step < n_lat) & (step % 2 == parity))(
                functools.partial(pipe, step + 1, step, parity))
    pl.when(step == n_lat)(functools.partial(pipe, None, step, n_lat % 2))


def _gqa_kernel(q_ref, k_ref, v_ref, z_ref, o_ref, s0_ref, s1_ref, m_ref, *, tq, tk, group, ctx_len, rows,
                ctx_out):
    def queries(row0, nq):
        q = jnp.concatenate([q_ref[pl.ds(row0, nq), g * HEAD_DIM:(g + 1) * HEAD_DIM]
                             for g in range(group)], axis=0)
        return [(q, slice(0, HEAD_DIM))]

    def finish(row0, nq, o):
        qrows = pl.ds(row0, nq)
        for g in range(group):
            cols = slice(g * HEAD_DIM, (g + 1) * HEAD_DIM)
            z = z_ref[qrows, cols].astype(F32)
            o_ref[qrows, cols] = (o[g * nq:(g + 1) * nq] * _silu(z)).astype(o_ref.dtype)

    _attention_steps(pl.program_id(2), queries, finish, k_ref, v_ref, o_ref, (s0_ref, s1_ref), m_ref,
                     tq=tq, tk=tk, ctx_len=ctx_len, rows=rows, ctx_out=ctx_out)


def _gqa_attention(proj, dm, *, tq, tk, ctx_out):
    rows = _rows(dm)
    m = proj.shape[0]
    width = dm.d
    heads = width // HEAD_DIM
    group = heads // GQA_KV_HEADS
    gw = group * HEAD_DIM
    k_col0 = width // HEAD_DIM
    v_col0 = k_col0 + GQA_KV_HEADS
    z_col0 = (width + 2 * GQA_KV_HEADS * HEAD_DIM) // gw
    kern = functools.partial(_gqa_kernel, tq=tq, tk=tk, group=group, ctx_len=dm.ctx, rows=rows,
                             ctx_out=ctx_out)
    return pl.pallas_call(
        kern,
        grid=(dm.batch, GQA_KV_HEADS, 1 + dm.seq // tq),
        in_specs=[pl.BlockSpec((rows, gw), lambda b, h, i: (b, h)),
                  pl.BlockSpec((rows, HEAD_DIM), lambda b, h, i: (b, k_col0 + h)),
                  pl.BlockSpec((rows, HEAD_DIM), lambda b, h, i: (b, v_col0 + h)),
                  pl.BlockSpec((rows, gw), lambda b, h, i: (b, z_col0 + h))],
        out_specs=pl.BlockSpec((rows, gw), lambda b, h, i: (b, h)),
        out_shape=jax.ShapeDtypeStruct((m, width), BF16),
        scratch_shapes=[pltpu.VMEM((rows // tk, group * tq, tk), F32),
                        pltpu.VMEM((rows // tk, group * tq, tk), F32),
                        pltpu.VMEM((2, group * tq, HEAD_DIM), F32)],
        compiler_params=_params(("parallel", "parallel", "arbitrary")),
        name="gqa_attention",
    )(proj, proj, proj, proj)


def _diff_kernel(lam_ref, q_ref, k_ref, v_ref, z_ref, subln_ref, o_ref, s0_ref, s1_ref, m_ref,
                 *, tq, tk, ctx_len, rows, lambda_init):
    lv = lam_ref[...]
    lam = (jnp.exp(jnp.sum(lv[0:1] * lv[1:2], axis=1, keepdims=True))
           - jnp.exp(jnp.sum(lv[2:3] * lv[3:4], axis=1, keepdims=True)) + lambda_init)

    def queries(row0, nq):
        halves = [slice(t * HEAD_DIM, (t + 1) * HEAD_DIM) for t in range(2)]
        return [(q_ref[pl.ds(row0, nq), cols], cols) for cols in halves]

    def finish(row0, nq, o):
        qrows = pl.ds(row0, nq)
        o = o[:nq] - lam * o[nq:]
        o = o * lax.rsqrt(jnp.mean(o * o, axis=-1, keepdims=True) + EPS) * subln_ref[...]
        o = o * (1.0 - lambda_init)
        o_ref[qrows, :] = (o * _silu(z_ref[qrows, :].astype(F32))).astype(o_ref.dtype)

    _attention_steps(pl.program_id(2), queries, finish, k_ref, v_ref, o_ref, (s0_ref, s1_ref), m_ref,
                     tq=tq, tk=tk, ctx_len=ctx_len, rows=rows, ctx_out=True)


def _diff_attention(proj, lam_vecs, subln, dm, *, tq, tk, lambda_init):
    rows = _rows(dm)
    m = proj.shape[0]
    width = dm.d
    hw = 2 * HEAD_DIM
    heads = width // hw
    kern = functools.partial(_diff_kernel, tq=tq, tk=tk, ctx_len=dm.ctx, rows=rows, lambda_init=lambda_init)

    def head_block(first):
        return pl.BlockSpec((rows, hw), lambda b, h, i: (b, first + h))

    return pl.pallas_call(
        kern,
        grid=(dm.batch, heads, 1 + dm.seq // tq),
        in_specs=[pl.BlockSpec(lam_vecs.shape, lambda b, h, i: (0, 0)),
                  head_block(0), head_block(heads), head_block(2 * heads), head_block(3 * heads),
                  pl.BlockSpec((1, hw), lambda b, h, i: (0, 0))],
        out_specs=head_block(0),
        out_shape=jax.ShapeDtypeStruct((m, width), BF16),
        scratch_shapes=[pltpu.VMEM((rows // tk, 2 * tq, tk), F32),
                        pltpu.VMEM((rows // tk, 2 * tq, tk), F32),
                        pltpu.VMEM((2, 2 * tq, HEAD_DIM), F32)],
        compiler_params=_params(("parallel", "parallel", "arbitrary")),
        name="diff_attention",
    )(lam_vecs, proj, proj, proj, proj, subln.reshape(1, hw))


CONV_TAPS = (-1, 0, 1, 2)
CONV_STRIP = 256


def _conv_kernel(cur_ref, prev_ref, next_ref, shift_ref, w_ref, b_ref, o_ref, *, tr, tiles_per_batch,
                 ctx_tiles):
    r = pl.program_id(1)
    edge = BF16_SUBLANES
    centre = CONV_TAPS.index(0)
    has_prev = (r != 0) & (r != ctx_tiles)
    has_next = (r != ctx_tiles - 1) & (r != tiles_per_batch - 1)
    row = lax.broadcasted_iota(jnp.int32, (edge, 1), 0)
    for c0 in range(0, cur_ref.shape[1], CONV_STRIP):
        cols = slice(c0, c0 + CONV_STRIP)
        xb = cur_ref[:, cols]
        w = [w_ref[t:t + 1, cols] for t in range(SSM_CONV)]
        y = b_ref[:, cols] + w[centre] * xb.astype(F32)
        for k, tap in enumerate(t for t in range(SSM_CONV) if t != centre):
            y = y + w[tap] * jnp.dot(shift_ref[k], xb, preferred_element_type=F32)
        o_ref[edge:tr - edge, cols] = _silu(y[edge:tr - edge]).astype(o_ref.dtype)
        prev = jnp.where(has_prev, prev_ref[edge - 1:edge, cols].astype(F32), 0.0)
        nxt0 = jnp.where(has_next, next_ref[0:1, cols].astype(F32), 0.0)
        nxt1 = jnp.where(has_next, next_ref[1:2, cols].astype(F32), 0.0)
        top = y[:edge] + jnp.where(row == 0, w[0] * prev, 0.0)
        o_ref[:edge, cols] = _silu(top).astype(o_ref.dtype)
        bottom = (y[tr - edge:] + jnp.where(row == edge - 2, w[3] * nxt0, 0.0)
                  + jnp.where(row == edge - 1, w[2] * nxt0 + w[3] * nxt1, 0.0))
        o_ref[tr - edge:, cols] = _silu(bottom).astype(o_ref.dtype)


def _ssd_conv(proj, conv_w, conv_b, dm, *, col0, tr, tn):
    rows = _rows(dm)
    m = proj.shape[0]
    n = conv_w.shape[1]
    tpb = rows // tr
    sub = tr // BF16_SUBLANES
    last = m // BF16_SUBLANES - 1
    c0 = col0 // tn
    shifts = jnp.stack([jnp.eye(tr, k=off, dtype=BF16) for off in CONV_TAPS if off != 0])
    kern = functools.partial(_conv_kernel, tr=tr, tiles_per_batch=tpb, ctx_tiles=dm.ctx // tr)
    return pl.pallas_call(
        kern,
        grid=(dm.batch, tpb, n // tn),
        in_specs=[pl.BlockSpec((tr, tn), lambda b, r, j: (b * tpb + r, c0 + j)),
                  pl.BlockSpec((BF16_SUBLANES, tn),
                               lambda b, r, j: (jnp.maximum((b * tpb + r) * sub - 1, 0), c0 + j)),
                  pl.BlockSpec((BF16_SUBLANES, tn),
                               lambda b, r, j: (jnp.minimum((b * tpb + r + 1) * sub, last), c0 + j)),
                  pl.BlockSpec(shifts.shape, lambda b, r, j: (0, 0, 0)),
                  pl.BlockSpec((SSM_CONV, tn), lambda b, r, j: (0, j)),
                  pl.BlockSpec((1, tn), lambda b, r, j: (0, j))],
        out_specs=pl.BlockSpec((tr, tn), lambda b, r, j: (b * tpb + r, j)),
        out_shape=jax.ShapeDtypeStruct((m, n), BF16),
        compiler_params=_params(("parallel", "parallel", "parallel")),
        name="ssd_conv",
    )(proj, proj, proj, shifts, conv_w, conv_b.reshape(1, n))


def _split_dot(a, b_hi_exact, dims):
    hi = a.astype(BF16)
    lo = (a - hi.astype(F32)).astype(BF16)
    if dims == "ab":
        return (jnp.dot(hi, b_hi_exact, preferred_element_type=F32)
                + jnp.dot(lo, b_hi_exact, preferred_element_type=F32))
    return (jnp.dot(b_hi_exact, hi, preferred_element_type=F32)
            + jnp.dot(b_hi_exact, lo, preferred_element_type=F32))


def _ssd_kernel(*refs, heads, reverse, finish):
    it = iter(refs)
    (x_ref, b_ref, c_ref, dt_ref, dtT_ref, bias_ref, biasT_ref, alog_ref, alogT_ref,
     expand_ref) = (next(it) for _ in range(10))
    other_ref = z_ref = dskip_ref = gain_ref = None
    if finish:
        other_ref, z_ref, dskip_ref, gain_ref = (next(it) for _ in range(4))
    y_ref, state_ref = next(it), next(it)
    q = SSM_CHUNK
    hpg = heads // SSM_GROUPS
    gw = hpg * SSM_HEAD_DIM
    pw = 2 * SSM_HEAD_DIM
    step = pl.program_id(1)

    @pl.when(step == 0)
    def _():
        state_ref[...] = jnp.zeros(state_ref.shape, F32)

    def softplus(v):
        return jnp.maximum(v, 0.0) + jnp.log(1.0 + jnp.exp(-jnp.abs(v)))

    dcol = heads if reverse else 0
    dt = softplus(dt_ref[:, dcol:dcol + heads] + bias_ref[...])
    dtT = softplus(dtT_ref[...] + biasT_ref[...])
    a = dt * -jnp.exp(alog_ref[...])
    aT = dtT * -jnp.exp(alogT_ref[...])
    ri = lax.broadcasted_iota(jnp.int32, (q, q), 0)
    ci = lax.broadcasted_iota(jnp.int32, (q, q), 1)
    before = (ci >= ri) if reverse else (ci <= ri)
    tri = before.astype(BF16)
    triT = ((ri >= ci) if reverse else (ri <= ci)).astype(BF16)
    cum = _split_dot(a, tri, "ba")
    cumT = _split_dot(aT, triT, "ab")
    end = 0 if reverse else q - 1
    total = cum[end:end + 1, :]
    expand = expand_ref[...]
    w_full = jnp.dot((jnp.exp(total - cum) * dt).astype(BF16), expand,
                     preferred_element_type=F32)
    e_full = jnp.dot(jnp.exp(cum).astype(BF16), expand, preferred_element_type=F32)
    cd_full = _split_dot(jnp.broadcast_to(jnp.exp(total), (8, heads)), expand, "ab")[0:1]
    lcumT = cumT - jnp.log(dtT)
    low = lax.broadcasted_iota(jnp.int32, (q, pw), 1) < SSM_HEAD_DIM

    for g in range(SSM_GROUPS):
        gcols = slice(g * gw, (g + 1) * gw)
        ncols = slice(g * SSM_STATE, (g + 1) * SSM_STATE)
        bg = b_ref[:, ncols]
        cg = c_ref[:, ncols]
        cb = lax.dot_general(cg, bg, (((1,), (1,)), ((), ())), preferred_element_type=F32)
        st = state_ref[:, gcols]
        y_off = jnp.dot(cg, st.astype(BF16), preferred_element_type=F32) * e_full[:, gcols]
        ys = []
        for j in range(hpg // 2):
            xp = x_ref[:, g * gw + j * pw:g * gw + (j + 1) * pw].astype(F32)
            xbd = jnp.concatenate([jnp.where(low, xp, 0.0), jnp.where(low, 0.0, xp)], axis=0).astype(BF16)
            mats = []
            for hh in range(2):
                h = g * hpg + 2 * j + hh
                diff = cum[:, h:h + 1] - lcumT[h:h + 1, :]
                mats.append((cb * jnp.exp(jnp.where(before, diff, -jnp.inf))).astype(BF16))
            ys.append(jnp.dot(jnp.concatenate(mats, axis=1), xbd, preferred_element_type=F32)
                      + y_off[:, j * pw:(j + 1) * pw])
        xg = x_ref[:, gcols].astype(F32)
        if finish:
            y = jnp.concatenate(ys, axis=1) + other_ref[:, gcols].astype(F32) + dskip_ref[:, gcols] * xg
            v = y * _silu(z_ref[:, gcols].astype(F32))
            v = v * lax.rsqrt(jnp.mean(v * v, axis=-1, keepdims=True) + EPS) * gain_ref[:, gcols]
            y_ref[:, gcols] = v.astype(y_ref.dtype)
        else:
            for j, y in enumerate(ys):
                y_ref[:, g * gw + j * pw:g * gw + (j + 1) * pw] = y.astype(y_ref.dtype)
        xw = (xg * w_full[:, gcols]).astype(BF16)
        bgT = bg.astype(F32).T.astype(BF16)
        state_ref[:, gcols] = st * cd_full[:, gcols] + jnp.dot(bgT, xw, preferred_element_type=F32)


def _ssd_scan(xbc, dt, dtT, dt_bias, a_log, dm, *, inner, reverse, tail=None):
    rows = _rows(dm)
    m = xbc.shape[0]
    q = SSM_CHUNK
    heads = inner // SSM_HEAD_DIM
    nc = rows // q
    ctx_chunks = dm.ctx // q
    sw = SSM_GROUPS * SSM_STATE
    d = 1 if reverse else 0

    def chunk(b, s):
        if reverse:
            s = jnp.where(s < ctx_chunks, ctx_chunks - 1 - s, nc - 1 - (s - ctx_chunks))
        return b * nc + s

    expand = jnp.repeat(jnp.eye(heads, dtype=BF16), SSM_HEAD_DIM, axis=1)
    row_spec = pl.BlockSpec((q, inner), lambda b, s: (chunk(b, s), 0))
    vec_spec = pl.BlockSpec((1, inner), lambda b, s: (0, 0))
    args = [xbc, xbc, xbc, dt, dtT, dt_bias[d][None, :], dt_bias[d][:, None], a_log[d][None, :],
            a_log[d][:, None], expand]
    in_specs = [row_spec,
                pl.BlockSpec((q, sw), lambda b, s: (chunk(b, s), inner // sw)),
                pl.BlockSpec((q, sw), lambda b, s: (chunk(b, s), inner // sw + 1)),
                pl.BlockSpec((q, 2 * heads), lambda b, s: (chunk(b, s), 0)),
                pl.BlockSpec((heads, q), lambda b, s: (d, chunk(b, s))),
                pl.BlockSpec((1, heads), lambda b, s: (0, 0)),
                pl.BlockSpec((heads, 1), lambda b, s: (0, 0)),
                pl.BlockSpec((1, heads), lambda b, s: (0, 0)),
                pl.BlockSpec((heads, 1), lambda b, s: (0, 0)),
                pl.BlockSpec((heads, inner), lambda b, s: (0, 0))]
    if tail is not None:
        y_other, proj, d_skip, gain = tail
        args += [y_other, proj, jnp.repeat(d_skip.astype(F32), SSM_HEAD_DIM).reshape(1, inner),
                 gain.reshape(1, inner)]
        in_specs += [row_spec, row_spec, vec_spec, vec_spec]
    kern = functools.partial(_ssd_kernel, heads=heads, reverse=reverse, finish=tail is not None)
    return pl.pallas_call(
        kern,
        grid=(dm.batch, nc),
        in_specs=in_specs,
        out_specs=row_spec,
        out_shape=jax.ShapeDtypeStruct((m, inner), BF16),
        scratch_shapes=[pltpu.VMEM((SSM_STATE, inner), F32)],
        compiler_params=_params(("parallel", "arbitrary")),
        name="ssd_scan_bwd" if reverse else "ssd_scan_fwd",
    )(*args)


def _final_norm_kernel(x_ref, g_ref, o_ref):
    x = x_ref[...]
    o_ref[...] = x * lax.rsqrt(jnp.mean(x * x, axis=-1, keepdims=True) + EPS) * g_ref[...]


def _final_norm(xc, gain, dm, *, tr):
    rows = _rows(dm)
    tpb = rows // tr
    lat_tiles = dm.seq // tr
    ctx_tiles = dm.ctx // tr
    return pl.pallas_call(
        _final_norm_kernel,
        grid=(dm.batch, lat_tiles),
        in_specs=[pl.BlockSpec((tr, dm.d), lambda b, i: (b * tpb + ctx_tiles + i, 0)),
                  pl.BlockSpec((1, dm.d), lambda b, i: (0, 0))],
        out_specs=pl.BlockSpec((tr, dm.d), lambda b, i: (b * lat_tiles + i, 0)),
        out_shape=jax.ShapeDtypeStruct((dm.batch * dm.seq, dm.d), F32),
        compiler_params=_params(("parallel", "parallel")),
        name="final_norm",
    )(xc, gain.reshape(1, dm.d))


def _rope_tables(dm):
    d_axis = HEAD_DIM // 2
    t = jnp.arange(dm.seq, dtype=jnp.int32)
    inv_freq = jnp.power(ROPE_THETA, -jnp.arange(0, d_axis, 2, dtype=F32) / d_axis)
    ang_r = (t // GRID_W).astype(F32)[:, None] * inv_freq[None, :]
    ang_c = (t % GRID_W).astype(F32)[:, None] * inv_freq[None, :]
    cos = jnp.concatenate([jnp.cos(ang_r), jnp.cos(ang_c), jnp.cos(ang_r), jnp.cos(ang_c)], axis=1)
    sin = jnp.concatenate([-jnp.sin(ang_r), -jnp.sin(ang_c), jnp.sin(ang_r), jnp.sin(ang_c)], axis=1)
    cos = jnp.concatenate([jnp.ones((dm.ctx, HEAD_DIM), F32), cos], axis=0)
    sin = jnp.concatenate([jnp.zeros((dm.ctx, HEAD_DIM), F32), sin], axis=0)
    return cos, sin


def _relayout_kernel(w_ref, o_ref):
    tn = w_ref.shape[1]
    quarter = HEAD_DIM // 4
    r = lax.broadcasted_iota(jnp.int32, (tn, tn), 0)
    c = lax.broadcasted_iota(jnp.int32, (tn, tn), 1)
    blk = (c // quarter) % 4
    src_blk = jnp.where(blk == 1, 2, jnp.where(blk == 2, 1, blk))
    src = (c // HEAD_DIM) * HEAD_DIM + src_blk * quarter + c % quarter
    perm = (r == src).astype(BF16)
    o_ref[...] = jnp.dot(w_ref[...].astype(BF16), perm, preferred_element_type=F32).astype(BF16)


def _qk_weights(w_in, qk_cols, tn):
    d = w_in.shape[0]
    return pl.pallas_call(
        _relayout_kernel,
        grid=(qk_cols // tn,),
        in_specs=[pl.BlockSpec((d, tn), lambda j: (0, j))],
        out_specs=pl.BlockSpec((d, tn), lambda j: (0, j)),
        out_shape=jax.ShapeDtypeStruct((d, qk_cols), BF16),
        compiler_params=_params(("parallel",)),
        name="qk_relayout",
    )(w_in)


def _lambda_init(layer_idx):
    return 0.8 - 0.6 * math.exp(-0.3 * layer_idx)


def _tiles(dm):
    rows = _rows(dm)
    tm = rows // 2 if (rows // 2) % 128 == 0 else rows
    return dict(tm=tm, tn=min(512, dm.d), tq=dm.ctx, tk=dm.ctx)


def _forward(dm, x, c, ctx, c_ctx, layers, final_norm):
    d = dm.d
    til = _tiles(dm)
    tm, tn, tq, tk = til["tm"], til["tn"], til["tq"], til["tk"]
    scale = HEAD_DIM ** -0.5 * math.log2(math.e)
    xc =jnp.concatenate([ctx, x], axis=1).reshape(dm.batch * _rows(dm), d)
    cvec = jnp.concatenate([c, c_ctx[None, :], jnp.zeros((MOD_ROWS - dm.batch - 1, d), F32)], axis=0)
    rope = _rope_tables(dm)
    q_tiles = d // tn

    for i, (kind, (w_mod, b_mod, norm), p) in enumerate(layers):
        mods = _modulation(cvec, w_mod, b_mod)
        if kind == "gqa":
            w_in, q_gain, k_gain, w_out = p
            kv_tiles = GQA_KV_HEADS * HEAD_DIM // tn
            n_tiles = w_in.shape[1] // tn
            segs = ((0, q_tiles, 0, True, scale),
                    (q_tiles, q_tiles + kv_tiles, 1, True, None),
                    (q_tiles + kv_tiles, n_tiles, None, False, None))
            w_qk = _qk_weights(w_in, (q_tiles + kv_tiles) * tn, tn)
            gains = jnp.stack([q_gain, k_gain]).astype(F32)[:, _rope_layout()]
            proj = _in_proj(xc, norm, mods, w_in, dm, tm=tm, tn=tn, segs=segs, rope=rope, w_qk=w_qk,
                            gains=gains)
            g = _gqa_attention(proj, dm, tq=tq, tk=tk, ctx_out=i < len(layers) - 1)
        elif kind == "diff":
            w_in, lq1, lk1, lq2, lk2, subln, w_out = p
            n_tiles = w_in.shape[1] // tn
            segs = ((0, q_tiles, None, True, scale),
                    (q_tiles, 2 * q_tiles, None, True, None),
                    (2 * q_tiles, n_tiles, None, False, None))
            proj = _in_proj(xc, norm, mods, w_in, dm, tm=tm, tn=tn, segs=segs, rope=rope,
                            w_qk=_qk_weights(w_in, 2 * q_tiles * tn, tn))
            lam_vecs = jnp.stack([lq1, lk1, lq2, lk2]).astype(F32)
            g = _diff_attention(proj, lam_vecs, subln, dm, tq=2 * tq, tk=tk, lambda_init=_lambda_init(i))
        else:
            w_in, conv_w, conv_b, dt_bias, a_log, d_skip, ssm_norm, w_out = p
            inner = w_out.shape[0]
            n_main = 2 * inner + 2 * SSM_GROUPS * SSM_STATE
            segs = ((0, n_main // tn, None, False, None),)
            proj, dt, dtT = _in_proj(xc, norm, mods, w_in, dm, tm=tm, tn=tn,
                                     segs=segs, w_dt=w_in[:, n_main:].astype(BF16))
            xbc = _ssd_conv(proj, conv_w, conv_b, dm, col0=inner, tr=tq,
                            tn=math.gcd(inner, conv_w.shape[1]))
            y_f = _ssd_scan(xbc, dt, dtT, dt_bias, a_log, dm, inner=inner, reverse=False)
            g = _ssd_scan(xbc, dt, dtT, dt_bias, a_log, dm, inner=inner, reverse=True,
                          tail=(y_f, proj, d_skip, ssm_norm))
        wide = w_out.shape[0] > d
        xc = _out_proj(g, w_out, xc, mods, dm, tm=tm if wide else _rows(dm), tn=tn if wide else tn // 2)

    out = _final_norm(xc, final_norm, dm, tr=tq)
    return out.reshape(dm.batch, dm.seq, d)


def kernel(x, c, ctx, c_ctx, l0_w_mod, l0_b_mod, l0_norm, l0_w_in, l0_q_gain, l0_k_gain, l0_w_out, l1_w_mod, l1_b_mod, l1_norm, l1_w_in, l1_lambda_q1, l1_lambda_k1, l1_lambda_q2, l1_lambda_k2, l1_subln, l1_w_out, l2_w_mod, l2_b_mod, l2_norm, l2_w_in, l2_conv_w, l2_conv_b, l2_dt_bias, l2_A_log, l2_D, l2_ssm_norm, l2_w_out, l3_w_mod, l3_b_mod, l3_norm, l3_w_in, l3_q_gain, l3_k_gain, l3_w_out, final_norm):
    dm = Dims(batch=x.shape[0], seq=x.shape[1], ctx=ctx.shape[1], d=x.shape[2])
    layers = (
        ("gqa", (l0_w_mod, l0_b_mod, l0_norm), (l0_w_in, l0_q_gain, l0_k_gain, l0_w_out)),
        ("diff", (l1_w_mod, l1_b_mod, l1_norm), (l1_w_in, l1_lambda_q1, l1_lambda_k1, l1_lambda_q2,
                                                 l1_lambda_k2, l1_subln, l1_w_out)),
        ("ssd", (l2_w_mod, l2_b_mod, l2_norm), (l2_w_in, l2_conv_w, l2_conv_b, l2_dt_bias, l2_A_log,
                                                l2_D, l2_ssm_norm, l2_w_out)),
        ("gqa", (l3_w_mod, l3_b_mod, l3_norm), (l3_w_in, l3_q_gain, l3_k_gain, l3_w_out)),
    )
    return _forward(dm, x, c, ctx, c_ctx, layers, final_norm)
```

```python
import collections
import functools
import math

import jax
import jax.numpy as jnp
from jax import lax
from jax.experimental import pallas as pl
from jax.experimental.pallas import tpu as pltpu

F32 = jnp.float32
BF16 = jnp.bfloat16

EPS = 1e-6
ROPE_THETA = 10000.0
HEAD_DIM = 128
GQA_KV_HEADS = 4
SSM_HEAD_DIM = 64
SSM_GROUPS = 8
SSM_STATE = 128
SSM_CONV = 4
SSM_CHUNK = 128
GRID_W = 64
DEPTH = 4
N_MODS = 3
MOD_ROWS = 8
MOD_TILE = 1024
NORM_SLAB = 16
NORM_UNROLL = 8
BF16_SUBLANES = 16
VMEM_LIMIT = 56 * 1024 * 1024

Dims = collections.namedtuple("Dims", "batch seq ctx d")


def _rows(dm):
    return dm.ctx + dm.seq


def _silu(v):
    return v * (1.0 / (1.0 + jnp.exp2(v * -math.log2(math.e))))


def _params(semantics):
    return pltpu.CompilerParams(dimension_semantics=semantics, vmem_limit_bytes=VMEM_LIMIT)


def _mod_kernel(c_ref, w_ref, b_ref, o_ref):
    s = _silu(c_ref[...]).astype(BF16)
    o_ref[...] = jnp.dot(s, w_ref[...].astype(BF16), preferred_element_type=F32) + b_ref[...]


def _modulation(cvec, w_mod, b_mod):
    d, n = w_mod.shape
    tn = math.gcd(n, MOD_TILE)
    return pl.pallas_call(
        _mod_kernel,
        grid=(n // tn,),
        in_specs=[pl.BlockSpec((MOD_ROWS, d), lambda j: (0, 0)),
                  pl.BlockSpec((d, tn), lambda j: (0, j)),
                  pl.BlockSpec((1, tn), lambda j: (0, j))],
        out_specs=pl.BlockSpec((MOD_ROWS, tn), lambda j: (0, j)),
        out_shape=jax.ShapeDtypeStruct((MOD_ROWS, n), F32),
        compiler_params=_params(("arbitrary",)),
        name="adaln_mod",
    )(cvec, w_mod, b_mod.reshape(1, n))


def _rope_layout():
    quarter = HEAD_DIM // 4
    idx = jnp.arange(HEAD_DIM).reshape(4, quarter)
    return idx[jnp.array([0, 2, 1, 3])].reshape(HEAD_DIM)


def _head_sums(sq, tn):
    r = lax.broadcasted_iota(jnp.int32, (tn, tn), 0) // HEAD_DIM
    c = lax.broadcasted_iota(jnp.int32, (tn, tn), 1) // HEAD_DIM
    return jnp.dot(sq.astype(BF16), (r == c).astype(BF16), preferred_element_type=F32)


def _in_proj_kernel(*refs, tm, tn, tiles_per_batch, ctx_len, segs, has_rope, has_gain, has_dt):
    it = iter(refs)
    x_ref, g_ref, shl_ref, scl_ref, shc_ref, scc_ref, w_ref = (next(it) for _ in range(7))
    cos_ref = sin_ref = wqk_ref = gain_ref = wdt_ref = wdtT_ref = dt_ref = dtT_ref = None
    if has_rope:
        cos_ref, sin_ref, wqk_ref = next(it), next(it), next(it)
    if has_gain:
        gain_ref = next(it)
    if has_dt:
        wdt_ref, wdtT_ref = next(it), next(it)
    o_ref = next(it)
    if has_dt:
        dt_ref, dtT_ref = next(it), next(it)
    h_ref, mul_ref, add_ref = next(it), next(it), next(it)

    i = pl.program_id(0)
    j = pl.program_id(1)

    @pl.when(j == 0)
    def _():
        d = g_ref.shape[1]
        g = g_ref[...]
        for k, (sc_ref, sh_ref) in enumerate(((scl_ref, shl_ref), (scc_ref, shc_ref))):
            mul_ref[k] = jnp.broadcast_to(g * (1.0 + sc_ref[0]), (NORM_SLAB, d))
            add_ref[k] = jnp.broadcast_to(sh_ref[0], (NORM_SLAB, d))

        def slab(r, carry):
            rows = pl.ds(pl.multiple_of(r * NORM_SLAB, NORM_SLAB), NORM_SLAB)
            x = x_ref[rows, :]
            y = x * lax.rsqrt(jnp.mean(x * x, axis=-1, keepdims=True) + EPS)
            k = ((i % tiles_per_batch) * tm + r * NORM_SLAB < ctx_len).astype(jnp.int32)
            h_ref[rows, :] = (y * mul_ref[k] + add_ref[k]).astype(BF16)
            return carry

        lax.fori_loop(0, tm // NORM_SLAB, slab, 0, unroll=NORM_UNROLL)
        h = h_ref[...]
        if has_dt:
            dt_ref[...] = jnp.dot(h, wdt_ref[...], preferred_element_type=F32)
            dtT_ref[...] = lax.dot_general(wdtT_ref[...], h, (((1,), (1,)), ((), ())),
                                           preferred_element_type=F32)

    def finish(gain_idx, rope, scale):
        w = wqk_ref[...] if rope else w_ref[...].astype(BF16)
        acc = jnp.dot(h_ref[...], w, preferred_element_type=F32)
        if gain_idx is None and not rope and scale is None:
            o_ref[...] = acc.astype(o_ref.dtype)
            return
        if gain_idx is not None:
            acc = acc * lax.rsqrt(_head_sums(acc * acc, tn) * (1.0 / HEAD_DIM) + EPS)
        for hh in range(tn // HEAD_DIM):
            cols = slice(hh * HEAD_DIM, (hh + 1) * HEAD_DIM)
            v = acc[:, cols]
            if gain_idx is not None:
                v = v * gain_ref[gain_idx:gain_idx + 1, :]
            if rope:
                v = v * cos_ref[...] + pltpu.roll(v, HEAD_DIM // 2, 1) * sin_ref[...]
            if scale is not None:
                v = v * scale
            o_ref[:, cols] = v.astype(o_ref.dtype)

    for (lo, hi, gain_idx, rope, scale) in segs:
        pl.when((j >= lo) & (j < hi))(functools.partial(finish, gain_idx, rope, scale))


def _in_proj(xc, norm_g, mods, w, dm, *, tm, tn, segs, rope=None, w_qk=None, gains=None, w_dt=None):
    m, d = xc.shape
    n = segs[-1][1] * tn
    tpb = _rows(dm) // tm
    mods3 = mods.reshape(MOD_ROWS, 1, N_MODS * d)
    has_rope, has_gain, has_dt = rope is not None, gains is not None, w_dt is not None
    qk_tiles = w_qk.shape[1] // tn if has_rope else 0

    def lat(col):
        return pl.BlockSpec((1, 1, d), lambda i, j: (i // tpb, 0, col))

    def ctx(col):
        return pl.BlockSpec((1, 1, d), lambda i, j: (dm.batch, 0, col))

    args = [xc, norm_g.reshape(1, d), mods3, mods3, mods3, mods3, w]
    in_specs = [pl.BlockSpec((tm, d), lambda i, j: (i, 0)),
                pl.BlockSpec((1, d), lambda i, j: (0, 0)),
                lat(0), lat(1), ctx(0), ctx(1),
                pl.BlockSpec((d, tn), lambda i, j: (0, jnp.maximum(j, qk_tiles)))]
    if has_rope:
        args += list(rope) + [w_qk]
        in_specs += [pl.BlockSpec((tm, HEAD_DIM), lambda i, j: (i % tpb, 0))] * 2
        in_specs.append(pl.BlockSpec((d, tn), lambda i, j: (0, jnp.minimum(j, qk_tiles - 1))))
    if has_gain:
        args.append(gains)
        in_specs.append(pl.BlockSpec(gains.shape, lambda i, j: (0, 0)))
    out_shape = [jax.ShapeDtypeStruct((m, n), BF16)]
    out_specs = [pl.BlockSpec((tm, tn), lambda i, j: (i, j))]
    if has_dt:
        n_dt = w_dt.shape[1]
        args += [w_dt, w_dt.T]
        in_specs += [pl.BlockSpec((d, n_dt), lambda i, j: (0, 0)),
                     pl.BlockSpec((n_dt, d), lambda i, j: (0, 0))]
        out_shape += [jax.ShapeDtypeStruct((m, n_dt), F32), jax.ShapeDtypeStruct((n_dt, m), F32)]
        out_specs += [pl.BlockSpec((tm, n_dt), lambda i, j: (i, 0)),
                      pl.BlockSpec((n_dt, tm), lambda i, j: (0, i))]
    kern = functools.partial(_in_proj_kernel, tm=tm, tn=tn, tiles_per_batch=tpb, ctx_len=dm.ctx,
                             segs=segs, has_rope=has_rope, has_gain=has_gain, has_dt=has_dt)
    out = pl.pallas_call(
        kern,
        grid=(m // tm, n // tn),
        in_specs=in_specs,
        out_specs=out_specs,
        out_shape=out_shape,
        scratch_shapes=[pltpu.VMEM((tm, d), BF16),
                        pltpu.VMEM((2, NORM_SLAB, d), F32),
                        pltpu.VMEM((2, NORM_SLAB, d), F32)],
        compiler_params=_params(("parallel", "arbitrary")),
        name="in_proj",
    )(*args)
    return out if has_dt else out[0]


def _out_proj_kernel(g_ref, w_ref, x_ref, gl_ref, gc_ref, o_ref, *, tm, tiles_per_batch, ctx_len):
    i = pl.program_id(0)
    acc = jnp.dot(g_ref[...], w_ref[...].astype(BF16), preferred_element_type=F32)
    row = (i % tiles_per_batch) * tm + lax.broadcasted_iota(jnp.int32, (tm, 1), 0)
    gate = jnp.where(row < ctx_len, gc_ref[0], gl_ref[0])
    o_ref[...] = x_ref[...] + gate * acc


def _out_proj(g, w, xc, mods, dm, *, tm, tn):
    m, k = g.shape
    d = w.shape[1]
    tpb = _rows(dm) // tm
    tiles_n = d // tn
    mods3 = mods.reshape(MOD_ROWS, 1, N_MODS * d)
    gate_col = 2 * tiles_n
    kern = functools.partial(_out_proj_kernel, tm=tm, tiles_per_batch=tpb, ctx_len=dm.ctx)
    return pl.pallas_call(
        kern,
        grid=(m // tm, tiles_n),
        in_specs=[pl.BlockSpec((tm, k), lambda i, j: (i, 0)),
                  pl.BlockSpec((k, tn), lambda i, j: (0, j)),
                  pl.BlockSpec((tm, tn), lambda i, j: (i, j)),
                  pl.BlockSpec((1, 1, tn), lambda i, j: (i // tpb, 0, gate_col + j)),
                  pl.BlockSpec((1, 1, tn), lambda i, j: (dm.batch, 0, gate_col + j))],
        out_specs=pl.BlockSpec((tm, tn), lambda i, j: (i, j)),
        out_shape=jax.ShapeDtypeStruct((m, d), F32),
        compiler_params=_params(("parallel", "arbitrary")),
        name="out_proj",
    )(g, w, xc, mods3, mods3)


def _lane_tile_reduce(v, op):
    part = v[:, :HEAD_DIM]
    for t in range(1, v.shape[1] // HEAD_DIM):
        part = op(part, v[:, t * HEAD_DIM:(t + 1) * HEAD_DIM])
    return part


def _scores(pairs, k_ref, krows):
    blocks = []
    for q, kcols in pairs:
        half = q.shape[0] // 2
        for qq in (q[:half], q[half:]):
            blocks.append(lax.dot_general(qq, k_ref[krows, kcols], (((1,), (1,)), ((), ())),
                                          preferred_element_type=F32))
    return jnp.concatenate(blocks, axis=0)


def _weighted(p, v_ref, krows, with_sums=False):
    half = p.shape[0] // 2
    pb = p.astype(BF16)
    v = v_ref[krows, :]
    if with_sums:
        v = jnp.concatenate([v, jnp.ones_like(v)], axis=1)
    return jnp.concatenate([jnp.dot(pp, v, preferred_element_type=F32)
                            for pp in (pb[:half], pb[half:])], axis=0)


def _attend_in_place(pairs, k_ref, v_ref, key_rows, tk):
    chunks = [slice(c * tk, (c + 1) * tk) for c in range(key_rows // tk)]
    s = [_scores(pairs, k_ref, krows) for krows in chunks]
    m_part = functools.reduce(jnp.maximum, [_lane_tile_reduce(sc, jnp.maximum) for sc in s])
    m = jnp.max(m_part, axis=1, keepdims=True)
    p = [jnp.exp2(sc - m) for sc in s]
    l_part = functools.reduce(jnp.add, [_lane_tile_reduce(pc, jnp.add) for pc in p])
    acc = functools.reduce(jnp.add, [_weighted(pc, v_ref, krows) for pc, krows in zip(p, chunks)])
    return acc * (1.0 / jnp.sum(l_part, axis=1, keepdims=True))


def _attend_pipelined(pairs_next, s_next_ref, s_cur_ref, finish_cur, k_ref, v_ref, m_ref, key_rows, tk):
    n_chunks = key_rows // tk
    have_cur = s_cur_ref is not None
    dv = v_ref.shape[1]
    mxu_sums = dv == HEAD_DIM
    chunks = [slice(c * tk, (c + 1) * tk) for c in range(n_chunks)]
    acc = None
    for c, krows in enumerate(chunks if have_cur else ()):
        p = jnp.concatenate(
            [jnp.exp2(s_cur_ref[c, :, t * HEAD_DIM:(t + 1) * HEAD_DIM] - m_ref[0])
             for t in range(tk // HEAD_DIM)], axis=1)
        if not mxu_sums:
            part = _lane_tile_reduce(p, jnp.add)
            m_ref[1] = part if c == 0 else m_ref[1] + part
        pv = _weighted(p, v_ref, krows, with_sums=mxu_sums)
        acc = pv if acc is None else acc + pv
    if have_cur:
        if mxu_sums:
            finish_cur(acc[:, :dv] * (1.0 / acc[:, dv:]))
        else:
            finish_cur(acc * (1.0 / jnp.sum(m_ref[1], axis=1, keepdims=True)))
    m_part = None
    for c, krows in enumerate(chunks if pairs_next is not None else ()):
        s = _scores(pairs_next, k_ref, krows)
        s_next_ref[c] = s
        part = _lane_tile_reduce(s, jnp.maximum)
        m_part = part if m_part is None else jnp.maximum(m_part, part)
    if pairs_next is not None:
        m_ref[0] = jnp.broadcast_to(jnp.max(m_part, axis=1, keepdims=True), m_ref.shape[1:])


def _attention_steps(step, queries, finish, k_ref, v_ref, o_ref, s_refs, m_ref, *, tq, tk, ctx_len, rows,
                     ctx_out):
    n_lat = (rows - ctx_len) // tq

    def lat_row0(t):
        return pl.multiple_of(ctx_len + (t - 1) * tq, math.gcd(ctx_len, tq))

    def pipe(t_next, t_cur, cur_parity):
        pairs = None if t_next is None else queries(lat_row0(t_next), tq)
        _attend_pipelined(pairs, None if t_next is None else s_refs[1 - cur_parity],
                          None if t_cur is None else s_refs[cur_parity],
                          None if t_cur is None else functools.partial(finish, lat_row0(t_cur), tq),
                          k_ref, v_ref, m_ref, rows, tk)

    @pl.when(step == 0)
    def _():
        if ctx_out:
            finish(0, ctx_len, _attend_in_place(queries(0, ctx_len), k_ref, v_ref, ctx_len, tk))
        else:
            o_ref[:ctx_len, :] = jnp.zeros((ctx_len, o_ref.shape[1]), o_ref.dtype)
        pipe(1, None, 0)

    if n_lat > 1:
        for parity in range(2):
            pl.when((step >= 1) & (step < n_lat) & (step % 2 == parity))(
                functools.partial(pipe, step + 1, step, parity))
    pl.when(step == n_lat)(functools.partial(pipe, None, step, n_lat % 2))


def _gqa_kernel(q_ref, k_ref, v_ref, z_ref, o_ref, s0_ref, s1_ref, m_ref, *, tq, tk, group, ctx_len, rows,
                ctx_out):
    def queries(row0, nq):
        q = jnp.concatenate([q_ref[pl.ds(row0, nq), g * HEAD_DIM:(g + 1) * HEAD_DIM]
                             for g in range(group)], axis=0)
        return [(q, slice(0, HEAD_DIM))]

    def finish(row0, nq, o):
        qrows = pl.ds(row0, nq)
        for g in range(group):
            cols = slice(g * HEAD_DIM, (g + 1) * HEAD_DIM)
            z = z_ref[qrows, cols].astype(F32)
            o_ref[qrows, cols] = (o[g * nq:(g + 1) * nq] * _silu(z)).astype(o_ref.dtype)

    _attention_steps(pl.program_id(2), queries, finish, k_ref, v_ref, o_ref, (s0_ref, s1_ref), m_ref,
                     tq=tq, tk=tk, ctx_len=ctx_len, rows=rows, ctx_out=ctx_out)


def _gqa_attention(proj, dm, *, tq, tk, ctx_out):
    rows = _rows(dm)
    m = proj.shape[0]
    width = dm.d
    heads = width // HEAD_DIM
    group = heads // GQA_KV_HEADS
    gw = group * HEAD_DIM
    k_col0 = width // HEAD_DIM
    v_col0 = k_col0 + GQA_KV_HEADS
    z_col0 = (width + 2 * GQA_KV_HEADS * HEAD_DIM) // gw
    kern = functools.partial(_gqa_kernel, tq=tq, tk=tk, group=group, ctx_len=dm.ctx, rows=rows,
                             ctx_out=ctx_out)
    return pl.pallas_call(
        kern,
        grid=(dm.batch, GQA_KV_HEADS, 1 + dm.seq // tq),
        in_specs=[pl.BlockSpec((rows, gw), lambda b, h, i: (b, h)),
                  pl.BlockSpec((rows, HEAD_DIM), lambda b, h, i: (b, k_col0 + h)),
                  pl.BlockSpec((rows, HEAD_DIM), lambda b, h, i: (b, v_col0 + h)),
                  pl.BlockSpec((rows, gw), lambda b, h, i: (b, z_col0 + h))],
        out_specs=pl.BlockSpec((rows, gw), lambda b, h, i: (b, h)),
        out_shape=jax.ShapeDtypeStruct((m, width), BF16),
        scratch_shapes=[pltpu.VMEM((rows // tk, group * tq, tk), F32),
                        pltpu.VMEM((rows // tk, group * tq, tk), F32),
                        pltpu.VMEM((2, group * tq, HEAD_DIM), F32)],
        compiler_params=_params(("parallel", "parallel", "arbitrary")),
        name="gqa_attention",
    )(proj, proj, proj, proj)


def _diff_kernel(lam_ref, q_ref, k_ref, v_ref, z_ref, subln_ref, o_ref, s0_ref, s1_ref, m_ref,
                 *, tq, tk, ctx_len, rows, lambda_init):
    lv = lam_ref[...]
    lam = (jnp.exp(jnp.sum(lv[0:1] * lv[1:2], axis=1, keepdims=True))
           - jnp.exp(jnp.sum(lv[2:3] * lv[3:4], axis=1, keepdims=True)) + lambda_init)

    def queries(row0, nq):
        halves = [slice(t * HEAD_DIM, (t + 1) * HEAD_DIM) for t in range(2)]
        return [(q_ref[pl.ds(row0, nq), cols], cols) for cols in halves]

    def finish(row0, nq, o):
        qrows = pl.ds(row0, nq)
        o = o[:nq] - lam * o[nq:]
        o = o * lax.rsqrt(jnp.mean(o * o, axis=-1, keepdims=True) + EPS) * subln_ref[...]
        o = o * (1.0 - lambda_init)
        o_ref[qrows, :] = (o * _silu(z_ref[qrows, :].astype(F32))).astype(o_ref.dtype)

    _attention_steps(pl.program_id(2), queries, finish, k_ref, v_ref, o_ref, (s0_ref, s1_ref), m_ref,
                     tq=tq, tk=tk, ctx_len=ctx_len, rows=rows, ctx_out=True)


def _diff_attention(proj, lam_vecs, subln, dm, *, tq, tk, lambda_init):
    rows = _rows(dm)
    m = proj.shape[0]
    width = dm.d
    hw = 2 * HEAD_DIM
    heads = width // hw
    kern = functools.partial(_diff_kernel, tq=tq, tk=tk, ctx_len=dm.ctx, rows=rows, lambda_init=lambda_init)

    def head_block(first):
        return pl.BlockSpec((rows, hw), lambda b, h, i: (b, first + h))

    return pl.pallas_call(
        kern,
        grid=(dm.batch, heads, 1 + dm.seq // tq),
        in_specs=[pl.BlockSpec(lam_vecs.shape, lambda b, h, i: (0, 0)),
                  head_block(0), head_block(heads), head_block(2 * heads), head_block(3 * heads),
                  pl.BlockSpec((1, hw), lambda b, h, i: (0, 0))],
        out_specs=head_block(0),
        out_shape=jax.ShapeDtypeStruct((m, width), BF16),
        scratch_shapes=[pltpu.VMEM((rows // tk, 2 * tq, tk), F32),
                        pltpu.VMEM((rows // tk, 2 * tq, tk), F32),
                        pltpu.VMEM((2, 2 * tq, HEAD_DIM), F32)],
        compiler_params=_params(("parallel", "parallel", "arbitrary")),
        name="diff_attention",
    )(lam_vecs, proj, proj, proj, proj, subln.reshape(1, hw))


CONV_TAPS = (-1, 0, 1, 2)
CONV_STRIP = 256


def _conv_kernel(cur_ref, prev_ref, next_ref, shift_ref, w_ref, b_ref, o_ref, *, tr, tiles_per_batch,
                 ctx_tiles):
    r = pl.program_id(1)
    edge = BF16_SUBLANES
    centre = CONV_TAPS.index(0)
    has_prev = (r != 0) & (r != ctx_tiles)
    has_next = (r != ctx_tiles - 1) & (r != tiles_per_batch - 1)
    row = lax.broadcasted_iota(jnp.int32, (edge, 1), 0)
    for c0 in range(0, cur_ref.shape[1], CONV_STRIP):
        cols = slice(c0, c0 + CONV_STRIP)
        xb = cur_ref[:, cols]
        w = [w_ref[t:t + 1, cols] for t in range(SSM_CONV)]
        y = b_ref[:, cols] + w[centre] * xb.astype(F32)
        for k, tap in enumerate(t for t in range(SSM_CONV) if t != centre):
            y = y + w[tap] * jnp.dot(shift_ref[k], xb, preferred_element_type=F32)
        o_ref[edge:tr - edge, cols] = _silu(y[edge:tr - edge]).astype(o_ref.dtype)
        prev = jnp.where(has_prev, prev_ref[edge - 1:edge, cols].astype(F32), 0.0)
        nxt0 = jnp.where(has_next, next_ref[0:1, cols].astype(F32), 0.0)
        nxt1 = jnp.where(has_next, next_ref[1:2, cols].astype(F32), 0.0)
        top = y[:edge] + jnp.where(row == 0, w[0] * prev, 0.0)
        o_ref[:edge, cols] = _silu(top).astype(o_ref.dtype)
        bottom = (y[tr - edge:] + jnp.where(row == edge - 2, w[3] * nxt0, 0.0)
                  + jnp.where(row == edge - 1, w[2] * nxt0 + w[3] * nxt1, 0.0))
        o_ref[tr - edge:, cols] = _silu(bottom).astype(o_ref.dtype)


def _ssd_conv(proj, conv_w, conv_b, dm, *, col0, tr, tn):
    rows = _rows(dm)
    m = proj.shape[0]
    n = conv_w.shape[1]
    tpb = rows // tr
    sub = tr // BF16_SUBLANES
    last = m // BF16_SUBLANES - 1
    c0 = col0 // tn
    shifts = jnp.stack([jnp.eye(tr, k=off, dtype=BF16) for off in CONV_TAPS if off != 0])
    kern = functools.partial(_conv_kernel, tr=tr, tiles_per_batch=tpb, ctx_tiles=dm.ctx // tr)
    return pl.pallas_call(
        kern,
        grid=(dm.batch, tpb, n // tn),
        in_specs=[pl.BlockSpec((tr, tn), lambda b, r, j: (b * tpb + r, c0 + j)),
                  pl.BlockSpec((BF16_SUBLANES, tn),
                               lambda b, r, j: (jnp.maximum((b * tpb + r) * sub - 1, 0), c0 + j)),
                  pl.BlockSpec((BF16_SUBLANES, tn),
                               lambda b, r, j: (jnp.minimum((b * tpb + r + 1) * sub, last), c0 + j)),
                  pl.BlockSpec(shifts.shape, lambda b, r, j: (0, 0, 0)),
                  pl.BlockSpec((SSM_CONV, tn), lambda b, r, j: (0, j)),
                  pl.BlockSpec((1, tn), lambda b, r, j: (0, j))],
        out_specs=pl.BlockSpec((tr, tn), lambda b, r, j: (b * tpb + r, j)),
        out_shape=jax.ShapeDtypeStruct((m, n), BF16),
        compiler_params=_params(("parallel", "parallel", "parallel")),
        name="ssd_conv",
    )(proj, proj, proj, shifts, conv_w, conv_b.reshape(1, n))


def _split_dot(a, b_hi_exact, dims):
    hi = a.astype(BF16)
    lo = (a - hi.astype(F32)).astype(BF16)
    if dims == "ab":
        return (jnp.dot(hi, b_hi_exact, preferred_element_type=F32)
                + jnp.dot(lo, b_hi_exact, preferred_element_type=F32))
    return (jnp.dot(b_hi_exact, hi, preferred_element_type=F32)
            + jnp.dot(b_hi_exact, lo, preferred_element_type=F32))


def _ssd_kernel(*refs, heads, reverse, finish):
    it = iter(refs)
    (x_ref, b_ref, c_ref, dt_ref, dtT_ref, bias_ref, biasT_ref, alog_ref, alogT_ref,
     expand_ref) = (next(it) for _ in range(10))
    other_ref = z_ref = dskip_ref = gain_ref = None
    if finish:
        other_ref, z_ref, dskip_ref, gain_ref = (next(it) for _ in range(4))
    y_ref, state_ref = next(it), next(it)
    q = SSM_CHUNK
    hpg = heads // SSM_GROUPS
    gw = hpg * SSM_HEAD_DIM
    pw = 2 * SSM_HEAD_DIM
    step = pl.program_id(1)

    @pl.when(step == 0)
    def _():
        state_ref[...] = jnp.zeros(state_ref.shape, F32)

    def softplus(v):
        return jnp.maximum(v, 0.0) + jnp.log(1.0 + jnp.exp(-jnp.abs(v)))

    dcol = heads if reverse else 0
    dt = softplus(dt_ref[:, dcol:dcol + heads] + bias_ref[...])
    dtT = softplus(dtT_ref[...] + biasT_ref[...])
    a = dt * -jnp.exp(alog_ref[...])
    aT = dtT * -jnp.exp(alogT_ref[...])
    ri = lax.broadcasted_iota(jnp.int32, (q, q), 0)
    ci = lax.broadcasted_iota(jnp.int32, (q, q), 1)
    before = (ci >= ri) if reverse else (ci <= ri)
    tri = before.astype(BF16)
    triT = ((ri >= ci) if reverse else (ri <= ci)).astype(BF16)
    cum = _split_dot(a, tri, "ba")
    cumT = _split_dot(aT, triT, "ab")
    end = 0 if reverse else q - 1
    total = cum[end:end + 1, :]
    expand = expand_ref[...]
    w_full = jnp.dot((jnp.exp(total - cum) * dt).astype(BF16), expand,
                     preferred_element_type=F32)
    e_full = jnp.dot(jnp.exp(cum).astype(BF16), expand, preferred_element_type=F32)
    cd_full = _split_dot(jnp.broadcast_to(jnp.exp(total), (8, heads)), expand, "ab")[0:1]
    lcumT = cumT - jnp.log(dtT)
    low = lax.broadcasted_iota(jnp.int32, (q, pw), 1) < SSM_HEAD_DIM

    for g in range(SSM_GROUPS):
        gcols = slice(g * gw, (g + 1) * gw)
        ncols = slice(g * SSM_STATE, (g + 1) * SSM_STATE)
        bg = b_ref[:, ncols]
        cg = c_ref[:, ncols]
        cb = lax.dot_general(cg, bg, (((1,), (1,)), ((), ())), preferred_element_type=F32)
        st = state_ref[:, gcols]
        y_off = jnp.dot(cg, st.astype(BF16), preferred_element_type=F32) * e_full[:, gcols]
        ys = []
        for j in range(hpg // 2):
            xp = x_ref[:, g * gw + j * pw:g * gw + (j + 1) * pw].astype(F32)
            xbd = jnp.concatenate([jnp.where(low, xp, 0.0), jnp.where(low, 0.0, xp)], axis=0).astype(BF16)
            mats = []
            for hh in range(2):
                h = g * hpg + 2 * j + hh
                diff = cum[:, h:h + 1] - lcumT[h:h + 1, :]
                mats.append((cb * jnp.exp(jnp.where(before, diff, -jnp.inf))).astype(BF16))
            ys.append(jnp.dot(jnp.concatenate(mats, axis=1), xbd, preferred_element_type=F32)
                      + y_off[:, j * pw:(j + 1) * pw])
        xg = x_ref[:, gcols].astype(F32)
        if finish:
            y = jnp.concatenate(ys, axis=1) + other_ref[:, gcols].astype(F32) + dskip_ref[:, gcols] * xg
            v = y * _silu(z_ref[:, gcols].astype(F32))
            v = v * lax.rsqrt(jnp.mean(v * v, axis=-1, keepdims=True) + EPS) * gain_ref[:, gcols]
            y_ref[:, gcols] = v.astype(y_ref.dtype)
        else:
            for j, y in enumerate(ys):
                y_ref[:, g * gw + j * pw:g * gw + (j + 1) * pw] = y.astype(y_ref.dtype)
        xw = (xg * w_full[:, gcols]).astype(BF16)
        bgT = bg.astype(F32).T.astype(BF16)
        state_ref[:, gcols] = st * cd_full[:, gcols] + jnp.dot(bgT, xw, preferred_element_type=F32)


def _ssd_scan(xbc, dt, dtT, dt_bias, a_log, dm, *, inner, reverse, tail=None):
    rows = _rows(dm)
    m = xbc.shape[0]
    q = SSM_CHUNK
    heads = inner // SSM_HEAD_DIM
    nc = rows // q
    ctx_chunks = dm.ctx // q
    sw = SSM_GROUPS * SSM_STATE
    d = 1 if reverse else 0

    def chunk(b, s):
        if reverse:
            s = jnp.where(s < ctx_chunks, ctx_chunks - 1 - s, nc - 1 - (s - ctx_chunks))
        return b * nc + s

    expand = jnp.repeat(jnp.eye(heads, dtype=BF16), SSM_HEAD_DIM, axis=1)
    row_spec = pl.BlockSpec((q, inner), lambda b, s: (chunk(b, s), 0))
    vec_spec = pl.BlockSpec((1, inner), lambda b, s: (0, 0))
    args = [xbc, xbc, xbc, dt, dtT, dt_bias[d][None, :], dt_bias[d][:, None], a_log[d][None, :],
            a_log[d][:, None], expand]
    in_specs = [row_spec,
                pl.BlockSpec((q, sw), lambda b, s: (chunk(b, s), inner // sw)),
                pl.BlockSpec((q, sw), lambda b, s: (chunk(b, s), inner // sw + 1)),
                pl.BlockSpec((q, 2 * heads), lambda b, s: (chunk(b, s), 0)),
                pl.BlockSpec((heads, q), lambda b, s: (d, chunk(b, s))),
                pl.BlockSpec((1, heads), lambda b, s: (0, 0)),
                pl.BlockSpec((heads, 1), lambda b, s: (0, 0)),
                pl.BlockSpec((1, heads), lambda b, s: (0, 0)),
                pl.BlockSpec((heads, 1), lambda b, s: (0, 0)),
                pl.BlockSpec((heads, inner), lambda b, s: (0, 0))]
    if tail is not None:
        y_other, proj, d_skip, gain = tail
        args += [y_other, proj, jnp.repeat(d_skip.astype(F32), SSM_HEAD_DIM).reshape(1, inner),
                 gain.reshape(1, inner)]
        in_specs += [row_spec, row_spec, vec_spec, vec_spec]
    kern = functools.partial(_ssd_kernel, heads=heads, reverse=reverse, finish=tail is not None)
    return pl.pallas_call(
        kern,
        grid=(dm.batch, nc),
        in_specs=in_specs,
        out_specs=row_spec,
        out_shape=jax.ShapeDtypeStruct((m, inner), BF16),
        scratch_shapes=[pltpu.VMEM((SSM_STATE, inner), F32)],
        compiler_params=_params(("parallel", "arbitrary")),
        name="ssd_scan_bwd" if reverse else "ssd_scan_fwd",
    )(*args)


def _final_norm_kernel(x_ref, g_ref, o_ref):
    x = x_ref[...]
    o_ref[...] = x * lax.rsqrt(jnp.mean(x * x, axis=-1, keepdims=True) + EPS) * g_ref[...]


def _final_norm(xc, gain, dm, *, tr):
    rows = _rows(dm)
    tpb = rows // tr
    lat_tiles = dm.seq // tr
    ctx_tiles = dm.ctx // tr
    return pl.pallas_call(
        _final_norm_kernel,
        grid=(dm.batch, lat_tiles),
        in_specs=[pl.BlockSpec((tr, dm.d), lambda b, i: (b * tpb + ctx_tiles + i, 0)),
                  pl.BlockSpec((1, dm.d), lambda b, i: (0, 0))],
        out_specs=pl.BlockSpec((tr, dm.d), lambda b, i: (b * lat_tiles + i, 0)),
        out_shape=jax.ShapeDtypeStruct((dm.batch * dm.seq, dm.d), F32),
        compiler_params=_params(("parallel", "parallel")),
        name="final_norm",
    )(xc, gain.reshape(1, dm.d))


def _rope_tables(dm):
    d_axis = HEAD_DIM // 2
    t = jnp.arange(dm.seq, dtype=jnp.int32)
    inv_freq = jnp.power(ROPE_THETA, -jnp.arange(0, d_axis, 2, dtype=F32) / d_axis)
    ang_r = (t // GRID_W).astype(F32)[:, None] * inv_freq[None, :]
    ang_c = (t % GRID_W).astype(F32)[:, None] * inv_freq[None, :]
    cos = jnp.concatenate([jnp.cos(ang_r), jnp.cos(ang_c), jnp.cos(ang_r), jnp.cos(ang_c)], axis=1)
    sin = jnp.concatenate([-jnp.sin(ang_r), -jnp.sin(ang_c), jnp.sin(ang_r), jnp.sin(ang_c)], axis=1)
    cos = jnp.concatenate([jnp.ones((dm.ctx, HEAD_DIM), F32), cos], axis=0)
    sin = jnp.concatenate([jnp.zeros((dm.ctx, HEAD_DIM), F32), sin], axis=0)
    return cos, sin


def _relayout_kernel(w_ref, o_ref):
    tn = w_ref.shape[1]
    quarter = HEAD_DIM // 4
    r = lax.broadcasted_iota(jnp.int32, (tn, tn), 0)
    c = lax.broadcasted_iota(jnp.int32, (tn, tn), 1)
    blk = (c // quarter) % 4
    src_blk = jnp.where(blk == 1, 2, jnp.where(blk == 2, 1, blk))
    src = (c // HEAD_DIM) * HEAD_DIM + src_blk * quarter + c % quarter
    perm = (r == src).astype(BF16)
    o_ref[...] = jnp.dot(w_ref[...].astype(BF16), perm, preferred_element_type=F32).astype(BF16)


def _qk_weights(w_in, qk_cols, tn):
    d = w_in.shape[0]
    return pl.pallas_call(
        _relayout_kernel,
        grid=(qk_cols // tn,),
        in_specs=[pl.BlockSpec((d, tn), lambda j: (0, j))],
        out_specs=pl.BlockSpec((d, tn), lambda j: (0, j)),
        out_shape=jax.ShapeDtypeStruct((d, qk_cols), BF16),
        compiler_params=_params(("parallel",)),
        name="qk_relayout",
    )(w_in)


def _lambda_init(layer_idx):
    return 0.8 - 0.6 * math.exp(-0.3 * layer_idx)


def _tiles(dm):
    rows = _rows(dm)
    tm = rows // 2 if (rows // 2) % 128 == 0 else rows
    return dict(tm=tm, tn=min(512, dm.d), tq=dm.ctx, tk=dm.ctx)


def _forward(dm, x, c, ctx, c_ctx, layers, final_norm):
    d = dm.d
    til = _tiles(dm)
    tm, tn, tq, tk = til["tm"], til["tn"], til["tq"], til["tk"]
    scale = HEAD_DIM ** -0.5 * math.log2(math.e)
    xc =jnp.concatenate([ctx, x], axis=1).reshape(dm.batch * _rows(dm), d)
    cvec = jnp.concatenate([c, c_ctx[None, :], jnp.zeros((MOD_ROWS - dm.batch - 1, d), F32)], axis=0)
    rope = _rope_tables(dm)
    q_tiles = d // tn

    for i, (kind, (w_mod, b_mod, norm), p) in enumerate(layers):
        mods = _modulation(cvec, w_mod, b_mod)
        if kind == "gqa":
            w_in, q_gain, k_gain, w_out = p
            kv_tiles = GQA_KV_HEADS * HEAD_DIM // tn
            n_tiles = w_in.shape[1] // tn
            segs = ((0, q_tiles, 0, True, scale),
                    (q_tiles, q_tiles + kv_tiles, 1, True, None),
                    (q_tiles + kv_tiles, n_tiles, None, False, None))
            w_qk = _qk_weights(w_in, (q_tiles + kv_tiles) * tn, tn)
            gains = jnp.stack([q_gain, k_gain]).astype(F32)[:, _rope_layout()]
            proj = _in_proj(xc, norm, mods, w_in, dm, tm=tm, tn=tn, segs=segs, rope=rope, w_qk=w_qk,
                            gains=gains)
            g = _gqa_attention(proj, dm, tq=tq, tk=tk, ctx_out=i < len(layers) - 1)
        elif kind == "diff":
            w_in, lq1, lk1, lq2, lk2, subln, w_out = p
            n_tiles = w_in.shape[1] // tn
            segs = ((0, q_tiles, None, True, scale),
                    (q_tiles, 2 * q_tiles, None, True, None),
                    (2 * q_tiles, n_tiles, None, False, None))
            proj = _in_proj(xc, norm, mods, w_in, dm, tm=tm, tn=tn, segs=segs, rope=rope,
                            w_qk=_qk_weights(w_in, 2 * q_tiles * tn, tn))
            lam_vecs = jnp.stack([lq1, lk1, lq2, lk2]).astype(F32)
            g = _diff_attention(proj, lam_vecs, subln, dm, tq=2 * tq, tk=tk, lambda_init=_lambda_init(i))
        else:
            w_in, conv_w, conv_b, dt_bias, a_log, d_skip, ssm_norm, w_out = p
            inner = w_out.shape[0]
            n_main = 2 * inner + 2 * SSM_GROUPS * SSM_STATE
            segs = ((0, n_main // tn, None, False, None),)
            proj, dt, dtT = _in_proj(xc, norm, mods, w_in, dm, tm=tm, tn=tn,
                                     segs=segs, w_dt=w_in[:, n_main:].astype(BF16))
            xbc = _ssd_conv(proj, conv_w, conv_b, dm, col0=inner, tr=tq,
                            tn=math.gcd(inner, conv_w.shape[1]))
            y_f = _ssd_scan(xbc, dt, dtT, dt_bias, a_log, dm, inner=inner, reverse=False)
            g = _ssd_scan(xbc, dt, dtT, dt_bias, a_log, dm, inner=inner, reverse=True,
                          tail=(y_f, proj, d_skip, ssm_norm))
        wide = w_out.shape[0] > d
        xc = _out_proj(g, w_out, xc, mods, dm, tm=tm if wide else _rows(dm), tn=tn)

    out = _final_norm(xc, final_norm, dm, tr=tq)
    return out.reshape(dm.batch, dm.seq, d)


def kernel(x, c, ctx, c_ctx, l0_w_mod, l0_b_mod, l0_norm, l0_w_in, l0_q_gain, l0_k_gain, l0_w_out, l1_w_mod, l1_b_mod, l1_norm, l1_w_in, l1_lambda_q1, l1_lambda_k1, l1_lambda_q2, l1_lambda_k2, l1_subln, l1_w_out, l2_w_mod, l2_b_mod, l2_norm, l2_w_in, l2_conv_w, l2_conv_b, l2_dt_bias, l2_A_log, l2_D, l2_ssm_norm, l2_w_out, l3_w_mod, l3_b_mod, l3_norm, l3_w_in, l3_q_gain, l3_k_gain, l3_w_out, final_norm):
    dm = Dims(batch=x.shape[0], seq=x.shape[1], ctx=ctx.shape[1], d=x.shape[2])
    layers = (
        ("gqa", (l0_w_mod, l0_b_mod, l0_norm), (l0_w_in, l0_q_gain, l0_k_gain, l0_w_out)),
        ("diff", (l1_w_mod, l1_b_mod, l1_norm), (l1_w_in, l1_lambda_q1, l1_lambda_k1, l1_lambda_q2,
                                                 l1_lambda_k2, l1_subln, l1_w_out)),
        ("ssd", (l2_w_mod, l2_b_mod, l2_norm), (l2_w_in, l2_conv_w, l2_conv_b, l2_dt_bias, l2_A_log,
                                                l2_D, l2_ssm_norm, l2_w_out)),
        ("gqa", (l3_w_mod, l3_b_mod, l3_norm), (l3_w_in, l3_q_gain, l3_k_gain, l3_w_out)),
    )
    return _forward(dm, x, c, ctx, c_ctx, layers, final_norm)
```

```python
import collections
import functools
import math

import jax
import jax.numpy as jnp
from jax import lax
from jax.experimental import pallas as pl
from jax.experimental.pallas import tpu as pltpu

F32 = jnp.float32
BF16 = jnp.bfloat16

EPS = 1e-6
ROPE_THETA = 10000.0
HEAD_DIM = 128
GQA_KV_HEADS = 4
SSM_HEAD_DIM = 64
SSM_GROUPS = 8
SSM_STATE = 128
SSM_CONV = 4
SSM_CHUNK = 128
GRID_W = 64
DEPTH = 4
N_MODS = 3
MOD_ROWS = 8
MOD_TILE = 1024
ROW_PARTS = 8
NORM_SLAB = 16
BF16_SUBLANES = 16
VMEM_LIMIT = 56 * 1024 * 1024

Dims = collections.namedtuple("Dims", "batch seq ctx d")


def _rows(dm):
    return dm.ctx + dm.seq


def _silu(v):
    return v * (1.0 / (1.0 + jnp.exp2(v * -math.log2(math.e))))


def _params(semantics):
    return pltpu.CompilerParams(dimension_semantics=semantics, vmem_limit_bytes=VMEM_LIMIT)


def _mod_kernel(c_ref, w_ref, b_ref, o_ref):
    s = _silu(c_ref[...]).astype(BF16)
    o_ref[...] = jnp.dot(s, w_ref[...].astype(BF16), preferred_element_type=F32) + b_ref[...]


def _modulation(cvec, w_mod, b_mod):
    d, n = w_mod.shape
    tn = math.gcd(n, MOD_TILE)
    return pl.pallas_call(
        _mod_kernel,
        grid=(n // tn,),
        in_specs=[pl.BlockSpec((MOD_ROWS, d), lambda j: (0, 0)),
                  pl.BlockSpec((d, tn), lambda j: (0, j)),
                  pl.BlockSpec((1, tn), lambda j: (0, j))],
        out_specs=pl.BlockSpec((MOD_ROWS, tn), lambda j: (0, j)),
        out_shape=jax.ShapeDtypeStruct((MOD_ROWS, n), F32),
        compiler_params=_params(("arbitrary",)),
        name="adaln_mod",
    )(cvec, w_mod, b_mod.reshape(1, n))


def _rope_layout():
    quarter = HEAD_DIM // 4
    idx = jnp.arange(HEAD_DIM).reshape(4, quarter)
    return idx[jnp.array([0, 2, 1, 3])].reshape(HEAD_DIM)


def _in_proj_kernel(*refs, tm, tn, tiles_per_batch, ctx_len, segs, has_rope, has_gain, has_dt):
    it = iter(refs)
    x_ref, g_ref, shl_ref, scl_ref, shc_ref, scc_ref, w_ref = (next(it) for _ in range(7))
    cos_ref = sin_ref = wqk_ref = gain_ref = wdt_ref = wdtT_ref = dt_ref = dtT_ref = None
    if has_rope:
        cos_ref, sin_ref, wqk_ref = next(it), next(it), next(it)
    if has_gain:
        gain_ref = next(it)
    if has_dt:
        wdt_ref, wdtT_ref = next(it), next(it)
    o_ref = next(it)
    if has_dt:
        dt_ref, dtT_ref = next(it), next(it)
    h_ref, mul_ref, add_ref = next(it), next(it), next(it)

    i = pl.program_id(0)
    j = pl.program_id(1)

    part = tm // ROW_PARTS
    starts = range(0, tm, part)

    def modulation_tables():
        d = g_ref.shape[1]
        g = g_ref[...]
        for k, (sc_ref, sh_ref) in enumerate(((scl_ref, shl_ref), (scc_ref, shc_ref))):
            mul_ref[k] = jnp.broadcast_to(g * (1.0 + sc_ref[0]), (NORM_SLAB, d))
            add_ref[k] = jnp.broadcast_to(sh_ref[0], (NORM_SLAB, d))

    def normalise(r0):
        for s0 in range(r0, r0 + part, NORM_SLAB):
            rows = slice(s0, s0 + NORM_SLAB)
            x = x_ref[rows, :]
            y = x * lax.rsqrt(jnp.mean(x * x, axis=-1, keepdims=True) + EPS)
            k = ((i % tiles_per_batch) * tm + s0 < ctx_len).astype(jnp.int32)
            h_ref[rows, :] = (y * mul_ref[k] + add_ref[k]).astype(BF16)

    def tile(gain_idx, rope, scale, first):
        w = wqk_ref[...] if rope else w_ref[...].astype(BF16)
        plain = gain_idx is None and not rope and scale is None
        if plain and not first:
            o_ref[...] = jnp.dot(h_ref[...], w, preferred_element_type=F32).astype(o_ref.dtype)
            return
        if first:
            modulation_tables()
        accs = []
        for r0 in starts:
            if first:
                normalise(r0)
            accs.append(jnp.dot(h_ref[r0:r0 + part, :], w, preferred_element_type=F32))
        for r0, acc in zip(starts, accs):
            rows = slice(r0, r0 + part)
            if plain:
                o_ref[rows, :] = acc.astype(o_ref.dtype)
                continue
            for hh in range(tn // HEAD_DIM):
                cols = slice(hh * HEAD_DIM, (hh + 1) * HEAD_DIM)
                v = acc[:, cols]
                if gain_idx is not None:
                    v = v * lax.rsqrt(jnp.mean(v * v, axis=-1, keepdims=True) + EPS)
                    v = v * gain_ref[gain_idx:gain_idx + 1, :]
                if rope:
                    v = v * cos_ref[rows, :] + pltpu.roll(v, HEAD_DIM // 2, 1) * sin_ref[rows, :]
                if scale is not None:
                    v = v * scale
                o_ref[rows, cols] = v.astype(o_ref.dtype)
        if first and has_dt:
            h = h_ref[...]
            dt_ref[...] = jnp.dot(h, wdt_ref[...], preferred_element_type=F32)
            dtT_ref[...] = lax.dot_general(wdtT_ref[...], h, (((1,), (1,)), ((), ())),
                                           preferred_element_type=F32)

    for (lo, hi, gain_idx, rope, scale) in segs:
        if lo == 0:
            pl.when(j == 0)(functools.partial(tile, gain_idx, rope, scale, True))
            lo = 1
        pl.when((j >= lo) & (j < hi))(functools.partial(tile, gain_idx, rope, scale, False))


def _in_proj(xc, norm_g, mods, w, dm, *, tm, tn, segs, rope=None, w_qk=None, gains=None, w_dt=None):
    m, d = xc.shape
    n = segs[-1][1] * tn
    tpb = _rows(dm) // tm
    mods3 = mods.reshape(MOD_ROWS, 1, N_MODS * d)
    has_rope, has_gain, has_dt = rope is not None, gains is not None, w_dt is not None
    qk_tiles = w_qk.shape[1] // tn if has_rope else 0

    def lat(col):
        return pl.BlockSpec((1, 1, d), lambda i, j: (i // tpb, 0, col))

    def ctx(col):
        return pl.BlockSpec((1, 1, d), lambda i, j: (dm.batch, 0, col))

    args = [xc, norm_g.reshape(1, d), mods3, mods3, mods3, mods3, w]
    in_specs = [pl.BlockSpec((tm, d), lambda i, j: (i, 0)),
                pl.BlockSpec((1, d), lambda i, j: (0, 0)),
                lat(0), lat(1), ctx(0), ctx(1),
                pl.BlockSpec((d, tn), lambda i, j: (0, jnp.maximum(j, qk_tiles)))]
    if has_rope:
        args += list(rope) + [w_qk]
        in_specs += [pl.BlockSpec((tm, HEAD_DIM), lambda i, j: (i % tpb, 0))] * 2
        in_specs.append(pl.BlockSpec((d, tn), lambda i, j: (0, jnp.minimum(j, qk_tiles - 1))))
    if has_gain:
        args.append(gains)
        in_specs.append(pl.BlockSpec(gains.shape, lambda i, j: (0, 0)))
    out_shape = [jax.ShapeDtypeStruct((m, n), BF16)]
    out_specs = [pl.BlockSpec((tm, tn), lambda i, j: (i, j))]
    if has_dt:
        n_dt = w_dt.shape[1]
        args += [w_dt, w_dt.T]
        in_specs += [pl.BlockSpec((d, n_dt), lambda i, j: (0, 0)),
                     pl.BlockSpec((n_dt, d), lambda i, j: (0, 0))]
        out_shape += [jax.ShapeDtypeStruct((m, n_dt), F32), jax.ShapeDtypeStruct((n_dt, m), F32)]
        out_specs += [pl.BlockSpec((tm, n_dt), lambda i, j: (i, 0)),
                      pl.BlockSpec((n_dt, tm), lambda i, j: (0, i))]
    kern = functools.partial(_in_proj_kernel, tm=tm, tn=tn, tiles_per_batch=tpb, ctx_len=dm.ctx,
                             segs=segs, has_rope=has_rope, has_gain=has_gain, has_dt=has_dt)
    out = pl.pallas_call(
        kern,
        grid=(m // tm, n // tn),
        in_specs=in_specs,
        out_specs=out_specs,
        out_shape=out_shape,
        scratch_shapes=[pltpu.VMEM((tm, d), BF16),
                        pltpu.VMEM((2, NORM_SLAB, d), F32),
                        pltpu.VMEM((2, NORM_SLAB, d), F32)],
        compiler_params=_params(("parallel", "arbitrary")),
        name="in_proj",
    )(*args)
    return out if has_dt else out[0]


def _out_proj_kernel(g_ref, w_ref, x_ref, gl_ref, gc_ref, o_ref, *, tm, tiles_per_batch, ctx_len):
    i = pl.program_id(0)
    acc = jnp.dot(g_ref[...], w_ref[...].astype(BF16), preferred_element_type=F32)
    row = (i % tiles_per_batch) * tm + lax.broadcasted_iota(jnp.int32, (tm, 1), 0)
    gate = jnp.where(row < ctx_len, gc_ref[0], gl_ref[0])
    o_ref[...] = x_ref[...] + gate * acc


def _out_proj(g, w, xc, mods, dm, *, tm, tn):
    m, k = g.shape
    d = w.shape[1]
    tpb = _rows(dm) // tm
    tiles_n = d // tn
    mods3 = mods.reshape(MOD_ROWS, 1, N_MODS * d)
    gate_col = 2 * tiles_n
    kern = functools.partial(_out_proj_kernel, tm=tm, tiles_per_batch=tpb, ctx_len=dm.ctx)
    return pl.pallas_call(
        kern,
        grid=(m // tm, tiles_n),
        in_specs=[pl.BlockSpec((tm, k), lambda i, j: (i, 0)),
                  pl.BlockSpec((k, tn), lambda i, j: (0, j)),
                  pl.BlockSpec((tm, tn), lambda i, j: (i, j)),
                  pl.BlockSpec((1, 1, tn), lambda i, j: (i // tpb, 0, gate_col + j)),
                  pl.BlockSpec((1, 1, tn), lambda i, j: (dm.batch, 0, gate_col + j))],
        out_specs=pl.BlockSpec((tm, tn), lambda i, j: (i, j)),
        out_shape=jax.ShapeDtypeStruct((m, d), F32),
        compiler_params=_params(("parallel", "arbitrary")),
        name="out_proj",
    )(g, w, xc, mods3, mods3)


def _lane_tile_reduce(v, op):
    part = v[:, :HEAD_DIM]
    for t in range(1, v.shape[1] // HEAD_DIM):
        part = op(part, v[:, t * HEAD_DIM:(t + 1) * HEAD_DIM])
    return part


def _scores(pairs, k_ref, krows):
    blocks = []
    for q, kcols in pairs:
        half = q.shape[0] // 2
        for qq in (q[:half], q[half:]):
            blocks.append(lax.dot_general(qq, k_ref[krows, kcols], (((1,), (1,)), ((), ())),
                                          preferred_element_type=F32))
    return jnp.concatenate(blocks, axis=0)


def _weighted(p, v_ref, krows, with_sums=False):
    half = p.shape[0] // 2
    pb = p.astype(BF16)
    v = v_ref[krows, :]
    if with_sums:
        v = jnp.concatenate([v, jnp.ones_like(v)], axis=1)
    return jnp.concatenate([jnp.dot(pp, v, preferred_element_type=F32)
                            for pp in (pb[:half], pb[half:])], axis=0)


def _attend_in_place(pairs, k_ref, v_ref, key_rows, tk):
    chunks = [slice(c * tk, (c + 1) * tk) for c in range(key_rows // tk)]
    s = [_scores(pairs, k_ref, krows) for krows in chunks]
    m_part = functools.reduce(jnp.maximum, [_lane_tile_reduce(sc, jnp.maximum) for sc in s])
    m = jnp.max(m_part, axis=1, keepdims=True)
    p = [jnp.exp2(sc - m) for sc in s]
    l_part = functools.reduce(jnp.add, [_lane_tile_reduce(pc, jnp.add) for pc in p])
    acc = functools.reduce(jnp.add, [_weighted(pc, v_ref, krows) for pc, krows in zip(p, chunks)])
    return acc * (1.0 / jnp.sum(l_part, axis=1, keepdims=True))


def _attend_pipelined(pairs_next, s_next_ref, s_cur_ref, finish_cur, k_ref, v_ref, m_ref, key_rows, tk):
    n_chunks = key_rows // tk
    have_cur = s_cur_ref is not None
    dv = v_ref.shape[1]
    mxu_sums = dv == HEAD_DIM
    chunks = [slice(c * tk, (c + 1) * tk) for c in range(n_chunks)]
    acc = None
    for c, krows in enumerate(chunks if have_cur else ()):
        p = jnp.concatenate(
            [jnp.exp2(s_cur_ref[c, :, t * HEAD_DIM:(t + 1) * HEAD_DIM] - m_ref[0])
             for t in range(tk // HEAD_DIM)], axis=1)
        if not mxu_sums:
            part = _lane_tile_reduce(p, jnp.add)
            m_ref[1] = part if c == 0 else m_ref[1] + part
        pv = _weighted(p, v_ref, krows, with_sums=mxu_sums)
        acc = pv if acc is None else acc + pv
    if have_cur:
        if mxu_sums:
            finish_cur(acc[:, :dv] * (1.0 / acc[:, dv:]))
        else:
            finish_cur(acc * (1.0 / jnp.sum(m_ref[1], axis=1, keepdims=True)))
    m_part = None
    for c, krows in enumerate(chunks if pairs_next is not None else ()):
        s = _scores(pairs_next, k_ref, krows)
        s_next_ref[c] = s
        part = _lane_tile_reduce(s, jnp.maximum)
        m_part = part if m_part is None else jnp.maximum(m_part, part)
    if pairs_next is not None:
        m_ref[0] = jnp.broadcast_to(jnp.max(m_part, axis=1, keepdims=True), m_ref.shape[1:])


def _attention_steps(step, queries, finish, k_ref, v_ref, o_ref, s_refs, m_ref, *, tq, tk, ctx_len, rows,
                     ctx_out):
    n_lat = (rows - ctx_len) // tq

    def lat_row0(t):
        return pl.multiple_of(ctx_len + (t - 1) * tq, math.gcd(ctx_len, tq))

    def pipe(t_next, t_cur, cur_parity):
        pairs = None if t_next is None else queries(lat_row0(t_next), tq)
        _attend_pipelined(pairs, None if t_next is None else s_refs[1 - cur_parity],
                          None if t_cur is None else s_refs[cur_parity],
                          None if t_cur is None else functools.partial(finish, lat_row0(t_cur), tq),
                          k_ref, v_ref, m_ref, rows, tk)

    @pl.when(step == 0)
    def _():
        if ctx_out:
            finish(0, ctx_len, _attend_in_place(queries(0, ctx_len), k_ref, v_ref, ctx_len, tk))
        else:
            o_ref[:ctx_len, :] = jnp.zeros((ctx_len, o_ref.shape[1]), o_ref.dtype)
        pipe(1, None, 0)

    if n_lat > 1:
        for parity in range(2):
            pl.when((step >= 1) & (step < n_lat) & (step % 2 == parity))(
                functools.partial(pipe, step + 1, step, parity))
    pl.when(step == n_lat)(functools.partial(pipe, None, step, n_lat % 2))


def _gqa_kernel(q_ref, k_ref, v_ref, z_ref, o_ref, s0_ref, s1_ref, m_ref, *, tq, tk, group, ctx_len, rows,
                ctx_out):
    def queries(row0, nq):
        q = jnp.concatenate([q_ref[pl.ds(row0, nq), g * HEAD_DIM:(g + 1) * HEAD_DIM]
                             for g in range(group)], axis=0)
        return [(q, slice(0, HEAD_DIM))]

    def finish(row0, nq, o):
        qrows = pl.ds(row0, nq)
        for g in range(group):
            cols = slice(g * HEAD_DIM, (g + 1) * HEAD_DIM)
            z = z_ref[qrows, cols].astype(F32)
            o_ref[qrows, cols] = (o[g * nq:(g + 1) * nq] * _silu(z)).astype(o_ref.dtype)

    _attention_steps(pl.program_id(2), queries, finish, k_ref, v_ref, o_ref, (s0_ref, s1_ref), m_ref,
                     tq=tq, tk=tk, ctx_len=ctx_len, rows=rows, ctx_out=ctx_out)


def _gqa_attention(proj, dm, *, tq, tk, ctx_out):
    rows = _rows(dm)
    m = proj.shape[0]
    width = dm.d
    heads = width // HEAD_DIM
    group = heads // GQA_KV_HEADS
    gw = group * HEAD_DIM
    k_col0 = width // HEAD_DIM
    v_col0 = k_col0 + GQA_KV_HEADS
    z_col0 = (width + 2 * GQA_KV_HEADS * HEAD_DIM) // gw
    kern = functools.partial(_gqa_kernel, tq=tq, tk=tk, group=group, ctx_len=dm.ctx, rows=rows,
                             ctx_out=ctx_out)
    return pl.pallas_call(
        kern,
        grid=(dm.batch, GQA_KV_HEADS, 1 + dm.seq // tq),
        in_specs=[pl.BlockSpec((rows, gw), lambda b, h, i: (b, h)),
                  pl.BlockSpec((rows, HEAD_DIM), lambda b, h, i: (b, k_col0 + h)),
                  pl.BlockSpec((rows, HEAD_DIM), lambda b, h, i: (b, v_col0 + h)),
                  pl.BlockSpec((rows, gw), lambda b, h, i: (b, z_col0 + h))],
        out_specs=pl.BlockSpec((rows, gw), lambda b, h, i: (b, h)),
        out_shape=jax.ShapeDtypeStruct((m, width), BF16),
        scratch_shapes=[pltpu.VMEM((rows // tk, group * tq, tk), F32),
                        pltpu.VMEM((rows // tk, group * tq, tk), F32),
                        pltpu.VMEM((2, group * tq, HEAD_DIM), F32)],
        compiler_params=_params(("parallel", "parallel", "arbitrary")),
        name="gqa_attention",
    )(proj, proj, proj, proj)


def _diff_kernel(lam_ref, q_ref, k_ref, v_ref, z_ref, subln_ref, o_ref, s0_ref, s1_ref, m_ref,
                 *, tq, tk, ctx_len, rows, lambda_init):
    lv = lam_ref[...]
    lam = (jnp.exp(jnp.sum(lv[0:1] * lv[1:2], axis=1, keepdims=True))
           - jnp.exp(jnp.sum(lv[2:3] * lv[3:4], axis=1, keepdims=True)) + lambda_init)

    def queries(row0, nq):
        halves = [slice(t * HEAD_DIM, (t + 1) * HEAD_DIM) for t in range(2)]
        return [(q_ref[pl.ds(row0, nq), cols], cols) for cols in halves]

    def finish(row0, nq, o):
        qrows = pl.ds(row0, nq)
        o = o[:nq] - lam * o[nq:]
        o = o * lax.rsqrt(jnp.mean(o * o, axis=-1, keepdims=True) + EPS) * subln_ref[...]
        o = o * (1.0 - lambda_init)
        o_ref[qrows, :] = (o * _silu(z_ref[qrows, :].astype(F32))).astype(o_ref.dtype)

    _attention_steps(pl.program_id(2), queries, finish, k_ref, v_ref, o_ref, (s0_ref, s1_ref), m_ref,
                     tq=tq, tk=tk, ctx_len=ctx_len, rows=rows, ctx_out=True)


def _diff_attention(proj, lam_vecs, subln, dm, *, tq, tk, lambda_init):
    rows = _rows(dm)
    m = proj.shape[0]
    width = dm.d
    hw = 2 * HEAD_DIM
    heads = width // hw
    kern = functools.partial(_diff_kernel, tq=tq, tk=tk, ctx_len=dm.ctx, rows=rows, lambda_init=lambda_init)

    def head_block(first):
        return pl.BlockSpec((rows, hw), lambda b, h, i: (b, first + h))

    return pl.pallas_call(
        kern,
        grid=(dm.batch, heads, 1 + dm.seq // tq),
        in_specs=[pl.BlockSpec(lam_vecs.shape, lambda b, h, i: (0, 0)),
                  head_block(0), head_block(heads), head_block(2 * heads), head_block(3 * heads),
                  pl.BlockSpec((1, hw), lambda b, h, i: (0, 0))],
        out_specs=head_block(0),
        out_shape=jax.ShapeDtypeStruct((m, width), BF16),
        scratch_shapes=[pltpu.VMEM((rows // tk, 2 * tq, tk), F32),
                        pltpu.VMEM((rows // tk, 2 * tq, tk), F32),
                        pltpu.VMEM((2, 2 * tq, HEAD_DIM), F32)],
        compiler_params=_params(("parallel", "parallel", "arbitrary")),
        name="diff_attention",
    )(lam_vecs, proj, proj, proj, proj, subln.reshape(1, hw))


CONV_TAPS = (-1, 0, 1, 2)
CONV_STRIP = 256


def _conv_kernel(cur_ref, prev_ref, next_ref, shift_ref, w_ref, b_ref, o_ref, *, tr, tiles_per_batch,
                 ctx_tiles):
    r = pl.program_id(1)
    edge = BF16_SUBLANES
    centre = CONV_TAPS.index(0)
    has_prev = (r != 0) & (r != ctx_tiles)
    has_next = (r != ctx_tiles - 1) & (r != tiles_per_batch - 1)
    row = lax.broadcasted_iota(jnp.int32, (edge, 1), 0)
    for c0 in range(0, cur_ref.shape[1], CONV_STRIP):
        cols = slice(c0, c0 + CONV_STRIP)
        xb = cur_ref[:, cols]
        w = [w_ref[t:t + 1, cols] for t in range(SSM_CONV)]
        y = b_ref[:, cols] + w[centre] * xb.astype(F32)
        for k, tap in enumerate(t for t in range(SSM_CONV) if t != centre):
            y = y + w[tap] * jnp.dot(shift_ref[k], xb, preferred_element_type=F32)
        o_ref[edge:tr - edge, cols] = _silu(y[edge:tr - edge]).astype(o_ref.dtype)
        prev = jnp.where(has_prev, prev_ref[edge - 1:edge, cols].astype(F32), 0.0)
        nxt0 = jnp.where(has_next, next_ref[0:1, cols].astype(F32), 0.0)
        nxt1 = jnp.where(has_next, next_ref[1:2, cols].astype(F32), 0.0)
        top = y[:edge] + jnp.where(row == 0, w[0] * prev, 0.0)
        o_ref[:edge, cols] = _silu(top).astype(o_ref.dtype)
        bottom = (y[tr - edge:] + jnp.where(row == edge - 2, w[3] * nxt0, 0.0)
                  + jnp.where(row == edge - 1, w[2] * nxt0 + w[3] * nxt1, 0.0))
        o_ref[tr - edge:, cols] = _silu(bottom).astype(o_ref.dtype)


def _ssd_conv(proj, conv_w, conv_b, dm, *, col0, tr, tn):
    rows = _rows(dm)
    m = proj.shape[0]
    n = conv_w.shape[1]
    tpb = rows // tr
    sub = tr // BF16_SUBLANES
    last = m // BF16_SUBLANES - 1
    c0 = col0 // tn
    shifts = jnp.stack([jnp.eye(tr, k=off, dtype=BF16) for off in CONV_TAPS if off != 0])
    kern = functools.partial(_conv_kernel, tr=tr, tiles_per_batch=tpb, ctx_tiles=dm.ctx // tr)
    return pl.pallas_call(
        kern,
        grid=(dm.batch, tpb, n // tn),
        in_specs=[pl.BlockSpec((tr, tn), lambda b, r, j: (b * tpb + r, c0 + j)),
                  pl.BlockSpec((BF16_SUBLANES, tn),
                               lambda b, r, j: (jnp.maximum((b * tpb + r) * sub - 1, 0), c0 + j)),
                  pl.BlockSpec((BF16_SUBLANES, tn),
                               lambda b, r, j: (jnp.minimum((b * tpb + r + 1) * sub, last), c0 + j)),
                  pl.BlockSpec(shifts.shape, lambda b, r, j: (0, 0, 0)),
                  pl.BlockSpec((SSM_CONV, tn), lambda b, r, j: (0, j)),
                  pl.BlockSpec((1, tn), lambda b, r, j: (0, j))],
        out_specs=pl.BlockSpec((tr, tn), lambda b, r, j: (b * tpb + r, j)),
        out_shape=jax.ShapeDtypeStruct((m, n), BF16),
        compiler_params=_params(("parallel", "parallel", "parallel")),
        name="ssd_conv",
    )(proj, proj, proj, shifts, conv_w, conv_b.reshape(1, n))


def _split_dot(a, b_hi_exact, dims):
    hi = a.astype(BF16)
    lo = (a - hi.astype(F32)).astype(BF16)
    if dims == "ab":
        return (jnp.dot(hi, b_hi_exact, preferred_element_type=F32)
                + jnp.dot(lo, b_hi_exact, preferred_element_type=F32))
    return (jnp.dot(b_hi_exact, hi, preferred_element_type=F32)
            + jnp.dot(b_hi_exact, lo, preferred_element_type=F32))


def _ssd_kernel(*refs, heads, reverse, finish):
    it = iter(refs)
    (x_ref, b_ref, c_ref, dt_ref, dtT_ref, bias_ref, biasT_ref, alog_ref, alogT_ref,
     expand_ref) = (next(it) for _ in range(10))
    other_ref = z_ref = dskip_ref = gain_ref = None
    if finish:
        other_ref, z_ref, dskip_ref, gain_ref = (next(it) for _ in range(4))
    y_ref, state_ref = next(it), next(it)
    q = SSM_CHUNK
    hpg = heads // SSM_GROUPS
    gw = hpg * SSM_HEAD_DIM
    pw = 2 * SSM_HEAD_DIM
    step = pl.program_id(1)

    @pl.when(step == 0)
    def _():
        state_ref[...] = jnp.zeros(state_ref.shape, F32)

    def softplus(v):
        return jnp.maximum(v, 0.0) + jnp.log(1.0 + jnp.exp(-jnp.abs(v)))

    dcol = heads if reverse else 0
    dt = softplus(dt_ref[:, dcol:dcol + heads] + bias_ref[...])
    dtT = softplus(dtT_ref[...] + biasT_ref[...])
    a = dt * -jnp.exp(alog_ref[...])
    aT = dtT * -jnp.exp(alogT_ref[...])
    ri = lax.broadcasted_iota(jnp.int32, (q, q), 0)
    ci = lax.broadcasted_iota(jnp.int32, (q, q), 1)
    before = (ci >= ri) if reverse else (ci <= ri)
    tri = before.astype(BF16)
    triT = ((ri >= ci) if reverse else (ri <= ci)).astype(BF16)
    cum = _split_dot(a, tri, "ba")
    cumT = _split_dot(aT, triT, "ab")
    end = 0 if reverse else q - 1
    total = cum[end:end + 1, :]
    expand = expand_ref[...]
    w_full = jnp.dot((jnp.exp(total - cum) * dt).astype(BF16), expand,
                     preferred_element_type=F32)
    e_full = jnp.dot(jnp.exp(cum).astype(BF16), expand, preferred_element_type=F32)
    cd_full = _split_dot(jnp.broadcast_to(jnp.exp(total), (8, heads)), expand, "ab")[0:1]
    lcumT = cumT - jnp.log(dtT)
    low = lax.broadcasted_iota(jnp.int32, (q, pw), 1) < SSM_HEAD_DIM

    for g in range(SSM_GROUPS):
        gcols = slice(g * gw, (g + 1) * gw)
        ncols = slice(g * SSM_STATE, (g + 1) * SSM_STATE)
        bg = b_ref[:, ncols]
        cg = c_ref[:, ncols]
        cb = lax.dot_general(cg, bg, (((1,), (1,)), ((), ())), preferred_element_type=F32)
        st = state_ref[:, gcols]
        y_off = jnp.dot(cg, st.astype(BF16), preferred_element_type=F32) * e_full[:, gcols]
        ys = []
        for j in range(hpg // 2):
            xp = x_ref[:, g * gw + j * pw:g * gw + (j + 1) * pw].astype(F32)
            xbd = jnp.concatenate([jnp.where(low, xp, 0.0), jnp.where(low, 0.0, xp)], axis=0).astype(BF16)
            mats = []
            for hh in range(2):
                h = g * hpg + 2 * j + hh
                diff = cum[:, h:h + 1] - lcumT[h:h + 1, :]
                mats.append((cb * jnp.exp(jnp.where(before, diff, -jnp.inf))).astype(BF16))
            ys.append(jnp.dot(jnp.concatenate(mats, axis=1), xbd, preferred_element_type=F32)
                      + y_off[:, j * pw:(j + 1) * pw])
        xg = x_ref[:, gcols].astype(F32)
        if finish:
            y = jnp.concatenate(ys, axis=1) + other_ref[:, gcols].astype(F32) + dskip_ref[:, gcols] * xg
            v = y * _silu(z_ref[:, gcols].astype(F32))
            v = v * lax.rsqrt(jnp.mean(v * v, axis=-1, keepdims=True) + EPS) * gain_ref[:, gcols]
            y_ref[:, gcols] = v.astype(y_ref.dtype)
        else:
            for j, y in enumerate(ys):
                y_ref[:, g * gw + j * pw:g * gw + (j + 1) * pw] = y.astype(y_ref.dtype)
        xw = (xg * w_full[:, gcols]).astype(BF16)
        bgT = bg.astype(F32).T.astype(BF16)
        state_ref[:, gcols] = st * cd_full[:, gcols] + jnp.dot(bgT, xw, preferred_element_type=F32)


def _ssd_scan(xbc, dt, dtT, dt_bias, a_log, dm, *, inner, reverse, tail=None):
    rows = _rows(dm)
    m = xbc.shape[0]
    q = SSM_CHUNK
    heads = inner // SSM_HEAD_DIM
    nc = rows // q
    ctx_chunks = dm.ctx // q
    sw = SSM_GROUPS * SSM_STATE
    d = 1 if reverse else 0

    def chunk(b, s):
        if reverse:
            s = jnp.where(s < ctx_chunks, ctx_chunks - 1 - s, nc - 1 - (s - ctx_chunks))
        return b * nc + s

    expand = jnp.repeat(jnp.eye(heads, dtype=BF16), SSM_HEAD_DIM, axis=1)
    row_spec = pl.BlockSpec((q, inner), lambda b, s: (chunk(b, s), 0))
    vec_spec = pl.BlockSpec((1, inner), lambda b, s: (0, 0))
    args = [xbc, xbc, xbc, dt, dtT, dt_bias[d][None, :], dt_bias[d][:, None], a_log[d][None, :],
            a_log[d][:, None], expand]
    in_specs = [row_spec,
                pl.BlockSpec((q, sw), lambda b, s: (chunk(b, s), inner // sw)),
                pl.BlockSpec((q, sw), lambda b, s: (chunk(b, s), inner // sw + 1)),
                pl.BlockSpec((q, 2 * heads), lambda b, s: (chunk(b, s), 0)),
                pl.BlockSpec((heads, q), lambda b, s: (d, chunk(b, s))),
                pl.BlockSpec((1, heads), lambda b, s: (0, 0)),
                pl.BlockSpec((heads, 1), lambda b, s: (0, 0)),
                pl.BlockSpec((1, heads), lambda b, s: (0, 0)),
                pl.BlockSpec((heads, 1), lambda b, s: (0, 0)),
                pl.BlockSpec((heads, inner), lambda b, s: (0, 0))]
    if tail is not None:
        y_other, proj, d_skip, gain = tail
        args += [y_other, proj, jnp.repeat(d_skip.astype(F32), SSM_HEAD_DIM).reshape(1, inner),
                 gain.reshape(1, inner)]
        in_specs += [row_spec, row_spec, vec_spec, vec_spec]
    kern = functools.partial(_ssd_kernel, heads=heads, reverse=reverse, finish=tail is not None)
    return pl.pallas_call(
        kern,
        grid=(dm.batch, nc),
        in_specs=in_specs,
        out_specs=row_spec,
        out_shape=jax.ShapeDtypeStruct((m, inner), BF16),
        scratch_shapes=[pltpu.VMEM((SSM_STATE, inner), F32)],
        compiler_params=_params(("parallel", "arbitrary")),
        name="ssd_scan_bwd" if reverse else "ssd_scan_fwd",
    )(*args)


def _final_norm_kernel(x_ref, g_ref, o_ref):
    x = x_ref[...]
    o_ref[...] = x * lax.rsqrt(jnp.mean(x * x, axis=-1, keepdims=True) + EPS) * g_ref[...]


def _final_norm(xc, gain, dm, *, tr):
    rows = _rows(dm)
    tpb = rows // tr
    lat_tiles = dm.seq // tr
    ctx_tiles = dm.ctx // tr
    return pl.pallas_call(
        _final_norm_kernel,
        grid=(dm.batch, lat_tiles),
        in_specs=[pl.BlockSpec((tr, dm.d), lambda b, i: (b * tpb + ctx_tiles + i, 0)),
                  pl.BlockSpec((1, dm.d), lambda b, i: (0, 0))],
        out_specs=pl.BlockSpec((tr, dm.d), lambda b, i: (b * lat_tiles + i, 0)),
        out_shape=jax.ShapeDtypeStruct((dm.batch * dm.seq, dm.d), F32),
        compiler_params=_params(("parallel", "parallel")),
        name="final_norm",
    )(xc, gain.reshape(1, dm.d))


def _rope_tables(dm):
    d_axis = HEAD_DIM // 2
    t = jnp.arange(dm.seq, dtype=jnp.int32)
    inv_freq = jnp.power(ROPE_THETA, -jnp.arange(0, d_axis, 2, dtype=F32) / d_axis)
    ang_r = (t // GRID_W).astype(F32)[:, None] * inv_freq[None, :]
    ang_c = (t % GRID_W).astype(F32)[:, None] * inv_freq[None, :]
    cos = jnp.concatenate([jnp.cos(ang_r), jnp.cos(ang_c), jnp.cos(ang_r), jnp.cos(ang_c)], axis=1)
    sin = jnp.concatenate([-jnp.sin(ang_r), -jnp.sin(ang_c), jnp.sin(ang_r), jnp.sin(ang_c)], axis=1)
    cos = jnp.concatenate([jnp.ones((dm.ctx, HEAD_DIM), F32), cos], axis=0)
    sin = jnp.concatenate([jnp.zeros((dm.ctx, HEAD_DIM), F32), sin], axis=0)
    return cos, sin


def _relayout_kernel(w_ref, o_ref):
    tn = w_ref.shape[1]
    quarter = HEAD_DIM // 4
    r = lax.broadcasted_iota(jnp.int32, (tn, tn), 0)
    c = lax.broadcasted_iota(jnp.int32, (tn, tn), 1)
    blk = (c // quarter) % 4
    src_blk = jnp.where(blk == 1, 2, jnp.where(blk == 2, 1, blk))
    src = (c // HEAD_DIM) * HEAD_DIM + src_blk * quarter + c % quarter
    perm = (r == src).astype(BF16)
    o_ref[...] = jnp.dot(w_ref[...].astype(BF16), perm, preferred_element_type=F32).astype(BF16)


def _qk_weights(w_in, qk_cols, tn):
    d = w_in.shape[0]
    return pl.pallas_call(
        _relayout_kernel,
        grid=(qk_cols // tn,),
        in_specs=[pl.BlockSpec((d, tn), lambda j: (0, j))],
        out_specs=pl.BlockSpec((d, tn), lambda j: (0, j)),
        out_shape=jax.ShapeDtypeStruct((d, qk_cols), BF16),
        compiler_params=_params(("parallel",)),
        name="qk_relayout",
    )(w_in)


def _lambda_init(layer_idx):
    return 0.8 - 0.6 * math.exp(-0.3 * layer_idx)


def _tiles(dm):
    rows = _rows(dm)
    tm = rows // 2 if (rows // 2) % 128 == 0 else rows
    return dict(tm=tm, tn=min(512, dm.d), tq=dm.ctx, tk=dm.ctx)


def _forward(dm, x, c, ctx, c_ctx, layers, final_norm):
    d = dm.d
    til = _tiles(dm)
    tm, tn, tq, tk = til["tm"], til["tn"], til["tq"], til["tk"]
    scale = HEAD_DIM ** -0.5 * math.log2(math.e)
    xc =jnp.concatenate([ctx, x], axis=1).reshape(dm.batch * _rows(dm), d)
    cvec = jnp.concatenate([c, c_ctx[None, :], jnp.zeros((MOD_ROWS - dm.batch - 1, d), F32)], axis=0)
    rope = _rope_tables(dm)
    q_tiles = d // tn

    for i, (kind, (w_mod, b_mod, norm), p) in enumerate(layers):
        mods = _modulation(cvec, w_mod, b_mod)
        if kind == "gqa":
            w_in, q_gain, k_gain, w_out = p
            kv_tiles = GQA_KV_HEADS * HEAD_DIM // tn
            n_tiles = w_in.shape[1] // tn
            segs = ((0, q_tiles, 0, True, scale),
                    (q_tiles, q_tiles + kv_tiles, 1, True, None),
                    (q_tiles + kv_tiles, n_tiles, None, False, None))
            w_qk = _qk_weights(w_in, (q_tiles + kv_tiles) * tn, tn)
            gains = jnp.stack([q_gain, k_gain]).astype(F32)[:, _rope_layout()]
            proj = _in_proj(xc, norm, mods, w_in, dm, tm=tm, tn=tn, segs=segs, rope=rope, w_qk=w_qk,
                            gains=gains)
            g = _gqa_attention(proj, dm, tq=tq, tk=tk, ctx_out=i < len(layers) - 1)
        elif kind == "diff":
            w_in, lq1, lk1, lq2, lk2, subln, w_out = p
            n_tiles = w_in.shape[1] // tn
            segs = ((0, q_tiles, None, True, scale),
                    (q_tiles, 2 * q_tiles, None, True, None),
                    (2 * q_tiles, n_tiles, None, False, None))
            proj = _in_proj(xc, norm, mods, w_in, dm, tm=tm, tn=tn, segs=segs, rope=rope,
                            w_qk=_qk_weights(w_in, 2 * q_tiles * tn, tn))
            lam_vecs = jnp.stack([lq1, lk1, lq2, lk2]).astype(F32)
            g = _diff_attention(proj, lam_vecs, subln, dm, tq=2 * tq, tk=tk, lambda_init=_lambda_init(i))
        else:
            w_in, conv_w, conv_b, dt_bias, a_log, d_skip, ssm_norm, w_out = p
            inner = w_out.shape[0]
            n_main = 2 * inner + 2 * SSM_GROUPS * SSM_STATE
            segs = ((0, n_main // tn, None, False, None),)
            proj, dt, dtT = _in_proj(xc, norm, mods, w_in, dm, tm=tm, tn=tn,
                                     segs=segs, w_dt=w_in[:, n_main:].astype(BF16))
            xbc = _ssd_conv(proj, conv_w, conv_b, dm, col0=inner, tr=tq,
                            tn=math.gcd(inner, conv_w.shape[1]))
            y_f = _ssd_scan(xbc, dt, dtT, dt_bias, a_log, dm, inner=inner, reverse=False)
            g = _ssd_scan(xbc, dt, dtT, dt_bias, a_log, dm, inner=inner, reverse=True,
                          tail=(y_f, proj, d_skip, ssm_norm))
        wide = w_out.shape[0] > d
        xc = _out_proj(g, w_out, xc, mods, dm, tm=tm if wide else _rows(dm), tn=tn)

    out = _final_norm(xc, final_norm, dm, tr=tq)
    return out.reshape(dm.batch, dm.seq, d)


def kernel(x, c, ctx, c_ctx, l0_w_mod, l0_b_mod, l0_norm, l0_w_in, l0_q_gain, l0_k_gain, l0_w_out, l1_w_mod, l1_b_mod, l1_norm, l1_w_in, l1_lambda_q1, l1_lambda_k1, l1_lambda_q2, l1_lambda_k2, l1_subln, l1_w_out, l2_w_mod, l2_b_mod, l2_norm, l2_w_in, l2_conv_w, l2_conv_b, l2_dt_bias, l2_A_log, l2_D, l2_ssm_norm, l2_w_out, l3_w_mod, l3_b_mod, l3_norm, l3_w_in, l3_q_gain, l3_k_gain, l3_w_out, final_norm):
    dm = Dims(batch=x.shape[0], seq=x.shape[1], ctx=ctx.shape[1], d=x.shape[2])
    layers = (
        ("gqa", (l0_w_mod, l0_b_mod, l0_norm), (l0_w_in, l0_q_gain, l0_k_gain, l0_w_out)),
        ("diff", (l1_w_mod, l1_b_mod, l1_norm), (l1_w_in, l1_lambda_q1, l1_lambda_k1, l1_lambda_q2,
                                                 l1_lambda_k2, l1_subln, l1_w_out)),
        ("ssd", (l2_w_mod, l2_b_mod, l2_norm), (l2_w_in, l2_conv_w, l2_conv_b, l2_dt_bias, l2_A_log,
                                                l2_D, l2_ssm_norm, l2_w_out)),
        ("gqa", (l3_w_mod, l3_b_mod, l3_norm), (l3_w_in, l3_q_gain, l3_k_gain, l3_w_out)),
    )
    return _forward(dm, x, c, ctx, c_ctx, layers, final_norm)
```

```python
import collections
import functools
import math

import jax
import jax.numpy as jnp
from jax import lax
from jax.experimental import pallas as pl
from jax.experimental.pallas import tpu as pltpu

F32 = jnp.float32
BF16 = jnp.bfloat16

EPS = 1e-6
ROPE_THETA = 10000.0
HEAD_DIM = 128
GQA_KV_HEADS = 4
SSM_HEAD_DIM = 64
SSM_GROUPS = 8
SSM_STATE = 128
SSM_CONV = 4
SSM_CHUNK = 128
GRID_W = 64
DEPTH = 4
N_MODS = 3
MOD_ROWS = 8
MOD_TILE = 1024
ROW_PARTS = 8
NORM_SLAB = 16
BF16_SUBLANES = 16
VMEM_LIMIT = 56 * 1024 * 1024

Dims = collections.namedtuple("Dims", "batch seq ctx d")


def _rows(dm):
    return dm.ctx + dm.seq


def _silu(v):
    return v * (1.0 / (1.0 + jnp.exp2(v * -math.log2(math.e))))


def _params(semantics):
    return pltpu.CompilerParams(dimension_semantics=semantics, vmem_limit_bytes=VMEM_LIMIT)


def _mod_kernel(c_ref, w_ref, b_ref, o_ref):
    s = _silu(c_ref[...]).astype(BF16)
    o_ref[...] = jnp.dot(s, w_ref[...].astype(BF16), preferred_element_type=F32) + b_ref[...]


def _modulation(cvec, w_mod, b_mod):
    d, n = w_mod.shape
    tn = math.gcd(n, MOD_TILE)
    return pl.pallas_call(
        _mod_kernel,
        grid=(n // tn,),
        in_specs=[pl.BlockSpec((MOD_ROWS, d), lambda j: (0, 0)),
                  pl.BlockSpec((d, tn), lambda j: (0, j)),
                  pl.BlockSpec((1, tn), lambda j: (0, j))],
        out_specs=pl.BlockSpec((MOD_ROWS, tn), lambda j: (0, j)),
        out_shape=jax.ShapeDtypeStruct((MOD_ROWS, n), F32),
        compiler_params=_params(("arbitrary",)),
        name="adaln_mod",
    )(cvec, w_mod, b_mod.reshape(1, n))


def _rope_layout():
    quarter = HEAD_DIM // 4
    idx = jnp.arange(HEAD_DIM).reshape(4, quarter)
    return idx[jnp.array([0, 2, 1, 3])].reshape(HEAD_DIM)


def _in_proj_kernel(*refs, tm, tn, tiles_per_batch, ctx_len, segs, has_rope, has_gain, has_dt):
    it = iter(refs)
    x_ref, g_ref, shl_ref, scl_ref, shc_ref, scc_ref, w_ref = (next(it) for _ in range(7))
    cos_ref = sin_ref = wqk_ref = gain_ref = wdt_ref = wdtT_ref = dt_ref = dtT_ref = None
    if has_rope:
        cos_ref, sin_ref, wqk_ref = next(it), next(it), next(it)
    if has_gain:
        gain_ref = next(it)
    if has_dt:
        wdt_ref, wdtT_ref = next(it), next(it)
    o_ref = next(it)
    if has_dt:
        dt_ref, dtT_ref = next(it), next(it)
    h_ref, mul_ref, add_ref = next(it), next(it), next(it)

    i = pl.program_id(0)
    j = pl.program_id(1)

    part = tm // ROW_PARTS
    starts = range(0, tm, part)

    def modulation_tables():
        d = g_ref.shape[1]
        g = g_ref[...]
        for k, (sc_ref, sh_ref) in enumerate(((scl_ref, shl_ref), (scc_ref, shc_ref))):
            mul_ref[k] = jnp.broadcast_to(g * (1.0 + sc_ref[0]), (NORM_SLAB, d))
            add_ref[k] = jnp.broadcast_to(sh_ref[0], (NORM_SLAB, d))

    def normalise(r0):
        for s0 in range(r0, r0 + part, NORM_SLAB):
            rows = slice(s0, s0 + NORM_SLAB)
            x = x_ref[rows, :]
            y = x * lax.rsqrt(jnp.mean(x * x, axis=-1, keepdims=True) + EPS)
            k = ((i % tiles_per_batch) * tm + s0 < ctx_len).astype(jnp.int32)
            h_ref[rows, :] = (y * mul_ref[k] + add_ref[k]).astype(BF16)

    def tile(gain_idx, rope, scale, first):
        w = wqk_ref[...] if rope else w_ref[...].astype(BF16)
        plain = gain_idx is None and not rope and scale is None
        if plain and not first:
            o_ref[...] = jnp.dot(h_ref[...], w, preferred_element_type=F32).astype(o_ref.dtype)
            return
        if first:
            modulation_tables()
        accs = []
        for r0 in starts:
            if first:
                normalise(r0)
            accs.append(jnp.dot(h_ref[r0:r0 + part, :], w, preferred_element_type=F32))
        for r0, acc in zip(starts, accs):
            rows = slice(r0, r0 + part)
            if plain:
                o_ref[rows, :] = acc.astype(o_ref.dtype)
                continue
            for hh in range(tn // HEAD_DIM):
                cols = slice(hh * HEAD_DIM, (hh + 1) * HEAD_DIM)
                v = acc[:, cols]
                if gain_idx is not None:
                    v = v * lax.rsqrt(jnp.mean(v * v, axis=-1, keepdims=True) + EPS)
                    v = v * gain_ref[gain_idx:gain_idx + 1, :]
                if rope:
                    v = v * cos_ref[rows, :] + pltpu.roll(v, HEAD_DIM // 2, 1) * sin_ref[rows, :]
                if scale is not None:
                    v = v * scale
                o_ref[rows, cols] = v.astype(o_ref.dtype)
        if first and has_dt:
            h = h_ref[...]
            dt_ref[...] = jnp.dot(h, wdt_ref[...], preferred_element_type=F32)
            dtT_ref[...] = lax.dot_general(wdtT_ref[...], h, (((1,), (1,)), ((), ())),
                                           preferred_element_type=F32)

    for (lo, hi, gain_idx, rope, scale) in segs:
        if lo == 0:
            pl.when(j == 0)(functools.partial(tile, gain_idx, rope, scale, True))
            lo = 1
        pl.when((j >= lo) & (j < hi))(functools.partial(tile, gain_idx, rope, scale, False))


def _in_proj(xc, norm_g, mods, w, dm, *, tm, tn, segs, rope=None, w_qk=None, gains=None, w_dt=None):
    m, d = xc.shape
    n = segs[-1][1] * tn
    tpb = _rows(dm) // tm
    mods3 = mods.reshape(MOD_ROWS, 1, N_MODS * d)
    has_rope, has_gain, has_dt = rope is not None, gains is not None, w_dt is not None
    qk_tiles = w_qk.shape[1] // tn if has_rope else 0

    def lat(col):
        return pl.BlockSpec((1, 1, d), lambda i, j: (i // tpb, 0, col))

    def ctx(col):
        return pl.BlockSpec((1, 1, d), lambda i, j: (dm.batch, 0, col))

    args = [xc, norm_g.reshape(1, d), mods3, mods3, mods3, mods3, w]
    in_specs = [pl.BlockSpec((tm, d), lambda i, j: (i, 0)),
                pl.BlockSpec((1, d), lambda i, j: (0, 0)),
                lat(0), lat(1), ctx(0), ctx(1),
                pl.BlockSpec((d, tn), lambda i, j: (0, jnp.maximum(j, qk_tiles)))]
    if has_rope:
        args += list(rope) + [w_qk]
        in_specs += [pl.BlockSpec((tm, HEAD_DIM), lambda i, j: (i % tpb, 0))] * 2
        in_specs.append(pl.BlockSpec((d, tn), lambda i, j: (0, jnp.minimum(j, qk_tiles - 1))))
    if has_gain:
        args.append(gains)
        in_specs.append(pl.BlockSpec(gains.shape, lambda i, j: (0, 0)))
    out_shape = [jax.ShapeDtypeStruct((m, n), BF16)]
    out_specs = [pl.BlockSpec((tm, tn), lambda i, j: (i, j))]
    if has_dt:
        n_dt = w_dt.shape[1]
        args += [w_dt, w_dt.T]
        in_specs += [pl.BlockSpec((d, n_dt), lambda i, j: (0, 0)),
                     pl.BlockSpec((n_dt, d), lambda i, j: (0, 0))]
        out_shape += [jax.ShapeDtypeStruct((m, n_dt), F32), jax.ShapeDtypeStruct((n_dt, m), F32)]
        out_specs += [pl.BlockSpec((tm, n_dt), lambda i, j: (i, 0)),
                      pl.BlockSpec((n_dt, tm), lambda i, j: (0, i))]
    kern = functools.partial(_in_proj_kernel, tm=tm, tn=tn, tiles_per_batch=tpb, ctx_len=dm.ctx,
                             segs=segs, has_rope=has_rope, has_gain=has_gain, has_dt=has_dt)
    out = pl.pallas_call(
        kern,
        grid=(m // tm, n // tn),
        in_specs=in_specs,
        out_specs=out_specs,
        out_shape=out_shape,
        scratch_shapes=[pltpu.VMEM((tm, d), BF16),
                        pltpu.VMEM((2, NORM_SLAB, d), F32),
                        pltpu.VMEM((2, NORM_SLAB, d), F32)],
        compiler_params=_params(("parallel", "arbitrary")),
        name="in_proj",
    )(*args)
    return out if has_dt else out[0]


def _out_proj_kernel(g_ref, w_ref, x_ref, gl_ref, gc_ref, o_ref, *, tm, tiles_per_batch, ctx_len):
    i = pl.program_id(0)
    acc = jnp.dot(g_ref[...], w_ref[...].astype(BF16), preferred_element_type=F32)
    row = (i % tiles_per_batch) * tm + lax.broadcasted_iota(jnp.int32, (tm, 1), 0)
    gate = jnp.where(row < ctx_len, gc_ref[0], gl_ref[0])
    o_ref[...] = x_ref[...] + gate * acc


def _out_proj(g, w, xc, mods, dm, *, tm, tn):
    m, k = g.shape
    d = w.shape[1]
    tpb = _rows(dm) // tm
    tiles_n = d // tn
    mods3 = mods.reshape(MOD_ROWS, 1, N_MODS * d)
    gate_col = 2 * tiles_n
    kern = functools.partial(_out_proj_kernel, tm=tm, tiles_per_batch=tpb, ctx_len=dm.ctx)
    return pl.pallas_call(
        kern,
        grid=(m // tm, tiles_n),
        in_specs=[pl.BlockSpec((tm, k), lambda i, j: (i, 0)),
                  pl.BlockSpec((k, tn), lambda i, j: (0, j)),
                  pl.BlockSpec((tm, tn), lambda i, j: (i, j)),
                  pl.BlockSpec((1, 1, tn), lambda i, j: (i // tpb, 0, gate_col + j)),
                  pl.BlockSpec((1, 1, tn), lambda i, j: (dm.batch, 0, gate_col + j))],
        out_specs=pl.BlockSpec((tm, tn), lambda i, j: (i, j)),
        out_shape=jax.ShapeDtypeStruct((m, d), F32),
        compiler_params=_params(("parallel", "arbitrary")),
        name="out_proj",
    )(g, w, xc, mods3, mods3)


def _lane_tile_reduce(v, op):
    part = v[:, :HEAD_DIM]
    for t in range(1, v.shape[1] // HEAD_DIM):
        part = op(part, v[:, t * HEAD_DIM:(t + 1) * HEAD_DIM])
    return part


def _scores(pairs, k_ref, krows):
    blocks = []
    for q, kcols in pairs:
        half = q.shape[0] // 2
        for qq in (q[:half], q[half:]):
            blocks.append(lax.dot_general(qq, k_ref[krows, kcols], (((1,), (1,)), ((), ())),
                                          preferred_element_type=F32))
    return jnp.concatenate(blocks, axis=0)


def _weighted(p, v_ref, krows, with_sums=False):
    half = p.shape[0] // 2
    pb = p.astype(BF16)
    v = v_ref[krows, :]
    if with_sums:
        v = jnp.concatenate([v, jnp.ones_like(v)], axis=1)
    return jnp.concatenate([jnp.dot(pp, v, preferred_element_type=F32)
                            for pp in (pb[:half], pb[half:])], axis=0)


def _attend_in_place(pairs, k_ref, v_ref, key_rows, tk):
    chunks = [slice(c * tk, (c + 1) * tk) for c in range(key_rows // tk)]
    s = [_scores(pairs, k_ref, krows) for krows in chunks]
    m_part = functools.reduce(jnp.maximum, [_lane_tile_reduce(sc, jnp.maximum) for sc in s])
    m = jnp.max(m_part, axis=1, keepdims=True)
    p = [jnp.exp2(sc - m) for sc in s]
    l_part = functools.reduce(jnp.add, [_lane_tile_reduce(pc, jnp.add) for pc in p])
    acc = functools.reduce(jnp.add, [_weighted(pc, v_ref, krows) for pc, krows in zip(p, chunks)])
    return acc * (1.0 / jnp.sum(l_part, axis=1, keepdims=True))


def _attend_pipelined(pairs_next, s_next_ref, s_cur_ref, finish_cur, k_ref, v_ref, m_ref, key_rows, tk):
    n_chunks = key_rows // tk
    have_cur = s_cur_ref is not None
    dv = v_ref.shape[1]
    mxu_sums = dv == HEAD_DIM
    chunks = [slice(c * tk, (c + 1) * tk) for c in range(n_chunks)]
    acc = None
    for c, krows in enumerate(chunks if have_cur else ()):
        p = jnp.concatenate(
            [jnp.exp2(s_cur_ref[c, :, t * HEAD_DIM:(t + 1) * HEAD_DIM] - m_ref[0])
             for t in range(tk // HEAD_DIM)], axis=1)
        if not mxu_sums:
            part = _lane_tile_reduce(p, jnp.add)
            m_ref[1] = part if c == 0 else m_ref[1] + part
        pv = _weighted(p, v_ref, krows, with_sums=mxu_sums)
        acc = pv if acc is None else acc + pv
    if have_cur:
        if mxu_sums:
            finish_cur(acc[:, :dv] * (1.0 / acc[:, dv:]))
        else:
            finish_cur(acc * (1.0 / jnp.sum(m_ref[1], axis=1, keepdims=True)))
    m_part = None
    for c, krows in enumerate(chunks if pairs_next is not None else ()):
        s = _scores(pairs_next, k_ref, krows)
        s_next_ref[c] = s
        part = _lane_tile_reduce(s, jnp.maximum)
        m_part = part if m_part is None else jnp.maximum(m_part, part)
    if pairs_next is not None:
        m_ref[0] = jnp.broadcast_to(jnp.max(m_part, axis=1, keepdims=True), m_ref.shape[1:])


def _attention_steps(step, queries, finish, k_ref, v_ref, o_ref, s_ref, m_ref, *, tq, tk, ctx_len, rows,
                     ctx_out):
    n_lat = (rows - ctx_len) // tq

    def lat_row0(t):
        return pl.multiple_of(ctx_len + (t - 1) * tq, math.gcd(ctx_len, tq))

    def pipe(t_next, t_cur):
        pairs = None if t_next is None else queries(lat_row0(t_next), tq)
        _attend_pipelined(pairs, None if t_next is None else s_ref,
                          None if t_cur is None else s_ref,
                          None if t_cur is None else functools.partial(finish, lat_row0(t_cur), tq),
                          k_ref, v_ref, m_ref, rows, tk)

    @pl.when(step == 0)
    def _():
        if ctx_out:
            finish(0, ctx_len, _attend_in_place(queries(0, ctx_len), k_ref, v_ref, ctx_len, tk))
        else:
            o_ref[:ctx_len, :] = jnp.zeros((ctx_len, o_ref.shape[1]), o_ref.dtype)
        pipe(1, None)

    if n_lat > 1:
        pl.when((step >= 1) & (step < n_lat))(functools.partial(pipe, step + 1, step))
    pl.when(step == n_lat)(functools.partial(pipe, None, step))


def _gqa_kernel(q_ref, k_ref, v_ref, z_ref, o_ref, s_ref, m_ref, *, tq, tk, group, ctx_len, rows,
                ctx_out):
    def queries(row0, nq):
        q = jnp.concatenate([q_ref[pl.ds(row0, nq), g * HEAD_DIM:(g + 1) * HEAD_DIM]
                             for g in range(group)], axis=0)
        return [(q, slice(0, HEAD_DIM))]

    def finish(row0, nq, o):
        qrows = pl.ds(row0, nq)
        for g in range(group):
            cols = slice(g * HEAD_DIM, (g + 1) * HEAD_DIM)
            z = z_ref[qrows, cols].astype(F32)
            o_ref[qrows, cols] = (o[g * nq:(g + 1) * nq] * _silu(z)).astype(o_ref.dtype)

    _attention_steps(pl.program_id(2), queries, finish, k_ref, v_ref, o_ref, s_ref, m_ref,
                     tq=tq, tk=tk, ctx_len=ctx_len, rows=rows, ctx_out=ctx_out)


def _gqa_attention(proj, dm, *, tq, tk, ctx_out):
    rows = _rows(dm)
    m = proj.shape[0]
    width = dm.d
    heads = width // HEAD_DIM
    group = heads // GQA_KV_HEADS
    gw = group * HEAD_DIM
    k_col0 = width // HEAD_DIM
    v_col0 = k_col0 + GQA_KV_HEADS
    z_col0 = (width + 2 * GQA_KV_HEADS * HEAD_DIM) // gw
    kern = functools.partial(_gqa_kernel, tq=tq, tk=tk, group=group, ctx_len=dm.ctx, rows=rows,
                             ctx_out=ctx_out)
    return pl.pallas_call(
        kern,
        grid=(dm.batch, GQA_KV_HEADS, 1 + dm.seq // tq),
        in_specs=[pl.BlockSpec((rows, gw), lambda b, h, i: (b, h)),
                  pl.BlockSpec((rows, HEAD_DIM), lambda b, h, i: (b, k_col0 + h)),
                  pl.BlockSpec((rows, HEAD_DIM), lambda b, h, i: (b, v_col0 + h)),
                  pl.BlockSpec((rows, gw), lambda b, h, i: (b, z_col0 + h))],
        out_specs=pl.BlockSpec((rows, gw), lambda b, h, i: (b, h)),
        out_shape=jax.ShapeDtypeStruct((m, width), BF16),
        scratch_shapes=[pltpu.VMEM((rows // tk, group * tq, tk), F32),
                        pltpu.VMEM((2, group * tq, HEAD_DIM), F32)],
        compiler_params=_params(("parallel", "parallel", "arbitrary")),
        name="gqa_attention",
    )(proj, proj, proj, proj)


def _diff_kernel(lam_ref, q_ref, k_ref, v_ref, z_ref, subln_ref, o_ref, s_ref, m_ref,
                 *, tq, tk, ctx_len, rows, lambda_init):
    lv = lam_ref[...]
    lam = (jnp.exp(jnp.sum(lv[0:1] * lv[1:2], axis=1, keepdims=True))
           - jnp.exp(jnp.sum(lv[2:3] * lv[3:4], axis=1, keepdims=True)) + lambda_init)

    def queries(row0, nq):
        halves = [slice(t * HEAD_DIM, (t + 1) * HEAD_DIM) for t in range(2)]
        return [(q_ref[pl.ds(row0, nq), cols], cols) for cols in halves]

    def finish(row0, nq, o):
        qrows = pl.ds(row0, nq)
        o = o[:nq] - lam * o[nq:]
        o = o * lax.rsqrt(jnp.mean(o * o, axis=-1, keepdims=True) + EPS) * subln_ref[...]
        o = o * (1.0 - lambda_init)
        o_ref[qrows, :] = (o * _silu(z_ref[qrows, :].astype(F32))).astype(o_ref.dtype)

    _attention_steps(pl.program_id(2), queries, finish, k_ref, v_ref, o_ref, s_ref, m_ref,
                     tq=tq, tk=tk, ctx_len=ctx_len, rows=rows, ctx_out=True)


def _diff_attention(proj, lam_vecs, subln, dm, *, tq, tk, lambda_init):
    rows = _rows(dm)
    m = proj.shape[0]
    width = dm.d
    hw = 2 * HEAD_DIM
    heads = width // hw
    kern = functools.partial(_diff_kernel, tq=tq, tk=tk, ctx_len=dm.ctx, rows=rows, lambda_init=lambda_init)

    def head_block(first):
        return pl.BlockSpec((rows, hw), lambda b, h, i: (b, first + h))

    return pl.pallas_call(
        kern,
        grid=(dm.batch, heads, 1 + dm.seq // tq),
        in_specs=[pl.BlockSpec(lam_vecs.shape, lambda b, h, i: (0, 0)),
                  head_block(0), head_block(heads), head_block(2 * heads), head_block(3 * heads),
                  pl.BlockSpec((1, hw), lambda b, h, i: (0, 0))],
        out_specs=head_block(0),
        out_shape=jax.ShapeDtypeStruct((m, width), BF16),
        scratch_shapes=[pltpu.VMEM((rows // tk, 2 * tq, tk), F32),
                        pltpu.VMEM((2, 2 * tq, HEAD_DIM), F32)],
        compiler_params=_params(("parallel", "parallel", "arbitrary")),
        name="diff_attention",
    )(lam_vecs, proj, proj, proj, proj, subln.reshape(1, hw))


CONV_TAPS = (-1, 0, 1, 2)
CONV_STRIP = 256


def _conv_kernel(cur_ref, prev_ref, next_ref, shift_ref, w_ref, b_ref, o_ref, *, tr, tiles_per_batch,
                 ctx_tiles):
    r = pl.program_id(1)
    edge = BF16_SUBLANES
    centre = CONV_TAPS.index(0)
    has_prev = (r != 0) & (r != ctx_tiles)
    has_next = (r != ctx_tiles - 1) & (r != tiles_per_batch - 1)
    row = lax.broadcasted_iota(jnp.int32, (edge, 1), 0)
    for c0 in range(0, cur_ref.shape[1], CONV_STRIP):
        cols = slice(c0, c0 + CONV_STRIP)
        xb = cur_ref[:, cols]
        w = [w_ref[t:t + 1, cols] for t in range(SSM_CONV)]
        y = b_ref[:, cols] + w[centre] * xb.astype(F32)
        for k, tap in enumerate(t for t in range(SSM_CONV) if t != centre):
            y = y + w[tap] * jnp.dot(shift_ref[k], xb, preferred_element_type=F32)
        o_ref[edge:tr - edge, cols] = _silu(y[edge:tr - edge]).astype(o_ref.dtype)
        prev = jnp.where(has_prev, prev_ref[edge - 1:edge, cols].astype(F32), 0.0)
        nxt0 = jnp.where(has_next, next_ref[0:1, cols].astype(F32), 0.0)
        nxt1 = jnp.where(has_next, next_ref[1:2, cols].astype(F32), 0.0)
        top = y[:edge] + jnp.where(row == 0, w[0] * prev, 0.0)
        o_ref[:edge, cols] = _silu(top).astype(o_ref.dtype)
        bottom = (y[tr - edge:] + jnp.where(row == edge - 2, w[3] * nxt0, 0.0)
                  + jnp.where(row == edge - 1, w[2] * nxt0 + w[3] * nxt1, 0.0))
        o_ref[tr - edge:, cols] = _silu(bottom).astype(o_ref.dtype)


def _ssd_conv(proj, conv_w, conv_b, dm, *, col0, tr, tn):
    rows = _rows(dm)
    m = proj.shape[0]
    n = conv_w.shape[1]
    tpb = rows // tr
    sub = tr // BF16_SUBLANES
    last = m // BF16_SUBLANES - 1
    c0 = col0 // tn
    shifts = jnp.stack([jnp.eye(tr, k=off, dtype=BF16) for off in CONV_TAPS if off != 0])
    kern = functools.partial(_conv_kernel, tr=tr, tiles_per_batch=tpb, ctx_tiles=dm.ctx // tr)
    return pl.pallas_call(
        kern,
        grid=(dm.batch, tpb, n // tn),
        in_specs=[pl.BlockSpec((tr, tn), lambda b, r, j: (b * tpb + r, c0 + j)),
                  pl.BlockSpec((BF16_SUBLANES, tn),
                               lambda b, r, j: (jnp.maximum((b * tpb + r) * sub - 1, 0), c0 + j)),
                  pl.BlockSpec((BF16_SUBLANES, tn),
                               lambda b, r, j: (jnp.minimum((b * tpb + r + 1) * sub, last), c0 + j)),
                  pl.BlockSpec(shifts.shape, lambda b, r, j: (0, 0, 0)),
                  pl.BlockSpec((SSM_CONV, tn), lambda b, r, j: (0, j)),
                  pl.BlockSpec((1, tn), lambda b, r, j: (0, j))],
        out_specs=pl.BlockSpec((tr, tn), lambda b, r, j: (b * tpb + r, j)),
        out_shape=jax.ShapeDtypeStruct((m, n), BF16),
        compiler_params=_params(("parallel", "parallel", "parallel")),
        name="ssd_conv",
    )(proj, proj, proj, shifts, conv_w, conv_b.reshape(1, n))


def _split_dot(a, b_hi_exact, dims):
    hi = a.astype(BF16)
    lo = (a - hi.astype(F32)).astype(BF16)
    if dims == "ab":
        return (jnp.dot(hi, b_hi_exact, preferred_element_type=F32)
                + jnp.dot(lo, b_hi_exact, preferred_element_type=F32))
    return (jnp.dot(b_hi_exact, hi, preferred_element_type=F32)
            + jnp.dot(b_hi_exact, lo, preferred_element_type=F32))


def _ssd_kernel(*refs, heads, reverse, finish):
    it = iter(refs)
    (x_ref, b_ref, c_ref, dt_ref, dtT_ref, bias_ref, biasT_ref, alog_ref, alogT_ref,
     expand_ref) = (next(it) for _ in range(10))
    other_ref = z_ref = dskip_ref = gain_ref = None
    if finish:
        other_ref, z_ref, dskip_ref, gain_ref = (next(it) for _ in range(4))
    y_ref, state_ref = next(it), next(it)
    q = SSM_CHUNK
    hpg = heads // SSM_GROUPS
    gw = hpg * SSM_HEAD_DIM
    pw = 2 * SSM_HEAD_DIM
    step = pl.program_id(1)

    @pl.when(step == 0)
    def _():
        state_ref[...] = jnp.zeros(state_ref.shape, F32)

    def softplus(v):
        return jnp.maximum(v, 0.0) + jnp.log(1.0 + jnp.exp(-jnp.abs(v)))

    dcol = heads if reverse else 0
    dt = softplus(dt_ref[:, dcol:dcol + heads] + bias_ref[...])
    dtT = softplus(dtT_ref[...] + biasT_ref[...])
    a = dt * -jnp.exp(alog_ref[...])
    aT = dtT * -jnp.exp(alogT_ref[...])
    ri = lax.broadcasted_iota(jnp.int32, (q, q), 0)
    ci = lax.broadcasted_iota(jnp.int32, (q, q), 1)
    before = (ci >= ri) if reverse else (ci <= ri)
    tri = before.astype(BF16)
    triT = ((ri >= ci) if reverse else (ri <= ci)).astype(BF16)
    cum = _split_dot(a, tri, "ba")
    cumT = _split_dot(aT, triT, "ab")
    end = 0 if reverse else q - 1
    total = cum[end:end + 1, :]
    expand = expand_ref[...]
    w_full = jnp.dot((jnp.exp(total - cum) * dt).astype(BF16), expand,
                     preferred_element_type=F32)
    e_full = jnp.dot(jnp.exp(cum).astype(BF16), expand, preferred_element_type=F32)
    cd_full = _split_dot(jnp.broadcast_to(jnp.exp(total), (8, heads)), expand, "ab")[0:1]
    log2e = math.log2(math.e)
    cum2 = cum * log2e
    lcumT2 = (cumT - jnp.log(dtT)) * log2e
    low = lax.broadcasted_iota(jnp.int32, (q, pw), 1) < SSM_HEAD_DIM

    for g in range(SSM_GROUPS):
        gcols = slice(g * gw, (g + 1) * gw)
        ncols = slice(g * SSM_STATE, (g + 1) * SSM_STATE)
        bg = b_ref[:, ncols]
        cg = c_ref[:, ncols]
        cb = lax.dot_general(cg, bg, (((1,), (1,)), ((), ())), preferred_element_type=F32)
        st = state_ref[:, gcols]
        y_off = jnp.dot(cg, st.astype(BF16), preferred_element_type=F32) * e_full[:, gcols]
        ys = []
        for j in range(hpg // 2):
            xp = x_ref[:, g * gw + j * pw:g * gw + (j + 1) * pw].astype(F32)
            xbd = jnp.concatenate([jnp.where(low, xp, 0.0), jnp.where(low, 0.0, xp)], axis=0).astype(BF16)
            mats = []
            for hh in range(2):
                h = g * hpg + 2 * j + hh
                diff = cum2[:, h:h + 1] - lcumT2[h:h + 1, :]
                mats.append((cb * jnp.exp2(jnp.where(before, diff, -jnp.inf))).astype(BF16))
            ys.append(jnp.dot(jnp.concatenate(mats, axis=1), xbd, preferred_element_type=F32)
                      + y_off[:, j * pw:(j + 1) * pw])
        xg = x_ref[:, gcols].astype(F32)
        if finish:
            y = jnp.concatenate(ys, axis=1) + other_ref[:, gcols].astype(F32) + dskip_ref[:, gcols] * xg
            v = y * _silu(z_ref[:, gcols].astype(F32))
            v = v * lax.rsqrt(jnp.mean(v * v, axis=-1, keepdims=True) + EPS) * gain_ref[:, gcols]
            y_ref[:, gcols] = v.astype(y_ref.dtype)
        else:
            for j, y in enumerate(ys):
                y_ref[:, g * gw + j * pw:g * gw + (j + 1) * pw] = y.astype(y_ref.dtype)
        xw = (xg * w_full[:, gcols]).astype(BF16)
        bgT = bg.astype(F32).T.astype(BF16)
        state_ref[:, gcols] = st * cd_full[:, gcols] + jnp.dot(bgT, xw, preferred_element_type=F32)


def _ssd_scan(xbc, dt, dtT, dt_bias, a_log, dm, *, inner, reverse, tail=None):
    rows = _rows(dm)
    m = xbc.shape[0]
    q = SSM_CHUNK
    heads = inner // SSM_HEAD_DIM
    nc = rows // q
    ctx_chunks = dm.ctx // q
    sw = SSM_GROUPS * SSM_STATE
    d = 1 if reverse else 0

    def chunk(b, s):
        if reverse:
            s = jnp.where(s < ctx_chunks, ctx_chunks - 1 - s, nc - 1 - (s - ctx_chunks))
        return b * nc + s

    expand = jnp.repeat(jnp.eye(heads, dtype=BF16), SSM_HEAD_DIM, axis=1)
    row_spec = pl.BlockSpec((q, inner), lambda b, s: (chunk(b, s), 0))
    vec_spec = pl.BlockSpec((1, inner), lambda b, s: (0, 0))
    args = [xbc, xbc, xbc, dt, dtT, dt_bias[d][None, :], dt_bias[d][:, None], a_log[d][None, :],
            a_log[d][:, None], expand]
    in_specs = [row_spec,
                pl.BlockSpec((q, sw), lambda b, s: (chunk(b, s), inner // sw)),
                pl.BlockSpec((q, sw), lambda b, s: (chunk(b, s), inner // sw + 1)),
                pl.BlockSpec((q, 2 * heads), lambda b, s: (chunk(b, s), 0)),
                pl.BlockSpec((heads, q), lambda b, s: (d, chunk(b, s))),
                pl.BlockSpec((1, heads), lambda b, s: (0, 0)),
                pl.BlockSpec((heads, 1), lambda b, s: (0, 0)),
                pl.BlockSpec((1, heads), lambda b, s: (0, 0)),
                pl.BlockSpec((heads, 1), lambda b, s: (0, 0)),
                pl.BlockSpec((heads, inner), lambda b, s: (0, 0))]
    if tail is not None:
        y_other, proj, d_skip, gain = tail
        args += [y_other, proj, jnp.repeat(d_skip.astype(F32), SSM_HEAD_DIM).reshape(1, inner),
                 gain.reshape(1, inner)]
        in_specs += [row_spec, row_spec, vec_spec, vec_spec]
    kern = functools.partial(_ssd_kernel, heads=heads, reverse=reverse, finish=tail is not None)
    return pl.pallas_call(
        kern,
        grid=(dm.batch, nc),
        in_specs=in_specs,
        out_specs=row_spec,
        out_shape=jax.ShapeDtypeStruct((m, inner), BF16),
        scratch_shapes=[pltpu.VMEM((SSM_STATE, inner), F32)],
        compiler_params=_params(("parallel", "arbitrary")),
        name="ssd_scan_bwd" if reverse else "ssd_scan_fwd",
    )(*args)


def _final_norm_kernel(x_ref, g_ref, o_ref):
    x = x_ref[...]
    o_ref[...] = x * lax.rsqrt(jnp.mean(x * x, axis=-1, keepdims=True) + EPS) * g_ref[...]


def _final_norm(xc, gain, dm, *, tr):
    rows = _rows(dm)
    tpb = rows // tr
    lat_tiles = dm.seq // tr
    ctx_tiles = dm.ctx // tr
    return pl.pallas_call(
        _final_norm_kernel,
        grid=(dm.batch, lat_tiles),
        in_specs=[pl.BlockSpec((tr, dm.d), lambda b, i: (b * tpb + ctx_tiles + i, 0)),
                  pl.BlockSpec((1, dm.d), lambda b, i: (0, 0))],
        out_specs=pl.BlockSpec((tr, dm.d), lambda b, i: (b * lat_tiles + i, 0)),
        out_shape=jax.ShapeDtypeStruct((dm.batch * dm.seq, dm.d), F32),
        compiler_params=_params(("parallel", "parallel")),
        name="final_norm",
    )(xc, gain.reshape(1, dm.d))


def _rope_tables(dm):
    d_axis = HEAD_DIM // 2
    t = jnp.arange(dm.seq, dtype=jnp.int32)
    inv_freq = jnp.power(ROPE_THETA, -jnp.arange(0, d_axis, 2, dtype=F32) / d_axis)
    ang_r = (t // GRID_W).astype(F32)[:, None] * inv_freq[None, :]
    ang_c = (t % GRID_W).astype(F32)[:, None] * inv_freq[None, :]
    cos = jnp.concatenate([jnp.cos(ang_r), jnp.cos(ang_c), jnp.cos(ang_r), jnp.cos(ang_c)], axis=1)
    sin = jnp.concatenate([-jnp.sin(ang_r), -jnp.sin(ang_c), jnp.sin(ang_r), jnp.sin(ang_c)], axis=1)
    cos = jnp.concatenate([jnp.ones((dm.ctx, HEAD_DIM), F32), cos], axis=0)
    sin = jnp.concatenate([jnp.zeros((dm.ctx, HEAD_DIM), F32), sin], axis=0)
    return cos, sin


def _relayout_kernel(w_ref, o_ref):
    tn = w_ref.shape[1]
    quarter = HEAD_DIM // 4
    r = lax.broadcasted_iota(jnp.int32, (tn, tn), 0)
    c = lax.broadcasted_iota(jnp.int32, (tn, tn), 1)
    blk = (c // quarter) % 4
    src_blk = jnp.where(blk == 1, 2, jnp.where(blk == 2, 1, blk))
    src = (c // HEAD_DIM) * HEAD_DIM + src_blk * quarter + c % quarter
    perm = (r == src).astype(BF16)
    o_ref[...] = jnp.dot(w_ref[...].astype(BF16), perm, preferred_element_type=F32).astype(BF16)


def _qk_weights(w_in, qk_cols, tn):
    d = w_in.shape[0]
    return pl.pallas_call(
        _relayout_kernel,
        grid=(qk_cols // tn,),
        in_specs=[pl.BlockSpec((d, tn), lambda j: (0, j))],
        out_specs=pl.BlockSpec((d, tn), lambda j: (0, j)),
        out_shape=jax.ShapeDtypeStruct((d, qk_cols), BF16),
        compiler_params=_params(("parallel",)),
        name="qk_relayout",
    )(w_in)


def _lambda_init(layer_idx):
    return 0.8 - 0.6 * math.exp(-0.3 * layer_idx)


def _tiles(dm):
    rows = _rows(dm)
    tm = rows // 2 if (rows // 2) % 128 == 0 else rows
    return dict(tm=tm, tn=min(512, dm.d), tq=dm.ctx, tk=dm.ctx)


def _forward(dm, x, c, ctx, c_ctx, layers, final_norm):
    d = dm.d
    til = _tiles(dm)
    tm, tn, tq, tk = til["tm"], til["tn"], til["tq"], til["tk"]
    scale = HEAD_DIM ** -0.5 * math.log2(math.e)
    xc =jnp.concatenate([ctx, x], axis=1).reshape(dm.batch * _rows(dm), d)
    cvec = jnp.concatenate([c, c_ctx[None, :], jnp.zeros((MOD_ROWS - dm.batch - 1, d), F32)], axis=0)
    rope = _rope_tables(dm)
    q_tiles = d // tn

    for i, (kind, (w_mod, b_mod, norm), p) in enumerate(layers):
        mods = _modulation(cvec, w_mod, b_mod)
        if kind == "gqa":
            w_in, q_gain, k_gain, w_out = p
            kv_tiles = GQA_KV_HEADS * HEAD_DIM // tn
            n_tiles = w_in.shape[1] // tn
            segs = ((0, q_tiles, 0, True, scale),
                    (q_tiles, q_tiles + kv_tiles, 1, True, None),
                    (q_tiles + kv_tiles, n_tiles, None, False, None))
            w_qk = _qk_weights(w_in, (q_tiles + kv_tiles) * tn, tn)
            gains = jnp.stack([q_gain, k_gain]).astype(F32)[:, _rope_layout()]
            proj = _in_proj(xc, norm, mods, w_in, dm, tm=tm, tn=tn, segs=segs, rope=rope, w_qk=w_qk,
                            gains=gains)
            g = _gqa_attention(proj, dm, tq=min(2 * tq, dm.seq), tk=tk, ctx_out=i < len(layers) - 1)
        elif kind == "diff":
            w_in, lq1, lk1, lq2, lk2, subln, w_out = p
            n_tiles = w_in.shape[1] // tn
            segs = ((0, q_tiles, None, True, scale),
                    (q_tiles, 2 * q_tiles, None, True, None),
                    (2 * q_tiles, n_tiles, None, False, None))
            proj = _in_proj(xc, norm, mods, w_in, dm, tm=tm, tn=tn, segs=segs, rope=rope,
                            w_qk=_qk_weights(w_in, 2 * q_tiles * tn, tn))
            lam_vecs = jnp.stack([lq1, lk1, lq2, lk2]).astype(F32)
            g = _diff_attention(proj, lam_vecs, subln, dm, tq=min(4 * tq, dm.seq), tk=tk,
                                lambda_init=_lambda_init(i))
        else:
            w_in, conv_w, conv_b, dt_bias, a_log, d_skip, ssm_norm, w_out = p
            inner = w_out.shape[0]
            n_main = 2 * inner + 2 * SSM_GROUPS * SSM_STATE
            tn_ssd = math.gcd(n_main, 2 * tn)
            segs = ((0, n_main // tn_ssd, None, False, None),)
            proj, dt, dtT = _in_proj(xc, norm, mods, w_in, dm, tm=tm, tn=tn_ssd,
                                     segs=segs, w_dt=w_in[:, n_main:].astype(BF16))
            xbc = _ssd_conv(proj, conv_w, conv_b, dm, col0=inner, tr=tq,
                            tn=math.gcd(inner, conv_w.shape[1]))
            y_f = _ssd_scan(xbc, dt, dtT, dt_bias, a_log, dm, inner=inner, reverse=False)
            g = _ssd_scan(xbc, dt, dtT, dt_bias, a_log, dm, inner=inner, reverse=True,
                          tail=(y_f, proj, d_skip, ssm_norm))
        wide = w_out.shape[0] > d
        xc = _out_proj(g, w_out, xc, mods, dm, tm=tm if wide else _rows(dm), tn=tn)

    out = _final_norm(xc, final_norm, dm, tr=tq)
    return out.reshape(dm.batch, dm.seq, d)


def kernel(x, c, ctx, c_ctx, l0_w_mod, l0_b_mod, l0_norm, l0_w_in, l0_q_gain, l0_k_gain, l0_w_out, l1_w_mod, l1_b_mod, l1_norm, l1_w_in, l1_lambda_q1, l1_lambda_k1, l1_lambda_q2, l1_lambda_k2, l1_subln, l1_w_out, l2_w_mod, l2_b_mod, l2_norm, l2_w_in, l2_conv_w, l2_conv_b, l2_dt_bias, l2_A_log, l2_D, l2_ssm_norm, l2_w_out, l3_w_mod, l3_b_mod, l3_norm, l3_w_in, l3_q_gain, l3_k_gain, l3_w_out, final_norm):
    dm = Dims(batch=x.shape[0], seq=x.shape[1], ctx=ctx.shape[1], d=x.shape[2])
    layers = (
        ("gqa", (l0_w_mod, l0_b_mod, l0_norm), (l0_w_in, l0_q_gain, l0_k_gain, l0_w_out)),
        ("diff", (l1_w_mod, l1_b_mod, l1_norm), (l1_w_in, l1_lambda_q1, l1_lambda_k1, l1_lambda_q2,
                                                 l1_lambda_k2, l1_subln, l1_w_out)),
        ("ssd", (l2_w_mod, l2_b_mod, l2_norm), (l2_w_in, l2_conv_w, l2_conv_b, l2_dt_bias, l2_A_log,
                                                l2_D, l2_ssm_norm, l2_w_out)),
        ("gqa", (l3_w_mod, l3_b_mod, l3_norm), (l3_w_in, l3_q_gain, l3_k_gain, l3_w_out)),
    )
    return _forward(dm, x, c, ctx, c_ctx, layers, final_norm)
```

```python
import collections
import functools
import math

import jax
import jax.numpy as jnp
from jax import lax
from jax.experimental import pallas as pl
from jax.experimental.pallas import tpu as pltpu

F32 = jnp.float32
BF16 = jnp.bfloat16

EPS = 1e-6
ROPE_THETA = 10000.0
HEAD_DIM = 128
GQA_KV_HEADS = 4
SSM_HEAD_DIM = 64
SSM_GROUPS = 8
SSM_STATE = 128
SSM_CONV = 4
SSM_CHUNK = 128
SSM_CHUNKS_PER_STEP = 2
GRID_W = 64
DEPTH = 4
N_MODS = 3
MOD_ROWS = 8
MOD_TILE = 1024
ROW_PARTS = 8
NORM_SLAB = 16
BF16_SUBLANES = 16
VMEM_LIMIT = 56 * 1024 * 1024

Dims = collections.namedtuple("Dims", "batch seq ctx d")


def _rows(dm):
    return dm.ctx + dm.seq


def _silu(v):
    return v * (1.0 / (1.0 + jnp.exp2(v * -math.log2(math.e))))


def _params(semantics):
    return pltpu.CompilerParams(dimension_semantics=semantics, vmem_limit_bytes=VMEM_LIMIT)


def _mod_kernel(c_ref, w_ref, b_ref, o_ref):
    s = _silu(c_ref[...]).astype(BF16)
    o_ref[...] = jnp.dot(s, w_ref[...].astype(BF16), preferred_element_type=F32) + b_ref[...]


def _modulation(cvec, w_mod, b_mod):
    d, n = w_mod.shape
    tn = math.gcd(n, MOD_TILE)
    return pl.pallas_call(
        _mod_kernel,
        grid=(n // tn,),
        in_specs=[pl.BlockSpec((MOD_ROWS, d), lambda j: (0, 0)),
                  pl.BlockSpec((d, tn), lambda j: (0, j)),
                  pl.BlockSpec((1, tn), lambda j: (0, j))],
        out_specs=pl.BlockSpec((MOD_ROWS, tn), lambda j: (0, j)),
        out_shape=jax.ShapeDtypeStruct((MOD_ROWS, n), F32),
        compiler_params=_params(("arbitrary",)),
        name="adaln_mod",
    )(cvec, w_mod, b_mod.reshape(1, n))


def _rope_layout():
    quarter = HEAD_DIM // 4
    idx = jnp.arange(HEAD_DIM).reshape(4, quarter)
    return idx[jnp.array([0, 2, 1, 3])].reshape(HEAD_DIM)


def _in_proj_kernel(*refs, tm, tn, tiles_per_batch, ctx_len, segs, has_rope, has_gain, has_dt):
    it = iter(refs)
    x_ref, g_ref, shl_ref, scl_ref, shc_ref, scc_ref, w_ref = (next(it) for _ in range(7))
    cos_ref = sin_ref = wqk_ref = gain_ref = wdt_ref = wdtT_ref = dt_ref = dtT_ref = None
    if has_rope:
        cos_ref, sin_ref, wqk_ref = next(it), next(it), next(it)
    if has_gain:
        gain_ref = next(it)
    if has_dt:
        wdt_ref, wdtT_ref = next(it), next(it)
    o_ref = next(it)
    if has_dt:
        dt_ref, dtT_ref = next(it), next(it)
    h_ref, mul_ref, add_ref = next(it), next(it), next(it)

    i = pl.program_id(0)
    j = pl.program_id(1)

    part = tm // ROW_PARTS
    starts = range(0, tm, part)

    def modulation_tables():
        d = g_ref.shape[1]
        g = g_ref[...]
        for k, (sc_ref, sh_ref) in enumerate(((scl_ref, shl_ref), (scc_ref, shc_ref))):
            mul_ref[k] = jnp.broadcast_to(g * (1.0 + sc_ref[0]), (NORM_SLAB, d))
            add_ref[k] = jnp.broadcast_to(sh_ref[0], (NORM_SLAB, d))

    def normalise(r0):
        for s0 in range(r0, r0 + part, NORM_SLAB):
            rows = slice(s0, s0 + NORM_SLAB)
            x = x_ref[rows, :]
            y = x * lax.rsqrt(jnp.mean(x * x, axis=-1, keepdims=True) + EPS)
            k = ((i % tiles_per_batch) * tm + s0 < ctx_len).astype(jnp.int32)
            h_ref[rows, :] = (y * mul_ref[k] + add_ref[k]).astype(BF16)

    def tile(gain_idx, rope, scale, first):
        w = wqk_ref[...] if rope else w_ref[...].astype(BF16)
        plain = gain_idx is None and not rope and scale is None
        if plain and not first:
            o_ref[...] = jnp.dot(h_ref[...], w, preferred_element_type=F32).astype(o_ref.dtype)
            return
        if first:
            modulation_tables()
        accs = []
        for r0 in starts:
            if first:
                normalise(r0)
            accs.append(jnp.dot(h_ref[r0:r0 + part, :], w, preferred_element_type=F32))
        for r0, acc in zip(starts, accs):
            rows = slice(r0, r0 + part)
            if plain:
                o_ref[rows, :] = acc.astype(o_ref.dtype)
                continue
            for hh in range(tn // HEAD_DIM):
                cols = slice(hh * HEAD_DIM, (hh + 1) * HEAD_DIM)
                v = acc[:, cols]
                if gain_idx is not None:
                    v = v * lax.rsqrt(jnp.mean(v * v, axis=-1, keepdims=True) + EPS)
                    v = v * gain_ref[gain_idx:gain_idx + 1, :]
                if rope:
                    v = v * cos_ref[rows, :] + pltpu.roll(v, HEAD_DIM // 2, 1) * sin_ref[rows, :]
                if scale is not None:
                    v = v * scale
                o_ref[rows, cols] = v.astype(o_ref.dtype)
        if first and has_dt:
            h = h_ref[...]
            dt_ref[...] = jnp.dot(h, wdt_ref[...], preferred_element_type=F32)
            dtT_ref[...] = lax.dot_general(wdtT_ref[...], h, (((1,), (1,)), ((), ())),
                                           preferred_element_type=F32)

    for (lo, hi, gain_idx, rope, scale) in segs:
        if lo == 0:
            pl.when(j == 0)(functools.partial(tile, gain_idx, rope, scale, True))
            lo = 1
        pl.when((j >= lo) & (j < hi))(functools.partial(tile, gain_idx, rope, scale, False))


def _in_proj(xc, norm_g, mods, w, dm, *, tm, tn, segs, rope=None, w_qk=None, gains=None, w_dt=None):
    m, d = xc.shape
    n = segs[-1][1] * tn
    tpb = _rows(dm) // tm
    mods3 = mods.reshape(MOD_ROWS, 1, N_MODS * d)
    has_rope, has_gain, has_dt = rope is not None, gains is not None, w_dt is not None
    qk_tiles = w_qk.shape[1] // tn if has_rope else 0

    def lat(col):
        return pl.BlockSpec((1, 1, d), lambda i, j: (i // tpb, 0, col))

    def ctx(col):
        return pl.BlockSpec((1, 1, d), lambda i, j: (dm.batch, 0, col))

    args = [xc, norm_g.reshape(1, d), mods3, mods3, mods3, mods3, w]
    in_specs = [pl.BlockSpec((tm, d), lambda i, j: (i, 0)),
                pl.BlockSpec((1, d), lambda i, j: (0, 0)),
                lat(0), lat(1), ctx(0), ctx(1),
                pl.BlockSpec((d, tn), lambda i, j: (0, jnp.maximum(j, qk_tiles)))]
    if has_rope:
        args += list(rope) + [w_qk]
        in_specs += [pl.BlockSpec((tm, HEAD_DIM), lambda i, j: (i % tpb, 0))] * 2
        in_specs.append(pl.BlockSpec((d, tn), lambda i, j: (0, jnp.minimum(j, qk_tiles - 1))))
    if has_gain:
        args.append(gains)
        in_specs.append(pl.BlockSpec(gains.shape, lambda i, j: (0, 0)))
    out_shape = [jax.ShapeDtypeStruct((m, n), BF16)]
    out_specs = [pl.BlockSpec((tm, tn), lambda i, j: (i, j))]
    if has_dt:
        n_dt = w_dt.shape[1]
        args += [w_dt, w_dt.T]
        in_specs += [pl.BlockSpec((d, n_dt), lambda i, j: (0, 0)),
                     pl.BlockSpec((n_dt, d), lambda i, j: (0, 0))]
        out_shape += [jax.ShapeDtypeStruct((m, n_dt), F32), jax.ShapeDtypeStruct((n_dt, m), F32)]
        out_specs += [pl.BlockSpec((tm, n_dt), lambda i, j: (i, 0)),
                      pl.BlockSpec((n_dt, tm), lambda i, j: (0, i))]
    kern = functools.partial(_in_proj_kernel, tm=tm, tn=tn, tiles_per_batch=tpb, ctx_len=dm.ctx,
                             segs=segs, has_rope=has_rope, has_gain=has_gain, has_dt=has_dt)
    out = pl.pallas_call(
        kern,
        grid=(m // tm, n // tn),
        in_specs=in_specs,
        out_specs=out_specs,
        out_shape=out_shape,
        scratch_shapes=[pltpu.VMEM((tm, d), BF16),
                        pltpu.VMEM((2, NORM_SLAB, d), F32),
                        pltpu.VMEM((2, NORM_SLAB, d), F32)],
        compiler_params=_params(("parallel", "arbitrary")),
        name="in_proj",
    )(*args)
    return out if has_dt else out[0]


def _out_proj_kernel(g_ref, w_ref, x_ref, gl_ref, gc_ref, o_ref, *, tm, tiles_per_batch, ctx_len):
    i = pl.program_id(0)
    acc = jnp.dot(g_ref[...], w_ref[...].astype(BF16), preferred_element_type=F32)
    row = (i % tiles_per_batch) * tm + lax.broadcasted_iota(jnp.int32, (tm, 1), 0)
    gate = jnp.where(row < ctx_len, gc_ref[0], gl_ref[0])
    o_ref[...] = x_ref[...] + gate * acc


def _out_proj(g, w, xc, mods, dm, *, tm, tn):
    m, k = g.shape
    d = w.shape[1]
    tpb = _rows(dm) // tm
    tiles_n = d // tn
    mods3 = mods.reshape(MOD_ROWS, 1, N_MODS * d)
    gate_col = 2 * tiles_n
    kern = functools.partial(_out_proj_kernel, tm=tm, tiles_per_batch=tpb, ctx_len=dm.ctx)
    return pl.pallas_call(
        kern,
        grid=(m // tm, tiles_n),
        in_specs=[pl.BlockSpec((tm, k), lambda i, j: (i, 0)),
                  pl.BlockSpec((k, tn), lambda i, j: (0, j)),
                  pl.BlockSpec((tm, tn), lambda i, j: (i, j)),
                  pl.BlockSpec((1, 1, tn), lambda i, j: (i // tpb, 0, gate_col + j)),
                  pl.BlockSpec((1, 1, tn), lambda i, j: (dm.batch, 0, gate_col + j))],
        out_specs=pl.BlockSpec((tm, tn), lambda i, j: (i, j)),
        out_shape=jax.ShapeDtypeStruct((m, d), F32),
        compiler_params=_params(("parallel", "arbitrary")),
        name="out_proj",
    )(g, w, xc, mods3, mods3)


def _lane_tile_reduce(v, op):
    part = v[:, :HEAD_DIM]
    for t in range(1, v.shape[1] // HEAD_DIM):
        part = op(part, v[:, t * HEAD_DIM:(t + 1) * HEAD_DIM])
    return part


def _scores(pairs, k_ref, krows):
    blocks = []
    for q, kcols in pairs:
        half = q.shape[0] // 2
        for qq in (q[:half], q[half:]):
            blocks.append(lax.dot_general(qq, k_ref[krows, kcols], (((1,), (1,)), ((), ())),
                                          preferred_element_type=F32))
    return jnp.concatenate(blocks, axis=0)


def _weighted(p, v_ref, krows, with_sums=False):
    half = p.shape[0] // 2
    pb = p.astype(BF16)
    v = v_ref[krows, :]
    if with_sums:
        v = jnp.concatenate([v, jnp.ones_like(v)], axis=1)
    return jnp.concatenate([jnp.dot(pp, v, preferred_element_type=F32)
                            for pp in (pb[:half], pb[half:])], axis=0)


def _attend_in_place(pairs, k_ref, v_ref, key_rows, tk):
    chunks = [slice(c * tk, (c + 1) * tk) for c in range(key_rows // tk)]
    s = [_scores(pairs, k_ref, krows) for krows in chunks]
    m_part = functools.reduce(jnp.maximum, [_lane_tile_reduce(sc, jnp.maximum) for sc in s])
    m = jnp.max(m_part, axis=1, keepdims=True)
    p = [jnp.exp2(sc - m) for sc in s]
    l_part = functools.reduce(jnp.add, [_lane_tile_reduce(pc, jnp.add) for pc in p])
    acc = functools.reduce(jnp.add, [_weighted(pc, v_ref, krows) for pc, krows in zip(p, chunks)])
    return acc * (1.0 / jnp.sum(l_part, axis=1, keepdims=True))


def _attend_pipelined(pairs_next, s_next_ref, s_cur_ref, finish_cur, k_ref, v_ref, m_ref, key_rows, tk):
    n_chunks = key_rows // tk
    have_cur = s_cur_ref is not None
    dv = v_ref.shape[1]
    mxu_sums = dv == HEAD_DIM
    chunks = [slice(c * tk, (c + 1) * tk) for c in range(n_chunks)]
    acc = None
    for c, krows in enumerate(chunks if have_cur else ()):
        p = jnp.concatenate(
            [jnp.exp2(s_cur_ref[c, :, t * HEAD_DIM:(t + 1) * HEAD_DIM] - m_ref[0])
             for t in range(tk // HEAD_DIM)], axis=1)
        if not mxu_sums:
            part = _lane_tile_reduce(p, jnp.add)
            m_ref[1] = part if c == 0 else m_ref[1] + part
        pv = _weighted(p, v_ref, krows, with_sums=mxu_sums)
        acc = pv if acc is None else acc + pv
    if have_cur:
        if mxu_sums:
            finish_cur(acc[:, :dv] * (1.0 / acc[:, dv:]))
        else:
            finish_cur(acc * (1.0 / jnp.sum(m_ref[1], axis=1, keepdims=True)))
    m_part = None
    for c, krows in enumerate(chunks if pairs_next is not None else ()):
        s = _scores(pairs_next, k_ref, krows)
        s_next_ref[c] = s
        part = _lane_tile_reduce(s, jnp.maximum)
        m_part = part if m_part is None else jnp.maximum(m_part, part)
    if pairs_next is not None:
        m_ref[0] = jnp.broadcast_to(jnp.max(m_part, axis=1, keepdims=True), m_ref.shape[1:])


def _attention_steps(step, queries, finish, k_ref, v_ref, o_ref, s_ref, m_ref, *, tq, tk, ctx_len, rows,
                     ctx_out):
    n_lat = (rows - ctx_len) // tq

    def lat_row0(t):
        return pl.multiple_of(ctx_len + (t - 1) * tq, math.gcd(ctx_len, tq))

    def pipe(t_next, t_cur):
        pairs = None if t_next is None else queries(lat_row0(t_next), tq)
        _attend_pipelined(pairs, None if t_next is None else s_ref,
                          None if t_cur is None else s_ref,
                          None if t_cur is None else functools.partial(finish, lat_row0(t_cur), tq),
                          k_ref, v_ref, m_ref, rows, tk)

    @pl.when(step == 0)
    def _():
        if ctx_out:
            finish(0, ctx_len, _attend_in_place(queries(0, ctx_len), k_ref, v_ref, ctx_len, tk))
        else:
            o_ref[:ctx_len, :] = jnp.zeros((ctx_len, o_ref.shape[1]), o_ref.dtype)
        pipe(1, None)

    if n_lat > 1:
        pl.when((step >= 1) & (step < n_lat))(functools.partial(pipe, step + 1, step))
    pl.when(step == n_lat)(functools.partial(pipe, None, step))


def _gqa_kernel(q_ref, k_ref, v_ref, z_ref, o_ref, s_ref, m_ref, *, tq, tk, group, ctx_len, rows,
                ctx_out):
    def queries(row0, nq):
        q = jnp.concatenate([q_ref[pl.ds(row0, nq), g * HEAD_DIM:(g + 1) * HEAD_DIM]
                             for g in range(group)], axis=0)
        return [(q, slice(0, HEAD_DIM))]

    def finish(row0, nq, o):
        qrows = pl.ds(row0, nq)
        for g in range(group):
            cols = slice(g * HEAD_DIM, (g + 1) * HEAD_DIM)
            z = z_ref[qrows, cols].astype(F32)
            o_ref[qrows, cols] = (o[g * nq:(g + 1) * nq] * _silu(z)).astype(o_ref.dtype)

    _attention_steps(pl.program_id(2), queries, finish, k_ref, v_ref, o_ref, s_ref, m_ref,
                     tq=tq, tk=tk, ctx_len=ctx_len, rows=rows, ctx_out=ctx_out)


def _gqa_attention(proj, dm, *, tq, tk, ctx_out):
    rows = _rows(dm)
    m = proj.shape[0]
    width = dm.d
    heads = width // HEAD_DIM
    group = heads // GQA_KV_HEADS
    gw = group * HEAD_DIM
    k_col0 = width // HEAD_DIM
    v_col0 = k_col0 + GQA_KV_HEADS
    z_col0 = (width + 2 * GQA_KV_HEADS * HEAD_DIM) // gw
    kern = functools.partial(_gqa_kernel, tq=tq, tk=tk, group=group, ctx_len=dm.ctx, rows=rows,
                             ctx_out=ctx_out)
    return pl.pallas_call(
        kern,
        grid=(dm.batch, GQA_KV_HEADS, 1 + dm.seq // tq),
        in_specs=[pl.BlockSpec((rows, gw), lambda b, h, i: (b, h)),
                  pl.BlockSpec((rows, HEAD_DIM), lambda b, h, i: (b, k_col0 + h)),
                  pl.BlockSpec((rows, HEAD_DIM), lambda b, h, i: (b, v_col0 + h)),
                  pl.BlockSpec((rows, gw), lambda b, h, i: (b, z_col0 + h))],
        out_specs=pl.BlockSpec((rows, gw), lambda b, h, i: (b, h)),
        out_shape=jax.ShapeDtypeStruct((m, width), BF16),
        scratch_shapes=[pltpu.VMEM((rows // tk, group * tq, tk), F32),
                        pltpu.VMEM((2, group * tq, HEAD_DIM), F32)],
        compiler_params=_params(("parallel", "parallel", "arbitrary")),
        name="gqa_attention",
    )(proj, proj, proj, proj)


def _diff_kernel(lam_ref, q_ref, k_ref, v_ref, z_ref, subln_ref, o_ref, s_ref, m_ref,
                 *, tq, tk, ctx_len, rows, lambda_init):
    lv = lam_ref[...]
    lam = (jnp.exp(jnp.sum(lv[0:1] * lv[1:2], axis=1, keepdims=True))
           - jnp.exp(jnp.sum(lv[2:3] * lv[3:4], axis=1, keepdims=True)) + lambda_init)

    def queries(row0, nq):
        halves = [slice(t * HEAD_DIM, (t + 1) * HEAD_DIM) for t in range(2)]
        return [(q_ref[pl.ds(row0, nq), cols], cols) for cols in halves]

    def finish(row0, nq, o):
        qrows = pl.ds(row0, nq)
        o = o[:nq] - lam * o[nq:]
        o = o * lax.rsqrt(jnp.mean(o * o, axis=-1, keepdims=True) + EPS) * subln_ref[...]
        o = o * (1.0 - lambda_init)
        o_ref[qrows, :] = (o * _silu(z_ref[qrows, :].astype(F32))).astype(o_ref.dtype)

    _attention_steps(pl.program_id(2), queries, finish, k_ref, v_ref, o_ref, s_ref, m_ref,
                     tq=tq, tk=tk, ctx_len=ctx_len, rows=rows, ctx_out=True)


def _diff_attention(proj, lam_vecs, subln, dm, *, tq, tk, lambda_init):
    rows = _rows(dm)
    m = proj.shape[0]
    width = dm.d
    hw = 2 * HEAD_DIM
    heads = width // hw
    kern = functools.partial(_diff_kernel, tq=tq, tk=tk, ctx_len=dm.ctx, rows=rows, lambda_init=lambda_init)

    def head_block(first):
        return pl.BlockSpec((rows, hw), lambda b, h, i: (b, first + h))

    return pl.pallas_call(
        kern,
        grid=(dm.batch, heads, 1 + dm.seq // tq),
        in_specs=[pl.BlockSpec(lam_vecs.shape, lambda b, h, i: (0, 0)),
                  head_block(0), head_block(heads), head_block(2 * heads), head_block(3 * heads),
                  pl.BlockSpec((1, hw), lambda b, h, i: (0, 0))],
        out_specs=head_block(0),
        out_shape=jax.ShapeDtypeStruct((m, width), BF16),
        scratch_shapes=[pltpu.VMEM((rows // tk, 2 * tq, tk), F32),
                        pltpu.VMEM((2, 2 * tq, HEAD_DIM), F32)],
        compiler_params=_params(("parallel", "parallel", "arbitrary")),
        name="diff_attention",
    )(lam_vecs, proj, proj, proj, proj, subln.reshape(1, hw))


CONV_TAPS = (-1, 0, 1, 2)
CONV_STRIP = 256


def _conv_kernel(x_ref, shift_ref, w_ref, b_ref, o_ref, *, tr, ctx_tiles):
    edge = BF16_SUBLANES
    centre = CONV_TAPS.index(0)
    n_tiles = x_ref.shape[0] // tr
    row = lax.broadcasted_iota(jnp.int32, (edge, 1), 0)
    for c0 in range(0, x_ref.shape[1], CONV_STRIP):
        cols = slice(c0, c0 + CONV_STRIP)
        w = [w_ref[t:t + 1, cols] for t in range(SSM_CONV)]
        for r in range(n_tiles):
            r0 = r * tr
            xb = x_ref[r0:r0 + tr, cols]
            y = b_ref[:, cols] + w[centre] * xb.astype(F32)
            for k, tap in enumerate(t for t in range(SSM_CONV) if t != centre):
                y = y + w[tap] * jnp.dot(shift_ref[k], xb, preferred_element_type=F32)
            o_ref[r0 + edge:r0 + tr - edge, cols] = _silu(y[edge:tr - edge]).astype(o_ref.dtype)
            top, bottom = y[:edge], y[tr - edge:]
            if r not in (0, ctx_tiles):
                prev = x_ref[r0 - 1:r0, cols].astype(F32)
                top = top + jnp.where(row == 0, w[0] * prev, 0.0)
            if r not in (ctx_tiles - 1, n_tiles - 1):
                nxt0 = x_ref[r0 + tr:r0 + tr + 1, cols].astype(F32)
                nxt1 = x_ref[r0 + tr + 1:r0 + tr + 2, cols].astype(F32)
                bottom = (bottom + jnp.where(row == edge - 2, w[3] * nxt0, 0.0)
                          + jnp.where(row == edge - 1, w[2] * nxt0 + w[3] * nxt1, 0.0))
            o_ref[r0:r0 + edge, cols] = _silu(top).astype(o_ref.dtype)
            o_ref[r0 + tr - edge:r0 + tr, cols] = _silu(bottom).astype(o_ref.dtype)


def _ssd_conv(proj, conv_w, conv_b, dm, *, col0, tr, tn):
    rows = _rows(dm)
    m = proj.shape[0]
    n = conv_w.shape[1]
    c0 = col0 // tn
    shifts = jnp.stack([jnp.eye(tr, k=off, dtype=BF16) for off in CONV_TAPS if off != 0])
    kern = functools.partial(_conv_kernel, tr=tr, ctx_tiles=dm.ctx // tr)
    return pl.pallas_call(
        kern,
        grid=(dm.batch, n // tn),
        in_specs=[pl.BlockSpec((rows, tn), lambda b, j: (b, c0 + j)),
                  pl.BlockSpec(shifts.shape, lambda b, j: (0, 0, 0)),
                  pl.BlockSpec((SSM_CONV, tn), lambda b, j: (0, j)),
                  pl.BlockSpec((1, tn), lambda b, j: (0, j))],
        out_specs=pl.BlockSpec((rows, tn), lambda b, j: (b, j)),
        out_shape=jax.ShapeDtypeStruct((m, n), BF16),
        compiler_params=_params(("parallel", "parallel")),
        name="ssd_conv",
    )(proj, shifts, conv_w, conv_b.reshape(1, n))


def _split_dot(a, b_hi_exact, dims):
    hi = a.astype(BF16)
    lo = (a - hi.astype(F32)).astype(BF16)
    if dims == "ab":
        return (jnp.dot(hi, b_hi_exact, preferred_element_type=F32)
                + jnp.dot(lo, b_hi_exact, preferred_element_type=F32))
    return (jnp.dot(b_hi_exact, hi, preferred_element_type=F32)
            + jnp.dot(b_hi_exact, lo, preferred_element_type=F32))


def _ssd_kernel(*refs, heads, reverse, finish):
    it = iter(refs)
    (x_ref, b_ref, c_ref, dt_ref, dtT_ref, bias_ref, biasT_ref, alog_ref, alogT_ref,
     expand_ref) = (next(it) for _ in range(10))
    other_ref = z_ref = dskip_ref = gain_ref = None
    if finish:
        other_ref, z_ref, dskip_ref, gain_ref = (next(it) for _ in range(4))
    y_ref, state_ref = next(it), next(it)
    q = SSM_CHUNK
    hpg = heads // SSM_GROUPS
    gw = hpg * SSM_HEAD_DIM
    pw = 2 * SSM_HEAD_DIM
    step = pl.program_id(1)

    @pl.when(step == 0)
    def _():
        state_ref[...] = jnp.zeros(state_ref.shape, F32)

    def softplus(v):
        return jnp.maximum(v, 0.0) + jnp.log(1.0 + jnp.exp(-jnp.abs(v)))

    def chunk(rs):
        dcol = heads if reverse else 0
        dt = softplus(dt_ref[rs, dcol:dcol + heads] + bias_ref[...])
        dtT = softplus(dtT_ref[:, rs] + biasT_ref[...])
        a = dt * -jnp.exp(alog_ref[...])
        aT = dtT * -jnp.exp(alogT_ref[...])
        ri = lax.broadcasted_iota(jnp.int32, (q, q), 0)
        ci = lax.broadcasted_iota(jnp.int32, (q, q), 1)
        before = (ci >= ri) if reverse else (ci <= ri)
        tri = before.astype(BF16)
        triT = ((ri >= ci) if reverse else (ri <= ci)).astype(BF16)
        cum = _split_dot(a, tri, "ba")
        cumT = _split_dot(aT, triT, "ab")
        end = 0 if reverse else q - 1
        total = cum[end:end + 1, :]
        expand = expand_ref[...]
        w_full = jnp.dot((jnp.exp(total - cum) * dt).astype(BF16), expand,
                         preferred_element_type=F32)
        e_full = jnp.dot(jnp.exp(cum).astype(BF16), expand, preferred_element_type=F32)
        cd_full = _split_dot(jnp.broadcast_to(jnp.exp(total), (8, heads)), expand, "ab")[0:1]
        log2e = math.log2(math.e)
        cum2 = cum * log2e
        lcumT2 = (cumT - jnp.log(dtT)) * log2e
        low = lax.broadcasted_iota(jnp.int32, (q, pw), 1) < SSM_HEAD_DIM

        for g in range(SSM_GROUPS):
            gcols = slice(g * gw, (g + 1) * gw)
            ncols = slice(g * SSM_STATE, (g + 1) * SSM_STATE)
            bg = b_ref[rs, ncols]
            cg = c_ref[rs, ncols]
            cb = lax.dot_general(cg, bg, (((1,), (1,)), ((), ())), preferred_element_type=F32)
            st = state_ref[:, gcols]
            y_off = jnp.dot(cg, st.astype(BF16), preferred_element_type=F32) * e_full[:, gcols]
            ys = []
            for j in range(hpg // 2):
                xp = x_ref[rs, g * gw + j * pw:g * gw + (j + 1) * pw].astype(F32)
                xbd = jnp.concatenate([jnp.where(low, xp, 0.0), jnp.where(low, 0.0, xp)], axis=0).astype(BF16)
                mats = []
                for hh in range(2):
                    h = g * hpg + 2 * j + hh
                    diff = cum2[:, h:h + 1] - lcumT2[h:h + 1, :]
                    mats.append((cb * jnp.exp2(jnp.where(before, diff, -jnp.inf))).astype(BF16))
                ys.append(jnp.dot(jnp.concatenate(mats, axis=1), xbd, preferred_element_type=F32)
                          + y_off[:, j * pw:(j + 1) * pw])
            xg = x_ref[rs, gcols].astype(F32)
            if finish:
                y = jnp.concatenate(ys, axis=1) + other_ref[rs, gcols].astype(F32) + dskip_ref[:, gcols] * xg
                v = y * _silu(z_ref[rs, gcols].astype(F32))
                v = v * lax.rsqrt(jnp.mean(v * v, axis=-1, keepdims=True) + EPS) * gain_ref[:, gcols]
                y_ref[rs, gcols] = v.astype(y_ref.dtype)
            else:
                for j, y in enumerate(ys):
                    y_ref[rs, g * gw + j * pw:g * gw + (j + 1) * pw] = y.astype(y_ref.dtype)
            xw = (xg * w_full[:, gcols]).astype(BF16)
            bgT = bg.astype(F32).T.astype(BF16)
            state_ref[:, gcols] = st * cd_full[:, gcols] + jnp.dot(bgT, xw, preferred_element_type=F32)

    order = range(x_ref.shape[0] // q)
    for k in (reversed(order) if reverse else order):
        chunk(slice(k * q, (k + 1) * q))


def _ssd_scan(xbc, dt, dtT, dt_bias, a_log, dm, *, inner, reverse, tail=None):
    rows = _rows(dm)
    m = xbc.shape[0]
    q = SSM_CHUNKS_PER_STEP * SSM_CHUNK
    heads = inner // SSM_HEAD_DIM
    nc = rows // q
    ctx_chunks = dm.ctx // q
    sw = SSM_GROUPS * SSM_STATE
    d = 1 if reverse else 0

    def chunk(b, s):
        if reverse:
            s = jnp.where(s < ctx_chunks, ctx_chunks - 1 - s, nc - 1 - (s - ctx_chunks))
        return b * nc + s

    expand = jnp.repeat(jnp.eye(heads, dtype=BF16), SSM_HEAD_DIM, axis=1)
    row_spec = pl.BlockSpec((q, inner), lambda b, s: (chunk(b, s), 0))
    vec_spec = pl.BlockSpec((1, inner), lambda b, s: (0, 0))
    args = [xbc, xbc, xbc, dt, dtT, dt_bias[d][None, :], dt_bias[d][:, None], a_log[d][None, :],
            a_log[d][:, None], expand]
    in_specs = [row_spec,
                pl.BlockSpec((q, sw), lambda b, s: (chunk(b, s), inner // sw)),
                pl.BlockSpec((q, sw), lambda b, s: (chunk(b, s), inner // sw + 1)),
                pl.BlockSpec((q, 2 * heads), lambda b, s: (chunk(b, s), 0)),
                pl.BlockSpec((heads, q), lambda b, s: (d, chunk(b, s))),
                pl.BlockSpec((1, heads), lambda b, s: (0, 0)),
                pl.BlockSpec((heads, 1), lambda b, s: (0, 0)),
                pl.BlockSpec((1, heads), lambda b, s: (0, 0)),
                pl.BlockSpec((heads, 1), lambda b, s: (0, 0)),
                pl.BlockSpec((heads, inner), lambda b, s: (0, 0))]
    if tail is not None:
        y_other, proj, d_skip, gain = tail
        args += [y_other, proj, jnp.repeat(d_skip.astype(F32), SSM_HEAD_DIM).reshape(1, inner),
                 gain.reshape(1, inner)]
        in_specs += [row_spec, row_spec, vec_spec, vec_spec]
    kern = functools.partial(_ssd_kernel, heads=heads, reverse=reverse, finish=tail is not None)
    return pl.pallas_call(
        kern,
        grid=(dm.batch, nc),
        in_specs=in_specs,
        out_specs=row_spec,
        out_shape=jax.ShapeDtypeStruct((m, inner), BF16),
        scratch_shapes=[pltpu.VMEM((SSM_STATE, inner), F32)],
        compiler_params=_params(("parallel", "arbitrary")),
        name="ssd_scan_bwd" if reverse else "ssd_scan_fwd",
    )(*args)


def _final_norm_kernel(x_ref, g_ref, o_ref):
    x = x_ref[...]
    o_ref[...] = x * lax.rsqrt(jnp.mean(x * x, axis=-1, keepdims=True) + EPS) * g_ref[...]


def _final_norm(xc, gain, dm, *, tr):
    rows = _rows(dm)
    tpb = rows // tr
    lat_tiles = dm.seq // tr
    ctx_tiles = dm.ctx // tr
    return pl.pallas_call(
        _final_norm_kernel,
        grid=(dm.batch, lat_tiles),
        in_specs=[pl.BlockSpec((tr, dm.d), lambda b, i: (b * tpb + ctx_tiles + i, 0)),
                  pl.BlockSpec((1, dm.d), lambda b, i: (0, 0))],
        out_specs=pl.BlockSpec((tr, dm.d), lambda b, i: (b * lat_tiles + i, 0)),
        out_shape=jax.ShapeDtypeStruct((dm.batch * dm.seq, dm.d), F32),
        compiler_params=_params(("parallel", "parallel")),
        name="final_norm",
    )(xc, gain.reshape(1, dm.d))


def _rope_tables(dm):
    d_axis = HEAD_DIM // 2
    t = jnp.arange(dm.seq, dtype=jnp.int32)
    inv_freq = jnp.power(ROPE_THETA, -jnp.arange(0, d_axis, 2, dtype=F32) / d_axis)
    ang_r = (t // GRID_W).astype(F32)[:, None] * inv_freq[None, :]
    ang_c = (t % GRID_W).astype(F32)[:, None] * inv_freq[None, :]
    cos = jnp.concatenate([jnp.cos(ang_r), jnp.cos(ang_c), jnp.cos(ang_r), jnp.cos(ang_c)], axis=1)
    sin = jnp.concatenate([-jnp.sin(ang_r), -jnp.sin(ang_c), jnp.sin(ang_r), jnp.sin(ang_c)], axis=1)
    cos = jnp.concatenate([jnp.ones((dm.ctx, HEAD_DIM), F32), cos], axis=0)
    sin = jnp.concatenate([jnp.zeros((dm.ctx, HEAD_DIM), F32), sin], axis=0)
    return cos, sin


def _relayout_kernel(w_ref, o_ref):
    tn = w_ref.shape[1]
    quarter = HEAD_DIM // 4
    r = lax.broadcasted_iota(jnp.int32, (tn, tn), 0)
    c = lax.broadcasted_iota(jnp.int32, (tn, tn), 1)
    blk = (c // quarter) % 4
    src_blk = jnp.where(blk == 1, 2, jnp.where(blk == 2, 1, blk))
    src = (c // HEAD_DIM) * HEAD_DIM + src_blk * quarter + c % quarter
    perm = (r == src).astype(BF16)
    o_ref[...] = jnp.dot(w_ref[...].astype(BF16), perm, preferred_element_type=F32).astype(BF16)


def _qk_weights(w_in, qk_cols, tn):
    d = w_in.shape[0]
    return pl.pallas_call(
        _relayout_kernel,
        grid=(qk_cols // tn,),
        in_specs=[pl.BlockSpec((d, tn), lambda j: (0, j))],
        out_specs=pl.BlockSpec((d, tn), lambda j: (0, j)),
        out_shape=jax.ShapeDtypeStruct((d, qk_cols), BF16),
        compiler_params=_params(("parallel",)),
        name="qk_relayout",
    )(w_in)


def _lambda_init(layer_idx):
    return 0.8 - 0.6 * math.exp(-0.3 * layer_idx)


def _tiles(dm):
    rows = _rows(dm)
    tm = rows // 2 if (rows // 2) % 128 == 0 else rows
    return dict(tm=tm, tn=min(512, dm.d), tq=dm.ctx, tk=dm.ctx)


def _forward(dm, x, c, ctx, c_ctx, layers, final_norm):
    d = dm.d
    til = _tiles(dm)
    tm, tn, tq, tk = til["tm"], til["tn"], til["tq"], til["tk"]
    scale = HEAD_DIM ** -0.5 * math.log2(math.e)
    xc =jnp.concatenate([ctx, x], axis=1).reshape(dm.batch * _rows(dm), d)
    cvec = jnp.concatenate([c, c_ctx[None, :], jnp.zeros((MOD_ROWS - dm.batch - 1, d), F32)], axis=0)
    rope = _rope_tables(dm)
    q_tiles = d // tn

    for i, (kind, (w_mod, b_mod, norm), p) in enumerate(layers):
        mods = _modulation(cvec, w_mod, b_mod)
        if kind == "gqa":
            w_in, q_gain, k_gain, w_out = p
            kv_tiles = GQA_KV_HEADS * HEAD_DIM // tn
            n_tiles = w_in.shape[1] // tn
            segs = ((0, q_tiles, 0, True, scale),
                    (q_tiles, q_tiles + kv_tiles, 1, True, None),
                    (q_tiles + kv_tiles, n_tiles, None, False, None))
            w_qk = _qk_weights(w_in, (q_tiles + kv_tiles) * tn, tn)
            gains = jnp.stack([q_gain, k_gain]).astype(F32)[:, _rope_layout()]
            proj = _in_proj(xc, norm, mods, w_in, dm, tm=tm, tn=tn, segs=segs, rope=rope, w_qk=w_qk,
                            gains=gains)
            g = _gqa_attention(proj, dm, tq=min(2 * tq, dm.seq), tk=tk, ctx_out=i < len(layers) - 1)
        elif kind == "diff":
            w_in, lq1, lk1, lq2, lk2, subln, w_out = p
            n_tiles = w_in.shape[1] // tn
            segs = ((0, q_tiles, None, True, scale),
                    (q_tiles, 2 * q_tiles, None, True, None),
                    (2 * q_tiles, n_tiles, None, False, None))
            proj = _in_proj(xc, norm, mods, w_in, dm, tm=tm, tn=tn, segs=segs, rope=rope,
                            w_qk=_qk_weights(w_in, 2 * q_tiles * tn, tn))
            lam_vecs = jnp.stack([lq1, lk1, lq2, lk2]).astype(F32)
            g = _diff_attention(proj, lam_vecs, subln, dm, tq=min(4 * tq, dm.seq), tk=tk,
                                lambda_init=_lambda_init(i))
        else:
            w_in, conv_w, conv_b, dt_bias, a_log, d_skip, ssm_norm, w_out = p
            inner = w_out.shape[0]
            n_main = 2 * inner + 2 * SSM_GROUPS * SSM_STATE
            tn_ssd = math.gcd(n_main, 2 * tn)
            segs = ((0, n_main // tn_ssd, None, False, None),)
            proj, dt, dtT = _in_proj(xc, norm, mods, w_in, dm, tm=tm, tn=tn_ssd,
                                     segs=segs, w_dt=w_in[:, n_main:].astype(BF16))
            xbc = _ssd_conv(proj, conv_w, conv_b, dm, col0=inner, tr=tq,
                            tn=math.gcd(inner, conv_w.shape[1]))
            y_f = _ssd_scan(xbc, dt, dtT, dt_bias, a_log, dm, inner=inner, reverse=False)
            g = _ssd_scan(xbc, dt, dtT, dt_bias, a_log, dm, inner=inner, reverse=True,
                          tail=(y_f, proj, d_skip, ssm_norm))
        wide = w_out.shape[0] > d
        xc = _out_proj(g, w_out, xc, mods, dm, tm=tm if wide else _rows(dm), tn=tn)

    out = _final_norm(xc, final_norm, dm, tr=tq)
    return out.reshape(dm.batch, dm.seq, d)


def kernel(x, c, ctx, c_ctx, l0_w_mod, l0_b_mod, l0_norm, l0_w_in, l0_q_gain, l0_k_gain, l0_w_out, l1_w_mod, l1_b_mod, l1_norm, l1_w_in, l1_lambda_q1, l1_lambda_k1, l1_lambda_q2, l1_lambda_k2, l1_subln, l1_w_out, l2_w_mod, l2_b_mod, l2_norm, l2_w_in, l2_conv_w, l2_conv_b, l2_dt_bias, l2_A_log, l2_D, l2_ssm_norm, l2_w_out, l3_w_mod, l3_b_mod, l3_norm, l3_w_in, l3_q_gain, l3_k_gain, l3_w_out, final_norm):
    dm = Dims(batch=x.shape[0], seq=x.shape[1], ctx=ctx.shape[1], d=x.shape[2])
    layers = (
        ("gqa", (l0_w_mod, l0_b_mod, l0_norm), (l0_w_in, l0_q_gain, l0_k_gain, l0_w_out)),
        ("diff", (l1_w_mod, l1_b_mod, l1_norm), (l1_w_in, l1_lambda_q1, l1_lambda_k1, l1_lambda_q2,
                                                 l1_lambda_k2, l1_subln, l1_w_out)),
        ("ssd", (l2_w_mod, l2_b_mod, l2_norm), (l2_w_in, l2_conv_w, l2_conv_b, l2_dt_bias, l2_A_log,
                                                l2_D, l2_ssm_norm, l2_w_out)),
        ("gqa", (l3_w_mod, l3_b_mod, l3_norm), (l3_w_in, l3_q_gain, l3_k_gain, l3_w_out)),
    )
    return _forward(dm, x, c, ctx, c_ctx, layers, final_norm)
```

```python
import collections
import functools
import math

import jax
import jax.numpy as jnp
from jax import lax
from jax.experimental import pallas as pl
from jax.experimental.pallas import tpu as pltpu

F32 = jnp.float32
BF16 = jnp.bfloat16

EPS = 1e-6
ROPE_THETA = 10000.0
HEAD_DIM = 128
GQA_KV_HEADS = 4
SSM_HEAD_DIM = 64
SSM_GROUPS = 8
SSM_STATE = 128
SSM_CONV = 4
SSM_CHUNK = 128
SSM_CHUNKS_PER_STEP = 2
GRID_W = 64
N_MODS = 3
MOD_ROWS = 8
MOD_TILE = 1024
ROW_PARTS = 8
NORM_SLAB = 16
BF16_SUBLANES = 16
VMEM_LIMIT = 56 * 1024 * 1024

Dims = collections.namedtuple("Dims", "batch seq ctx d")


def _rows(dm):
    return dm.ctx + dm.seq


def _silu(v):
    return v * (1.0 / (1.0 + jnp.exp2(v * -math.log2(math.e))))


def _params(semantics):
    return pltpu.CompilerParams(dimension_semantics=semantics, vmem_limit_bytes=VMEM_LIMIT)


def _mod_kernel(c_ref, w_ref, b_ref, o_ref):
    s = _silu(c_ref[...]).astype(BF16)
    o_ref[...] = jnp.dot(s, w_ref[...].astype(BF16), preferred_element_type=F32) + b_ref[...]


def _modulation(cvec, w_mod, b_mod):
    d, n = w_mod.shape
    tn = math.gcd(n, MOD_TILE)
    return pl.pallas_call(
        _mod_kernel,
        grid=(n // tn,),
        in_specs=[pl.BlockSpec((MOD_ROWS, d), lambda j: (0, 0)),
                  pl.BlockSpec((d, tn), lambda j: (0, j)),
                  pl.BlockSpec((1, tn), lambda j: (0, j))],
        out_specs=pl.BlockSpec((MOD_ROWS, tn), lambda j: (0, j)),
        out_shape=jax.ShapeDtypeStruct((MOD_ROWS, n), F32),
        compiler_params=_params(("arbitrary",)),
        name="adaln_mod",
    )(cvec, w_mod, b_mod.reshape(1, n))


def _rope_layout():
    quarter = HEAD_DIM // 4
    idx = jnp.arange(HEAD_DIM).reshape(4, quarter)
    return idx[jnp.array([0, 2, 1, 3])].reshape(HEAD_DIM)


def _in_proj_kernel(*refs, tm, tn, tiles_per_batch, ctx_len, segs, has_rope, has_gain, has_dt):
    it = iter(refs)
    x_ref, g_ref, shl_ref, scl_ref, shc_ref, scc_ref, w_ref = (next(it) for _ in range(7))
    cos_ref = sin_ref = wqk_ref = gain_ref = wdt_ref = wdtT_ref = dt_ref = dtT_ref = None
    if has_rope:
        cos_ref, sin_ref, wqk_ref = next(it), next(it), next(it)
    if has_gain:
        gain_ref = next(it)
    if has_dt:
        wdt_ref, wdtT_ref = next(it), next(it)
    o_ref = next(it)
    if has_dt:
        dt_ref, dtT_ref = next(it), next(it)
    h_ref, mul_ref, add_ref = next(it), next(it), next(it)

    i = pl.program_id(0)
    j = pl.program_id(1)

    part = tm // ROW_PARTS
    starts = range(0, tm, part)

    def modulation_tables():
        d = g_ref.shape[1]
        g = g_ref[...]
        for k, (sc_ref, sh_ref) in enumerate(((scl_ref, shl_ref), (scc_ref, shc_ref))):
            mul_ref[k] = jnp.broadcast_to(g * (1.0 + sc_ref[0]), (NORM_SLAB, d))
            add_ref[k] = jnp.broadcast_to(sh_ref[0], (NORM_SLAB, d))

    def normalise(r0):
        for s0 in range(r0, r0 + part, NORM_SLAB):
            rows = slice(s0, s0 + NORM_SLAB)
            x = x_ref[rows, :]
            y = x * lax.rsqrt(jnp.mean(x * x, axis=-1, keepdims=True) + EPS)
            k = ((i % tiles_per_batch) * tm + s0 < ctx_len).astype(jnp.int32)
            h_ref[rows, :] = (y * mul_ref[k] + add_ref[k]).astype(BF16)

    def tile(gain_idx, rope, scale, first):
        w = wqk_ref[...] if rope else w_ref[...].astype(BF16)
        plain = gain_idx is None and not rope and scale is None
        if plain and not first:
            o_ref[...] = jnp.dot(h_ref[...], w, preferred_element_type=F32).astype(o_ref.dtype)
            return
        if first:
            modulation_tables()
        accs = []
        for r0 in starts:
            if first:
                normalise(r0)
            accs.append(jnp.dot(h_ref[r0:r0 + part, :], w, preferred_element_type=F32))
        for r0, acc in zip(starts, accs):
            rows = slice(r0, r0 + part)
            if plain:
                o_ref[rows, :] = acc.astype(o_ref.dtype)
                continue
            for hh in range(tn // HEAD_DIM):
                cols = slice(hh * HEAD_DIM, (hh + 1) * HEAD_DIM)
                v = acc[:, cols]
                if gain_idx is not None:
                    v = v * lax.rsqrt(jnp.mean(v * v, axis=-1, keepdims=True) + EPS)
                    v = v * gain_ref[gain_idx:gain_idx + 1, :]
                if rope:
                    v = v * cos_ref[rows, :] + pltpu.roll(v, HEAD_DIM // 2, 1) * sin_ref[rows, :]
                if scale is not None:
                    v = v * scale
                o_ref[rows, cols] = v.astype(o_ref.dtype)
        if first and has_dt:
            h = h_ref[...]
            dt_ref[...] = jnp.dot(h, wdt_ref[...], preferred_element_type=F32)
            dtT_ref[...] = lax.dot_general(wdtT_ref[...], h, (((1,), (1,)), ((), ())),
                                           preferred_element_type=F32)

    for (lo, hi, gain_idx, rope, scale) in segs:
        if lo == 0:
            pl.when(j == 0)(functools.partial(tile, gain_idx, rope, scale, True))
            lo = 1
        pl.when((j >= lo) & (j < hi))(functools.partial(tile, gain_idx, rope, scale, False))


def _in_proj(xc, norm_g, mods, w, dm, *, tm, tn, segs, rope=None, w_qk=None, gains=None, w_dt=None):
    m, d = xc.shape
    n = segs[-1][1] * tn
    tpb = _rows(dm) // tm
    mods3 = mods.reshape(MOD_ROWS, 1, N_MODS * d)
    has_rope, has_gain, has_dt = rope is not None, gains is not None, w_dt is not None
    qk_tiles = w_qk.shape[1] // tn if has_rope else 0

    def lat(col):
        return pl.BlockSpec((1, 1, d), lambda i, j: (i // tpb, 0, col))

    def ctx(col):
        return pl.BlockSpec((1, 1, d), lambda i, j: (dm.batch, 0, col))

    args = [xc, norm_g.reshape(1, d), mods3, mods3, mods3, mods3, w]
    in_specs = [pl.BlockSpec((tm, d), lambda i, j: (i, 0)),
                pl.BlockSpec((1, d), lambda i, j: (0, 0)),
                lat(0), lat(1), ctx(0), ctx(1),
                pl.BlockSpec((d, tn), lambda i, j: (0, jnp.maximum(j, qk_tiles)))]
    if has_rope:
        args += list(rope) + [w_qk]
        in_specs += [pl.BlockSpec((tm, HEAD_DIM), lambda i, j: (i % tpb, 0))] * 2
        in_specs.append(pl.BlockSpec((d, tn), lambda i, j: (0, jnp.minimum(j, qk_tiles - 1))))
    if has_gain:
        args.append(gains)
        in_specs.append(pl.BlockSpec(gains.shape, lambda i, j: (0, 0)))
    out_shape = [jax.ShapeDtypeStruct((m, n), BF16)]
    out_specs = [pl.BlockSpec((tm, tn), lambda i, j: (i, j))]
    if has_dt:
        n_dt = w_dt.shape[1]
        args += [w_dt, w_dt.T]
        in_specs += [pl.BlockSpec((d, n_dt), lambda i, j: (0, 0)),
                     pl.BlockSpec((n_dt, d), lambda i, j: (0, 0))]
        out_shape += [jax.ShapeDtypeStruct((m, n_dt), F32), jax.ShapeDtypeStruct((n_dt, m), F32)]
        out_specs += [pl.BlockSpec((tm, n_dt), lambda i, j: (i, 0)),
                      pl.BlockSpec((n_dt, tm), lambda i, j: (0, i))]
    kern = functools.partial(_in_proj_kernel, tm=tm, tn=tn, tiles_per_batch=tpb, ctx_len=dm.ctx,
                             segs=segs, has_rope=has_rope, has_gain=has_gain, has_dt=has_dt)
    out = pl.pallas_call(
        kern,
        grid=(m // tm, n // tn),
        in_specs=in_specs,
        out_specs=out_specs,
        out_shape=out_shape,
        scratch_shapes=[pltpu.VMEM((tm, d), BF16),
                        pltpu.VMEM((2, NORM_SLAB, d), F32),
                        pltpu.VMEM((2, NORM_SLAB, d), F32)],
        compiler_params=_params(("parallel", "arbitrary")),
        name="in_proj",
    )(*args)
    return out if has_dt else out[0]


def _out_proj_kernel(g_ref, w_ref, x_ref, gl_ref, gc_ref, o_ref, *, tm, tiles_per_batch, ctx_len):
    i = pl.program_id(0)
    acc = jnp.dot(g_ref[...], w_ref[...].astype(BF16), preferred_element_type=F32)
    row = (i % tiles_per_batch) * tm + lax.broadcasted_iota(jnp.int32, (tm, 1), 0)
    gate = jnp.where(row < ctx_len, gc_ref[0], gl_ref[0])
    o_ref[...] = x_ref[...] + gate * acc


def _out_proj(g, w, xc, mods, dm, *, tm, tn):
    m, k = g.shape
    d = w.shape[1]
    tpb = _rows(dm) // tm
    tiles_n = d // tn
    mods3 = mods.reshape(MOD_ROWS, 1, N_MODS * d)
    gate_col = 2 * tiles_n
    kern = functools.partial(_out_proj_kernel, tm=tm, tiles_per_batch=tpb, ctx_len=dm.ctx)
    return pl.pallas_call(
        kern,
        grid=(m // tm, tiles_n),
        in_specs=[pl.BlockSpec((tm, k), lambda i, j: (i, 0)),
                  pl.BlockSpec((k, tn), lambda i, j: (0, j)),
                  pl.BlockSpec((tm, tn), lambda i, j: (i, j)),
                  pl.BlockSpec((1, 1, tn), lambda i, j: (i // tpb, 0, gate_col + j)),
                  pl.BlockSpec((1, 1, tn), lambda i, j: (dm.batch, 0, gate_col + j))],
        out_specs=pl.BlockSpec((tm, tn), lambda i, j: (i, j)),
        out_shape=jax.ShapeDtypeStruct((m, d), F32),
        compiler_params=_params(("parallel", "arbitrary")),
        name="out_proj",
    )(g, w, xc, mods3, mods3)


def _lane_tile_reduce(v, op):
    part = v[:, :HEAD_DIM]
    for t in range(1, v.shape[1] // HEAD_DIM):
        part = op(part, v[:, t * HEAD_DIM:(t + 1) * HEAD_DIM])
    return part


def _scores(pairs, k_ref, krows):
    blocks = []
    for q, kcols in pairs:
        half = q.shape[0] // 2
        for qq in (q[:half], q[half:]):
            blocks.append(lax.dot_general(qq, k_ref[krows, kcols], (((1,), (1,)), ((), ())),
                                          preferred_element_type=F32))
    return jnp.concatenate(blocks, axis=0)


def _weighted(p, v_ref, krows, with_sums=False):
    half = p.shape[0] // 2
    pb = p.astype(BF16)
    v = v_ref[krows, :]
    if with_sums:
        v = jnp.concatenate([v, jnp.ones_like(v)], axis=1)
    return jnp.concatenate([jnp.dot(pp, v, preferred_element_type=F32)
                            for pp in (pb[:half], pb[half:])], axis=0)


def _attend_in_place(pairs, k_ref, v_ref, key_rows, tk):
    chunks = [slice(c * tk, (c + 1) * tk) for c in range(key_rows // tk)]
    s = [_scores(pairs, k_ref, krows) for krows in chunks]
    m_part = functools.reduce(jnp.maximum, [_lane_tile_reduce(sc, jnp.maximum) for sc in s])
    m = jnp.max(m_part, axis=1, keepdims=True)
    p = [jnp.exp2(sc - m) for sc in s]
    l_part = functools.reduce(jnp.add, [_lane_tile_reduce(pc, jnp.add) for pc in p])
    acc = functools.reduce(jnp.add, [_weighted(pc, v_ref, krows) for pc, krows in zip(p, chunks)])
    return acc * (1.0 / jnp.sum(l_part, axis=1, keepdims=True))


def _attend_pipelined(pairs_next, s_next_ref, s_cur_ref, finish_cur, k_ref, v_ref, m_ref, key_rows, tk):
    n_chunks = key_rows // tk
    have_cur = s_cur_ref is not None
    dv = v_ref.shape[1]
    mxu_sums = dv == HEAD_DIM
    chunks = [slice(c * tk, (c + 1) * tk) for c in range(n_chunks)]
    acc = None
    for c, krows in enumerate(chunks if have_cur else ()):
        p = jnp.concatenate(
            [jnp.exp2(s_cur_ref[c, :, t * HEAD_DIM:(t + 1) * HEAD_DIM] - m_ref[0])
             for t in range(tk // HEAD_DIM)], axis=1)
        if not mxu_sums:
            part = _lane_tile_reduce(p, jnp.add)
            m_ref[1] = part if c == 0 else m_ref[1] + part
        pv = _weighted(p, v_ref, krows, with_sums=mxu_sums)
        acc = pv if acc is None else acc + pv
    if have_cur:
        if mxu_sums:
            finish_cur(acc[:, :dv] * (1.0 / acc[:, dv:]))
        else:
            finish_cur(acc * (1.0 / jnp.sum(m_ref[1], axis=1, keepdims=True)))
    m_part = None
    for c, krows in enumerate(chunks if pairs_next is not None else ()):
        s = _scores(pairs_next, k_ref, krows)
        s_next_ref[c] = s
        part = _lane_tile_reduce(s, jnp.maximum)
        m_part = part if m_part is None else jnp.maximum(m_part, part)
    if pairs_next is not None:
        m_ref[0] = jnp.broadcast_to(jnp.max(m_part, axis=1, keepdims=True), m_ref.shape[1:])


def _attention_steps(step, queries, finish, k_ref, v_ref, o_ref, s_ref, m_ref, *, tq, tk, ctx_len, rows,
                     ctx_out):
    n_lat = (rows - ctx_len) // tq

    def lat_row0(t):
        return pl.multiple_of(ctx_len + (t - 1) * tq, math.gcd(ctx_len, tq))

    def pipe(t_next, t_cur):
        pairs = None if t_next is None else queries(lat_row0(t_next), tq)
        _attend_pipelined(pairs, None if t_next is None else s_ref,
                          None if t_cur is None else s_ref,
                          None if t_cur is None else functools.partial(finish, lat_row0(t_cur), tq),
                          k_ref, v_ref, m_ref, rows, tk)

    @pl.when(step == 0)
    def _():
        if ctx_out:
            finish(0, ctx_len, _attend_in_place(queries(0, ctx_len), k_ref, v_ref, ctx_len, tk))
        else:
            o_ref[:ctx_len, :] = jnp.zeros((ctx_len, o_ref.shape[1]), o_ref.dtype)
        pipe(1, None)

    if n_lat > 1:
        pl.when((step >= 1) & (step < n_lat))(functools.partial(pipe, step + 1, step))
    pl.when(step == n_lat)(functools.partial(pipe, None, step))


def _gqa_kernel(q_ref, k_ref, v_ref, z_ref, o_ref, s_ref, m_ref, *, tq, tk, group, ctx_len, rows,
                ctx_out):
    def queries(row0, nq):
        q = jnp.concatenate([q_ref[pl.ds(row0, nq), g * HEAD_DIM:(g + 1) * HEAD_DIM]
                             for g in range(group)], axis=0)
        return [(q, slice(0, HEAD_DIM))]

    def finish(row0, nq, o):
        qrows = pl.ds(row0, nq)
        for g in range(group):
            cols = slice(g * HEAD_DIM, (g + 1) * HEAD_DIM)
            z = z_ref[qrows, cols].astype(F32)
            o_ref[qrows, cols] = (o[g * nq:(g + 1) * nq] * _silu(z)).astype(o_ref.dtype)

    _attention_steps(pl.program_id(2), queries, finish, k_ref, v_ref, o_ref, s_ref, m_ref,
                     tq=tq, tk=tk, ctx_len=ctx_len, rows=rows, ctx_out=ctx_out)


def _gqa_attention(proj, dm, *, tq, tk, ctx_out):
    rows = _rows(dm)
    m = proj.shape[0]
    width = dm.d
    heads = width // HEAD_DIM
    group = heads // GQA_KV_HEADS
    gw = group * HEAD_DIM
    k_col0 = width // HEAD_DIM
    v_col0 = k_col0 + GQA_KV_HEADS
    z_col0 = (width + 2 * GQA_KV_HEADS * HEAD_DIM) // gw
    kern = functools.partial(_gqa_kernel, tq=tq, tk=tk, group=group, ctx_len=dm.ctx, rows=rows,
                             ctx_out=ctx_out)
    return pl.pallas_call(
        kern,
        grid=(dm.batch, GQA_KV_HEADS, 1 + dm.seq // tq),
        in_specs=[pl.BlockSpec((rows, gw), lambda b, h, i: (b, h)),
                  pl.BlockSpec((rows, HEAD_DIM), lambda b, h, i: (b, k_col0 + h)),
                  pl.BlockSpec((rows, HEAD_DIM), lambda b, h, i: (b, v_col0 + h)),
                  pl.BlockSpec((rows, gw), lambda b, h, i: (b, z_col0 + h))],
        out_specs=pl.BlockSpec((rows, gw), lambda b, h, i: (b, h)),
        out_shape=jax.ShapeDtypeStruct((m, width), BF16),
        scratch_shapes=[pltpu.VMEM((rows // tk, group * tq, tk), F32),
                        pltpu.VMEM((2, group * tq, HEAD_DIM), F32)],
        compiler_params=_params(("parallel", "parallel", "arbitrary")),
        name="gqa_attention",
    )(proj, proj, proj, proj)


def _diff_kernel(lam_ref, q_ref, k_ref, v_ref, z_ref, subln_ref, o_ref, s_ref, m_ref,
                 *, tq, tk, ctx_len, rows, lambda_init):
    lv = lam_ref[...]
    lam = (jnp.exp(jnp.sum(lv[0:1] * lv[1:2], axis=1, keepdims=True))
           - jnp.exp(jnp.sum(lv[2:3] * lv[3:4], axis=1, keepdims=True)) + lambda_init)

    def queries(row0, nq):
        halves = [slice(t * HEAD_DIM, (t + 1) * HEAD_DIM) for t in range(2)]
        return [(q_ref[pl.ds(row0, nq), cols], cols) for cols in halves]

    def finish(row0, nq, o):
        qrows = pl.ds(row0, nq)
        o = o[:nq] - lam * o[nq:]
        o = o * lax.rsqrt(jnp.mean(o * o, axis=-1, keepdims=True) + EPS) * subln_ref[...]
        o = o * (1.0 - lambda_init)
        o_ref[qrows, :] = (o * _silu(z_ref[qrows, :].astype(F32))).astype(o_ref.dtype)

    _attention_steps(pl.program_id(2), queries, finish, k_ref, v_ref, o_ref, s_ref, m_ref,
                     tq=tq, tk=tk, ctx_len=ctx_len, rows=rows, ctx_out=True)


def _diff_attention(proj, lam_vecs, subln, dm, *, tq, tk, lambda_init):
    rows = _rows(dm)
    m = proj.shape[0]
    width = dm.d
    hw = 2 * HEAD_DIM
    heads = width // hw
    kern = functools.partial(_diff_kernel, tq=tq, tk=tk, ctx_len=dm.ctx, rows=rows, lambda_init=lambda_init)

    def head_block(first):
        return pl.BlockSpec((rows, hw), lambda b, h, i: (b, first + h))

    return pl.pallas_call(
        kern,
        grid=(dm.batch, heads, 1 + dm.seq // tq),
        in_specs=[pl.BlockSpec(lam_vecs.shape, lambda b, h, i: (0, 0)),
                  head_block(0), head_block(heads), head_block(2 * heads), head_block(3 * heads),
                  pl.BlockSpec((1, hw), lambda b, h, i: (0, 0))],
        out_specs=head_block(0),
        out_shape=jax.ShapeDtypeStruct((m, width), BF16),
        scratch_shapes=[pltpu.VMEM((rows // tk, 2 * tq, tk), F32),
                        pltpu.VMEM((2, 2 * tq, HEAD_DIM), F32)],
        compiler_params=_params(("parallel", "parallel", "arbitrary")),
        name="diff_attention",
    )(lam_vecs, proj, proj, proj, proj, subln.reshape(1, hw))


CONV_TAPS = (-1, 0, 1, 2)
CONV_STRIP = 256


def _conv_kernel(x_ref, shift_ref, w_ref, b_ref, o_ref, *, tr, ctx_tiles):
    edge = BF16_SUBLANES
    centre = CONV_TAPS.index(0)
    n_tiles = x_ref.shape[0] // tr
    row = lax.broadcasted_iota(jnp.int32, (edge, 1), 0)
    for c0 in range(0, x_ref.shape[1], CONV_STRIP):
        cols = slice(c0, c0 + CONV_STRIP)
        w = [w_ref[t:t + 1, cols] for t in range(SSM_CONV)]
        for r in range(n_tiles):
            r0 = r * tr
            xb = x_ref[r0:r0 + tr, cols]
            y = b_ref[:, cols] + w[centre] * xb.astype(F32)
            for k, tap in enumerate(t for t in range(SSM_CONV) if t != centre):
                y = y + w[tap] * jnp.dot(shift_ref[k], xb, preferred_element_type=F32)
            o_ref[r0 + edge:r0 + tr - edge, cols] = _silu(y[edge:tr - edge]).astype(o_ref.dtype)
            top, bottom = y[:edge], y[tr - edge:]
            if r not in (0, ctx_tiles):
                prev = x_ref[r0 - 1:r0, cols].astype(F32)
                top = top + jnp.where(row == 0, w[0] * prev, 0.0)
            if r not in (ctx_tiles - 1, n_tiles - 1):
                nxt0 = x_ref[r0 + tr:r0 + tr + 1, cols].astype(F32)
                nxt1 = x_ref[r0 + tr + 1:r0 + tr + 2, cols].astype(F32)
                bottom = (bottom + jnp.where(row == edge - 2, w[3] * nxt0, 0.0)
                          + jnp.where(row == edge - 1, w[2] * nxt0 + w[3] * nxt1, 0.0))
            o_ref[r0:r0 + edge, cols] = _silu(top).astype(o_ref.dtype)
            o_ref[r0 + tr - edge:r0 + tr, cols] = _silu(bottom).astype(o_ref.dtype)


def _ssd_conv(proj, conv_w, conv_b, dm, *, col0, tr, tn):
    rows = _rows(dm)
    m = proj.shape[0]
    n = conv_w.shape[1]
    c0 = col0 // tn
    shifts = jnp.stack([jnp.eye(tr, k=off, dtype=BF16) for off in CONV_TAPS if off != 0])
    kern = functools.partial(_conv_kernel, tr=tr, ctx_tiles=dm.ctx // tr)
    return pl.pallas_call(
        kern,
        grid=(dm.batch, n // tn),
        in_specs=[pl.BlockSpec((rows, tn), lambda b, j: (b, c0 + j)),
                  pl.BlockSpec(shifts.shape, lambda b, j: (0, 0, 0)),
                  pl.BlockSpec((SSM_CONV, tn), lambda b, j: (0, j)),
                  pl.BlockSpec((1, tn), lambda b, j: (0, j))],
        out_specs=pl.BlockSpec((rows, tn), lambda b, j: (b, j)),
        out_shape=jax.ShapeDtypeStruct((m, n), BF16),
        compiler_params=_params(("parallel", "parallel")),
        name="ssd_conv",
    )(proj, shifts, conv_w, conv_b.reshape(1, n))


def _split_dot(a, b_hi_exact, dims):
    hi = a.astype(BF16)
    lo = (a - hi.astype(F32)).astype(BF16)
    if dims == "ab":
        return (jnp.dot(hi, b_hi_exact, preferred_element_type=F32)
                + jnp.dot(lo, b_hi_exact, preferred_element_type=F32))
    return (jnp.dot(b_hi_exact, hi, preferred_element_type=F32)
            + jnp.dot(b_hi_exact, lo, preferred_element_type=F32))


def _ssd_kernel(*refs, heads, reverse, finish):
    it = iter(refs)
    (x_ref, b_ref, c_ref, dt_ref, dtT_ref, bias_ref, biasT_ref, alog_ref, alogT_ref,
     expand_ref) = (next(it) for _ in range(10))
    other_ref = z_ref = dskip_ref = gain_ref = None
    if finish:
        other_ref, z_ref, dskip_ref, gain_ref = (next(it) for _ in range(4))
    y_ref, state_ref = next(it), next(it)
    q = SSM_CHUNK
    hpg = heads // SSM_GROUPS
    gw = hpg * SSM_HEAD_DIM
    pw = 2 * SSM_HEAD_DIM
    step = pl.program_id(1)

    @pl.when(step == 0)
    def _():
        state_ref[...] = jnp.zeros(state_ref.shape, F32)

    def softplus(v):
        return jnp.maximum(v, 0.0) + jnp.log(1.0 + jnp.exp(-jnp.abs(v)))

    def chunk(rs):
        dcol = heads if reverse else 0
        dt = softplus(dt_ref[rs, dcol:dcol + heads] + bias_ref[...])
        dtT = softplus(dtT_ref[:, rs] + biasT_ref[...])
        a = dt * -jnp.exp(alog_ref[...])
        aT = dtT * -jnp.exp(alogT_ref[...])
        ri = lax.broadcasted_iota(jnp.int32, (q, q), 0)
        ci = lax.broadcasted_iota(jnp.int32, (q, q), 1)
        before = (ci >= ri) if reverse else (ci <= ri)
        tri = before.astype(BF16)
        triT = ((ri >= ci) if reverse else (ri <= ci)).astype(BF16)
        cum = _split_dot(a, tri, "ba")
        cumT = _split_dot(aT, triT, "ab")
        end = 0 if reverse else q - 1
        total = cum[end:end + 1, :]
        expand = expand_ref[...]
        w_full = jnp.dot((jnp.exp(total - cum) * dt).astype(BF16), expand,
                         preferred_element_type=F32)
        e_full = jnp.dot(jnp.exp(cum).astype(BF16), expand, preferred_element_type=F32)
        cd_full = _split_dot(jnp.broadcast_to(jnp.exp(total), (8, heads)), expand, "ab")[0:1]
        log2e = math.log2(math.e)
        cum2 = cum * log2e
        lcumT2 = (cumT - jnp.log(dtT)) * log2e
        low = lax.broadcasted_iota(jnp.int32, (q, pw), 1) < SSM_HEAD_DIM

        for g in range(SSM_GROUPS):
            gcols = slice(g * gw, (g + 1) * gw)
            ncols = slice(g * SSM_STATE, (g + 1) * SSM_STATE)
            bg = b_ref[rs, ncols]
            cg = c_ref[rs, ncols]
            cb = lax.dot_general(cg, bg, (((1,), (1,)), ((), ())), preferred_element_type=F32)
            st = state_ref[:, gcols]
            y_off = jnp.dot(cg, st.astype(BF16), preferred_element_type=F32) * e_full[:, gcols]
            ys = []
            for j in range(hpg // 2):
                xp = x_ref[rs, g * gw + j * pw:g * gw + (j + 1) * pw].astype(F32)
                xbd = jnp.concatenate([jnp.where(low, xp, 0.0), jnp.where(low, 0.0, xp)], axis=0).astype(BF16)
                mats = []
                for hh in range(2):
                    h = g * hpg + 2 * j + hh
                    diff = cum2[:, h:h + 1] - lcumT2[h:h + 1, :]
                    mats.append((cb * jnp.exp2(jnp.where(before, diff, -jnp.inf))).astype(BF16))
                ys.append(jnp.dot(jnp.concatenate(mats, axis=1), xbd, preferred_element_type=F32)
                          + y_off[:, j * pw:(j + 1) * pw])
            xg = x_ref[rs, gcols].astype(F32)
            if finish:
                y = jnp.concatenate(ys, axis=1) + other_ref[rs, gcols].astype(F32) + dskip_ref[:, gcols] * xg
                v = y * _silu(z_ref[rs, gcols].astype(F32))
                v = v * lax.rsqrt(jnp.mean(v * v, axis=-1, keepdims=True) + EPS) * gain_ref[:, gcols]
                y_ref[rs, gcols] = v.astype(y_ref.dtype)
            else:
                for j, y in enumerate(ys):
                    y_ref[rs, g * gw + j * pw:g * gw + (j + 1) * pw] = y.astype(y_ref.dtype)
            xw = (xg * w_full[:, gcols]).astype(BF16)
            bgT = bg.astype(F32).T.astype(BF16)
            state_ref[:, gcols] = st * cd_full[:, gcols] + jnp.dot(bgT, xw, preferred_element_type=F32)

    order = range(x_ref.shape[0] // q)
    for k in (reversed(order) if reverse else order):
        chunk(slice(k * q, (k + 1) * q))


def _ssd_scan(xbc, dt, dtT, dt_bias, a_log, dm, *, inner, reverse, tail=None):
    rows = _rows(dm)
    m = xbc.shape[0]
    q = SSM_CHUNKS_PER_STEP * SSM_CHUNK
    heads = inner // SSM_HEAD_DIM
    nc = rows // q
    ctx_chunks = dm.ctx // q
    sw = SSM_GROUPS * SSM_STATE
    d = 1 if reverse else 0

    def chunk(b, s):
        if reverse:
            s = jnp.where(s < ctx_chunks, ctx_chunks - 1 - s, nc - 1 - (s - ctx_chunks))
        return b * nc + s

    expand = jnp.repeat(jnp.eye(heads, dtype=BF16), SSM_HEAD_DIM, axis=1)
    row_spec = pl.BlockSpec((q, inner), lambda b, s: (chunk(b, s), 0))
    vec_spec = pl.BlockSpec((1, inner), lambda b, s: (0, 0))
    args = [xbc, xbc, xbc, dt, dtT, dt_bias[d][None, :], dt_bias[d][:, None], a_log[d][None, :],
            a_log[d][:, None], expand]
    in_specs = [row_spec,
                pl.BlockSpec((q, sw), lambda b, s: (chunk(b, s), inner // sw)),
                pl.BlockSpec((q, sw), lambda b, s: (chunk(b, s), inner // sw + 1)),
                pl.BlockSpec((q, 2 * heads), lambda b, s: (chunk(b, s), 0)),
                pl.BlockSpec((heads, q), lambda b, s: (d, chunk(b, s))),
                pl.BlockSpec((1, heads), lambda b, s: (0, 0)),
                pl.BlockSpec((heads, 1), lambda b, s: (0, 0)),
                pl.BlockSpec((1, heads), lambda b, s: (0, 0)),
                pl.BlockSpec((heads, 1), lambda b, s: (0, 0)),
                pl.BlockSpec((heads, inner), lambda b, s: (0, 0))]
    if tail is not None:
        y_other, proj, d_skip, gain = tail
        args += [y_other, proj, jnp.repeat(d_skip.astype(F32), SSM_HEAD_DIM).reshape(1, inner),
                 gain.reshape(1, inner)]
        in_specs += [row_spec, row_spec, vec_spec, vec_spec]
    kern = functools.partial(_ssd_kernel, heads=heads, reverse=reverse, finish=tail is not None)
    return pl.pallas_call(
        kern,
        grid=(dm.batch, nc),
        in_specs=in_specs,
        out_specs=row_spec,
        out_shape=jax.ShapeDtypeStruct((m, inner), BF16),
        scratch_shapes=[pltpu.VMEM((SSM_STATE, inner), F32)],
        compiler_params=_params(("parallel", "arbitrary")),
        name="ssd_scan_bwd" if reverse else "ssd_scan_fwd",
    )(*args)


def _final_norm_kernel(x_ref, g_ref, o_ref):
    x = x_ref[...]
    o_ref[...] = x * lax.rsqrt(jnp.mean(x * x, axis=-1, keepdims=True) + EPS) * g_ref[...]


def _final_norm(xc, gain, dm, *, tr):
    rows = _rows(dm)
    tpb = rows // tr
    lat_tiles = dm.seq // tr
    ctx_tiles = dm.ctx // tr
    return pl.pallas_call(
        _final_norm_kernel,
        grid=(dm.batch, lat_tiles),
        in_specs=[pl.BlockSpec((tr, dm.d), lambda b, i: (b * tpb + ctx_tiles + i, 0)),
                  pl.BlockSpec((1, dm.d), lambda b, i: (0, 0))],
        out_specs=pl.BlockSpec((tr, dm.d), lambda b, i: (b * lat_tiles + i, 0)),
        out_shape=jax.ShapeDtypeStruct((dm.batch * dm.seq, dm.d), F32),
        compiler_params=_params(("parallel", "parallel")),
        name="final_norm",
    )(xc, gain.reshape(1, dm.d))


def _rope_tables(dm):
    d_axis = HEAD_DIM // 2
    t = jnp.arange(dm.seq, dtype=jnp.int32)
    inv_freq = jnp.power(ROPE_THETA, -jnp.arange(0, d_axis, 2, dtype=F32) / d_axis)
    ang_r = (t // GRID_W).astype(F32)[:, None] * inv_freq[None, :]
    ang_c = (t % GRID_W).astype(F32)[:, None] * inv_freq[None, :]
    cos = jnp.concatenate([jnp.cos(ang_r), jnp.cos(ang_c), jnp.cos(ang_r), jnp.cos(ang_c)], axis=1)
    sin = jnp.concatenate([-jnp.sin(ang_r), -jnp.sin(ang_c), jnp.sin(ang_r), jnp.sin(ang_c)], axis=1)
    cos = jnp.concatenate([jnp.ones((dm.ctx, HEAD_DIM), F32), cos], axis=0)
    sin = jnp.concatenate([jnp.zeros((dm.ctx, HEAD_DIM), F32), sin], axis=0)
    return cos, sin


def _relayout_kernel(w_ref, o_ref):
    tn = w_ref.shape[1]
    quarter = HEAD_DIM // 4
    r = lax.broadcasted_iota(jnp.int32, (tn, tn), 0)
    c = lax.broadcasted_iota(jnp.int32, (tn, tn), 1)
    blk = (c // quarter) % 4
    src_blk = jnp.where(blk == 1, 2, jnp.where(blk == 2, 1, blk))
    src = (c // HEAD_DIM) * HEAD_DIM + src_blk * quarter + c % quarter
    perm = (r == src).astype(BF16)
    o_ref[...] = jnp.dot(w_ref[...].astype(BF16), perm, preferred_element_type=F32).astype(BF16)


def _qk_weights(w_in, qk_cols, tn):
    d = w_in.shape[0]
    return pl.pallas_call(
        _relayout_kernel,
        grid=(qk_cols // tn,),
        in_specs=[pl.BlockSpec((d, tn), lambda j: (0, j))],
        out_specs=pl.BlockSpec((d, tn), lambda j: (0, j)),
        out_shape=jax.ShapeDtypeStruct((d, qk_cols), BF16),
        compiler_params=_params(("parallel",)),
        name="qk_relayout",
    )(w_in)


def _lambda_init(layer_idx):
    return 0.8 - 0.6 * math.exp(-0.3 * layer_idx)


def _tiles(dm):
    rows = _rows(dm)
    tm = rows // 2 if (rows // 2) % 128 == 0 else rows
    return dict(tm=tm, tn=min(512, dm.d), tq=dm.ctx, tk=dm.ctx)


def _forward(dm, x, c, ctx, c_ctx, layers, final_norm):
    d = dm.d
    til = _tiles(dm)
    tm, tn, tq, tk = til["tm"], til["tn"], til["tq"], til["tk"]
    scale = HEAD_DIM ** -0.5 * math.log2(math.e)
    xc = jnp.concatenate([ctx, x], axis=1).reshape(dm.batch * _rows(dm), d)
    cvec = jnp.concatenate([c, c_ctx[None, :], jnp.zeros((MOD_ROWS - dm.batch - 1, d), F32)], axis=0)
    rope = _rope_tables(dm)
    q_tiles = d // tn

    for i, (kind, (w_mod, b_mod, norm), p) in enumerate(layers):
        mods = _modulation(cvec, w_mod, b_mod)
        if kind == "gqa":
            w_in, q_gain, k_gain, w_out = p
            kv_tiles = GQA_KV_HEADS * HEAD_DIM // tn
            n_tiles = w_in.shape[1] // tn
            segs = ((0, q_tiles, 0, True, scale),
                    (q_tiles, q_tiles + kv_tiles, 1, True, None),
                    (q_tiles + kv_tiles, n_tiles, None, False, None))
            w_qk = _qk_weights(w_in, (q_tiles + kv_tiles) * tn, tn)
            gains = jnp.stack([q_gain, k_gain]).astype(F32)[:, _rope_layout()]
            proj = _in_proj(xc, norm, mods, w_in, dm, tm=tm, tn=tn, segs=segs, rope=rope, w_qk=w_qk,
                            gains=gains)
            g = _gqa_attention(proj, dm, tq=min(2 * tq, dm.seq), tk=tk, ctx_out=i < len(layers) - 1)
        elif kind == "diff":
            w_in, lq1, lk1, lq2, lk2, subln, w_out = p
            n_tiles = w_in.shape[1] // tn
            segs = ((0, q_tiles, None, True, scale),
                    (q_tiles, 2 * q_tiles, None, True, None),
                    (2 * q_tiles, n_tiles, None, False, None))
            proj = _in_proj(xc, norm, mods, w_in, dm, tm=tm, tn=tn, segs=segs, rope=rope,
                            w_qk=_qk_weights(w_in, 2 * q_tiles * tn, tn))
            lam_vecs = jnp.stack([lq1, lk1, lq2, lk2]).astype(F32)
            g = _diff_attention(proj, lam_vecs, subln, dm, tq=min(4 * tq, dm.seq), tk=tk,
                                lambda_init=_lambda_init(i))
        else:
            w_in, conv_w, conv_b, dt_bias, a_log, d_skip, ssm_norm, w_out = p
            inner = w_out.shape[0]
            n_main = 2 * inner + 2 * SSM_GROUPS * SSM_STATE
            tn_ssd = math.gcd(n_main, 2 * tn)
            segs = ((0, n_main // tn_ssd, None, False, None),)
            proj, dt, dtT = _in_proj(xc, norm, mods, w_in, dm, tm=tm, tn=tn_ssd,
                                     segs=segs, w_dt=w_in[:, n_main:].astype(BF16))
            xbc = _ssd_conv(proj, conv_w, conv_b, dm, col0=inner, tr=tq,
                            tn=math.gcd(inner, conv_w.shape[1]))
            y_f = _ssd_scan(xbc, dt, dtT, dt_bias, a_log, dm, inner=inner, reverse=False)
            g = _ssd_scan(xbc, dt, dtT, dt_bias, a_log, dm, inner=inner, reverse=True,
                          tail=(y_f, proj, d_skip, ssm_norm))
        wide = w_out.shape[0] > d
        xc = _out_proj(g, w_out, xc, mods, dm, tm=tm if wide else _rows(dm), tn=tn)

    out = _final_norm(xc, final_norm, dm, tr=tq)
    return out.reshape(dm.batch, dm.seq, d)


def kernel(x, c, ctx, c_ctx, l0_w_mod, l0_b_mod, l0_norm, l0_w_in, l0_q_gain, l0_k_gain, l0_w_out, l1_w_mod, l1_b_mod, l1_norm, l1_w_in, l1_lambda_q1, l1_lambda_k1, l1_lambda_q2, l1_lambda_k2, l1_subln, l1_w_out, l2_w_mod, l2_b_mod, l2_norm, l2_w_in, l2_conv_w, l2_conv_b, l2_dt_bias, l2_A_log, l2_D, l2_ssm_norm, l2_w_out, l3_w_mod, l3_b_mod, l3_norm, l3_w_in, l3_q_gain, l3_k_gain, l3_w_out, final_norm):
    dm = Dims(batch=x.shape[0], seq=x.shape[1], ctx=ctx.shape[1], d=x.shape[2])
    layers = (
        ("gqa", (l0_w_mod, l0_b_mod, l0_norm), (l0_w_in, l0_q_gain, l0_k_gain, l0_w_out)),
        ("diff", (l1_w_mod, l1_b_mod, l1_norm), (l1_w_in, l1_lambda_q1, l1_lambda_k1, l1_lambda_q2,
                                                 l1_lambda_k2, l1_subln, l1_w_out)),
        ("ssd", (l2_w_mod, l2_b_mod, l2_norm), (l2_w_in, l2_conv_w, l2_conv_b, l2_dt_bias, l2_A_log,
                                                l2_D, l2_ssm_norm, l2_w_out)),
        ("gqa", (l3_w_mod, l3_b_mod, l3_norm), (l3_w_in, l3_q_gain, l3_k_gain, l3_w_out)),
    )
    return _forward(dm, x, c, ctx, c_ctx, layers, final_norm)
```

```python
import collections
import functools
import math

import jax
import jax.numpy as jnp
from jax import lax
from jax.experimental import pallas as pl
from jax.experimental.pallas import tpu as pltpu

F32 = jnp.float32
BF16 = jnp.bfloat16

EPS = 1e-6
ROPE_THETA = 10000.0
HEAD_DIM = 128
GQA_KV_HEADS = 4
SSM_HEAD_DIM = 64
SSM_GROUPS = 8
SSM_STATE = 128
SSM_CONV = 4
SSM_CHUNK = 128
SSM_CHUNKS_PER_STEP = 2
GRID_W = 64
N_MODS = 3
MOD_ROWS = 8
MOD_TILE = 1024
ROW_PARTS = 8
NORM_SLAB = 16
BF16_SUBLANES = 16
VMEM_LIMIT = 56 * 1024 * 1024

Dims = collections.namedtuple("Dims", "batch seq ctx d")


def _rows(dm):
    return dm.ctx + dm.seq


def _silu(v):
    return v * (1.0 / (1.0 + jnp.exp2(v * -math.log2(math.e))))


def _params(semantics):
    return pltpu.CompilerParams(dimension_semantics=semantics, vmem_limit_bytes=VMEM_LIMIT)


def _mod_kernel(c_ref, w_ref, b_ref, o_ref):
    s = _silu(c_ref[...]).astype(BF16)
    o_ref[...] = jnp.dot(s, w_ref[...].astype(BF16), preferred_element_type=F32) + b_ref[...]


def _modulation(cvec, w_mod, b_mod):
    d, n = w_mod.shape
    tn = math.gcd(n, MOD_TILE)
    return pl.pallas_call(
        _mod_kernel,
        grid=(n // tn,),
        in_specs=[pl.BlockSpec((MOD_ROWS, d), lambda j: (0, 0)),
                  pl.BlockSpec((d, tn), lambda j: (0, j)),
                  pl.BlockSpec((1, tn), lambda j: (0, j))],
        out_specs=pl.BlockSpec((MOD_ROWS, tn), lambda j: (0, j)),
        out_shape=jax.ShapeDtypeStruct((MOD_ROWS, n), F32),
        compiler_params=_params(("arbitrary",)),
        name="adaln_mod",
    )(cvec, w_mod, b_mod.reshape(1, n))


def _rope_layout():
    quarter = HEAD_DIM // 4
    idx = jnp.arange(HEAD_DIM).reshape(4, quarter)
    return idx[jnp.array([0, 2, 1, 3])].reshape(HEAD_DIM)


def _in_proj_kernel(*refs, tm, tn, tiles_per_batch, ctx_len, segs, has_rope, has_gain, has_dt):
    it = iter(refs)
    x_ref, g_ref, shl_ref, scl_ref, shc_ref, scc_ref, w_ref = (next(it) for _ in range(7))
    cos_ref = sin_ref = wqk_ref = gain_ref = wdt_ref = wdtT_ref = dt_ref = dtT_ref = None
    if has_rope:
        cos_ref, sin_ref, wqk_ref = next(it), next(it), next(it)
    if has_gain:
        gain_ref = next(it)
    if has_dt:
        wdt_ref, wdtT_ref = next(it), next(it)
    o_ref = next(it)
    if has_dt:
        dt_ref, dtT_ref = next(it), next(it)
    h_ref, mul_ref, add_ref = next(it), next(it), next(it)

    i = pl.program_id(0)
    j = pl.program_id(1)

    part = tm // ROW_PARTS
    starts = range(0, tm, part)

    def modulation_tables():
        d = g_ref.shape[1]
        g = g_ref[...]
        for k, (sc_ref, sh_ref) in enumerate(((scl_ref, shl_ref), (scc_ref, shc_ref))):
            mul_ref[k] = jnp.broadcast_to(g * (1.0 + sc_ref[0]), (NORM_SLAB, d))
            add_ref[k] = jnp.broadcast_to(sh_ref[0], (NORM_SLAB, d))

    def normalise(r0):
        for s0 in range(r0, r0 + part, NORM_SLAB):
            rows = slice(s0, s0 + NORM_SLAB)
            x = x_ref[rows, :]
            y = x * lax.rsqrt(jnp.mean(x * x, axis=-1, keepdims=True) + EPS)
            k = ((i % tiles_per_batch) * tm + s0 < ctx_len).astype(jnp.int32)
            h_ref[rows, :] = (y * mul_ref[k] + add_ref[k]).astype(BF16)

    def tile(gain_idx, rope, scale, gate, first):
        w = wqk_ref[...] if rope else w_ref[...].astype(BF16)
        plain = gain_idx is None and not rope and scale is None and not gate
        if plain and not first:
            o_ref[...] = jnp.dot(h_ref[...], w, preferred_element_type=F32).astype(o_ref.dtype)
            return
        if first:
            modulation_tables()
        accs = []
        for r0 in starts:
            if first:
                normalise(r0)
            accs.append(jnp.dot(h_ref[r0:r0 + part, :], w, preferred_element_type=F32))
        for r0, acc in zip(starts, accs):
            rows = slice(r0, r0 + part)
            if plain or gate:
                o_ref[rows, :] = (_silu(acc) if gate else acc).astype(o_ref.dtype)
                continue
            for hh in range(tn // HEAD_DIM):
                cols = slice(hh * HEAD_DIM, (hh + 1) * HEAD_DIM)
                v = acc[:, cols]
                if gain_idx is not None:
                    v = v * lax.rsqrt(jnp.mean(v * v, axis=-1, keepdims=True) + EPS)
                    v = v * gain_ref[gain_idx:gain_idx + 1, :]
                if rope:
                    v = v * cos_ref[rows, :] + pltpu.roll(v, HEAD_DIM // 2, 1) * sin_ref[rows, :]
                if scale is not None:
                    v = v * scale
                o_ref[rows, cols] = v.astype(o_ref.dtype)
        if first and has_dt:
            h = h_ref[...]
            dt_ref[...] = jnp.dot(h, wdt_ref[...], preferred_element_type=F32)
            dtT_ref[...] = lax.dot_general(wdtT_ref[...], h, (((1,), (1,)), ((), ())),
                                           preferred_element_type=F32)

    for (lo, hi, gain_idx, rope, scale, gate) in segs:
        if lo == 0:
            pl.when(j == 0)(functools.partial(tile, gain_idx, rope, scale, gate, True))
            lo = 1
        pl.when((j >= lo) & (j < hi))(functools.partial(tile, gain_idx, rope, scale, gate, False))


def _in_proj(xc, norm_g, mods, w, dm, *, tm, tn, segs, rope=None, w_qk=None, gains=None, w_dt=None):
    m, d = xc.shape
    n = segs[-1][1] * tn
    tpb = _rows(dm) // tm
    mods3 = mods.reshape(MOD_ROWS, 1, N_MODS * d)
    has_rope, has_gain, has_dt = rope is not None, gains is not None, w_dt is not None
    qk_tiles = w_qk.shape[1] // tn if has_rope else 0

    def lat(col):
        return pl.BlockSpec((1, 1, d), lambda i, j: (i // tpb, 0, col))

    def ctx(col):
        return pl.BlockSpec((1, 1, d), lambda i, j: (dm.batch, 0, col))

    args = [xc, norm_g.reshape(1, d), mods3, mods3, mods3, mods3, w]
    in_specs = [pl.BlockSpec((tm, d), lambda i, j: (i, 0)),
                pl.BlockSpec((1, d), lambda i, j: (0, 0)),
                lat(0), lat(1), ctx(0), ctx(1),
                pl.BlockSpec((d, tn), lambda i, j: (0, jnp.maximum(j, qk_tiles)))]
    if has_rope:
        args += list(rope) + [w_qk]
        in_specs += [pl.BlockSpec((tm, HEAD_DIM), lambda i, j: (i % tpb, 0))] * 2
        in_specs.append(pl.BlockSpec((d, tn), lambda i, j: (0, jnp.minimum(j, qk_tiles - 1))))
    if has_gain:
        args.append(gains)
        in_specs.append(pl.BlockSpec(gains.shape, lambda i, j: (0, 0)))
    out_shape = [jax.ShapeDtypeStruct((m, n), BF16)]
    out_specs = [pl.BlockSpec((tm, tn), lambda i, j: (i, j))]
    if has_dt:
        n_dt = w_dt.shape[1]
        args += [w_dt, w_dt.T]
        in_specs += [pl.BlockSpec((d, n_dt), lambda i, j: (0, 0)),
                     pl.BlockSpec((n_dt, d), lambda i, j: (0, 0))]
        out_shape += [jax.ShapeDtypeStruct((m, n_dt), F32), jax.ShapeDtypeStruct((n_dt, m), F32)]
        out_specs += [pl.BlockSpec((tm, n_dt), lambda i, j: (i, 0)),
                      pl.BlockSpec((n_dt, tm), lambda i, j: (0, i))]
    kern = functools.partial(_in_proj_kernel, tm=tm, tn=tn, tiles_per_batch=tpb, ctx_len=dm.ctx,
                             segs=segs, has_rope=has_rope, has_gain=has_gain, has_dt=has_dt)
    out = pl.pallas_call(
        kern,
        grid=(m // tm, n // tn),
        in_specs=in_specs,
        out_specs=out_specs,
        out_shape=out_shape,
        scratch_shapes=[pltpu.VMEM((tm, d), BF16),
                        pltpu.VMEM((2, NORM_SLAB, d), F32),
                        pltpu.VMEM((2, NORM_SLAB, d), F32)],
        compiler_params=_params(("parallel", "arbitrary")),
        name="in_proj",
    )(*args)
    return out if has_dt else out[0]


def _out_proj_kernel(g_ref, w_ref, x_ref, gl_ref, gc_ref, o_ref, *, tm, tiles_per_batch, ctx_len):
    i = pl.program_id(0)
    acc = jnp.dot(g_ref[...], w_ref[...].astype(BF16), preferred_element_type=F32)
    row = (i % tiles_per_batch) * tm + lax.broadcasted_iota(jnp.int32, (tm, 1), 0)
    gate = jnp.where(row < ctx_len, gc_ref[0], gl_ref[0])
    o_ref[...] = x_ref[...] + gate * acc


def _out_proj(g, w, xc, mods, dm, *, tm, tn):
    m, k = g.shape
    d = w.shape[1]
    tpb = _rows(dm) // tm
    tiles_n = d // tn
    mods3 = mods.reshape(MOD_ROWS, 1, N_MODS * d)
    gate_col = 2 * tiles_n
    kern = functools.partial(_out_proj_kernel, tm=tm, tiles_per_batch=tpb, ctx_len=dm.ctx)
    return pl.pallas_call(
        kern,
        grid=(m // tm, tiles_n),
        in_specs=[pl.BlockSpec((tm, k), lambda i, j: (i, 0)),
                  pl.BlockSpec((k, tn), lambda i, j: (0, j)),
                  pl.BlockSpec((tm, tn), lambda i, j: (i, j)),
                  pl.BlockSpec((1, 1, tn), lambda i, j: (i // tpb, 0, gate_col + j)),
                  pl.BlockSpec((1, 1, tn), lambda i, j: (dm.batch, 0, gate_col + j))],
        out_specs=pl.BlockSpec((tm, tn), lambda i, j: (i, j)),
        out_shape=jax.ShapeDtypeStruct((m, d), F32),
        compiler_params=_params(("parallel", "arbitrary")),
        name="out_proj",
    )(g, w, xc, mods3, mods3)


def _lane_tile_reduce(v, op):
    part = v[:, :HEAD_DIM]
    for t in range(1, v.shape[1] // HEAD_DIM):
        part = op(part, v[:, t * HEAD_DIM:(t + 1) * HEAD_DIM])
    return part


def _scores(pairs, k_ref, krows):
    blocks = []
    for q, kcols in pairs:
        half = q.shape[0] // 2
        for qq in (q[:half], q[half:]):
            blocks.append(lax.dot_general(qq, k_ref[krows, kcols], (((1,), (1,)), ((), ())),
                                          preferred_element_type=F32))
    return jnp.concatenate(blocks, axis=0)


def _weighted(p, v_ref, krows, with_sums=False):
    half = p.shape[0] // 2
    pb = p.astype(BF16)
    v = v_ref[krows, :]
    if with_sums:
        v = jnp.concatenate([v, jnp.ones_like(v)], axis=1)
    return jnp.concatenate([jnp.dot(pp, v, preferred_element_type=F32)
                            for pp in (pb[:half], pb[half:])], axis=0)


def _attend_in_place(pairs, k_ref, v_ref, key_rows, tk):
    chunks = [slice(c * tk, (c + 1) * tk) for c in range(key_rows // tk)]
    s = [_scores(pairs, k_ref, krows) for krows in chunks]
    m_part = functools.reduce(jnp.maximum, [_lane_tile_reduce(sc, jnp.maximum) for sc in s])
    m = jnp.max(m_part, axis=1, keepdims=True)
    p = [jnp.exp2(sc - m) for sc in s]
    l_part = functools.reduce(jnp.add, [_lane_tile_reduce(pc, jnp.add) for pc in p])
    acc = functools.reduce(jnp.add, [_weighted(pc, v_ref, krows) for pc, krows in zip(p, chunks)])
    return acc * (1.0 / jnp.sum(l_part, axis=1, keepdims=True))


def _attend_pipelined(pairs_next, s_next_ref, s_cur_ref, finish_cur, k_ref, v_ref, m_ref, key_rows, tk):
    n_chunks = key_rows // tk
    have_cur = s_cur_ref is not None
    dv = v_ref.shape[1]
    mxu_sums = dv == HEAD_DIM
    chunks = [slice(c * tk, (c + 1) * tk) for c in range(n_chunks)]
    acc = None
    for c, krows in enumerate(chunks if have_cur else ()):
        p = jnp.concatenate(
            [jnp.exp2(s_cur_ref[c, :, t * HEAD_DIM:(t + 1) * HEAD_DIM] - m_ref[0])
             for t in range(tk // HEAD_DIM)], axis=1)
        if not mxu_sums:
            part = _lane_tile_reduce(p, jnp.add)
            m_ref[1] = part if c == 0 else m_ref[1] + part
        pv = _weighted(p, v_ref, krows, with_sums=mxu_sums)
        acc = pv if acc is None else acc + pv
    if have_cur:
        if mxu_sums:
            finish_cur(acc[:, :dv] * (1.0 / acc[:, dv:]))
        else:
            finish_cur(acc * (1.0 / jnp.sum(m_ref[1], axis=1, keepdims=True)))
    m_part = None
    for c, krows in enumerate(chunks if pairs_next is not None else ()):
        s = _scores(pairs_next, k_ref, krows)
        s_next_ref[c] = s
        part = _lane_tile_reduce(s, jnp.maximum)
        m_part = part if m_part is None else jnp.maximum(m_part, part)
    if pairs_next is not None:
        m_ref[0] = jnp.broadcast_to(jnp.max(m_part, axis=1, keepdims=True), m_ref.shape[1:])


def _attention_steps(step, queries, finish, k_ref, v_ref, o_ref, s_ref, m_ref, *, tq, tk, ctx_len, rows,
                     ctx_out):
    n_lat = (rows - ctx_len) // tq

    def lat_row0(t):
        return pl.multiple_of(ctx_len + (t - 1) * tq, math.gcd(ctx_len, tq))

    def pipe(t_next, t_cur):
        pairs = None if t_next is None else queries(lat_row0(t_next), tq)
        _attend_pipelined(pairs, None if t_next is None else s_ref,
                          None if t_cur is None else s_ref,
                          None if t_cur is None else functools.partial(finish, lat_row0(t_cur), tq),
                          k_ref, v_ref, m_ref, rows, tk)

    @pl.when(step == 0)
    def _():
        if ctx_out:
            finish(0, ctx_len, _attend_in_place(queries(0, ctx_len), k_ref, v_ref, ctx_len, tk))
        else:
            o_ref[:ctx_len, :] = jnp.zeros((ctx_len, o_ref.shape[1]), o_ref.dtype)
        pipe(1, None)

    if n_lat > 1:
        pl.when((step >= 1) & (step < n_lat))(functools.partial(pipe, step + 1, step))
    pl.when(step == n_lat)(functools.partial(pipe, None, step))


def _gqa_kernel(q_ref, k_ref, v_ref, z_ref, o_ref, s_ref, m_ref, *, tq, tk, group, ctx_len, rows,
                ctx_out):
    def queries(row0, nq):
        q = jnp.concatenate([q_ref[pl.ds(row0, nq), g * HEAD_DIM:(g + 1) * HEAD_DIM]
                             for g in range(group)], axis=0)
        return [(q, slice(0, HEAD_DIM))]

    def finish(row0, nq, o):
        qrows = pl.ds(row0, nq)
        for g in range(group):
            cols = slice(g * HEAD_DIM, (g + 1) * HEAD_DIM)
            z = z_ref[qrows, cols].astype(F32)
            o_ref[qrows, cols] = (o[g * nq:(g + 1) * nq] * _silu(z)).astype(o_ref.dtype)

    _attention_steps(pl.program_id(2), queries, finish, k_ref, v_ref, o_ref, s_ref, m_ref,
                     tq=tq, tk=tk, ctx_len=ctx_len, rows=rows, ctx_out=ctx_out)


def _gqa_attention(proj, dm, *, tq, tk, ctx_out):
    rows = _rows(dm)
    m = proj.shape[0]
    width = dm.d
    heads = width // HEAD_DIM
    group = heads // GQA_KV_HEADS
    gw = group * HEAD_DIM
    k_col0 = width // HEAD_DIM
    v_col0 = k_col0 + GQA_KV_HEADS
    z_col0 = (width + 2 * GQA_KV_HEADS * HEAD_DIM) // gw
    kern = functools.partial(_gqa_kernel, tq=tq, tk=tk, group=group, ctx_len=dm.ctx, rows=rows,
                             ctx_out=ctx_out)
    return pl.pallas_call(
        kern,
        grid=(dm.batch, GQA_KV_HEADS, 1 + dm.seq // tq),
        in_specs=[pl.BlockSpec((rows, gw), lambda b, h, i: (b, h)),
                  pl.BlockSpec((rows, HEAD_DIM), lambda b, h, i: (b, k_col0 + h)),
                  pl.BlockSpec((rows, HEAD_DIM), lambda b, h, i: (b, v_col0 + h)),
                  pl.BlockSpec((rows, gw), lambda b, h, i: (b, z_col0 + h))],
        out_specs=pl.BlockSpec((rows, gw), lambda b, h, i: (b, h)),
        out_shape=jax.ShapeDtypeStruct((m, width), BF16),
        scratch_shapes=[pltpu.VMEM((rows // tk, group * tq, tk), F32),
                        pltpu.VMEM((2, group * tq, HEAD_DIM), F32)],
        compiler_params=_params(("parallel", "parallel", "arbitrary")),
        name="gqa_attention",
    )(proj, proj, proj, proj)


def _diff_kernel(lam_ref, q_ref, k_ref, v_ref, z_ref, subln_ref, o_ref, s_ref, m_ref,
                 *, tq, tk, ctx_len, rows, lambda_init):
    lv = lam_ref[...]
    lam = (jnp.exp(jnp.sum(lv[0:1] * lv[1:2], axis=1, keepdims=True))
           - jnp.exp(jnp.sum(lv[2:3] * lv[3:4], axis=1, keepdims=True)) + lambda_init)

    def queries(row0, nq):
        halves = [slice(t * HEAD_DIM, (t + 1) * HEAD_DIM) for t in range(2)]
        return [(q_ref[pl.ds(row0, nq), cols], cols) for cols in halves]

    def finish(row0, nq, o):
        qrows = pl.ds(row0, nq)
        o = o[:nq] - lam * o[nq:]
        o = o * lax.rsqrt(jnp.mean(o * o, axis=-1, keepdims=True) + EPS) * subln_ref[...]
        o = o * (1.0 - lambda_init)
        o_ref[qrows, :] = (o * _silu(z_ref[qrows, :].astype(F32))).astype(o_ref.dtype)

    _attention_steps(pl.program_id(2), queries, finish, k_ref, v_ref, o_ref, s_ref, m_ref,
                     tq=tq, tk=tk, ctx_len=ctx_len, rows=rows, ctx_out=True)


def _diff_attention(proj, lam_vecs, subln, dm, *, tq, tk, lambda_init):
    rows = _rows(dm)
    m = proj.shape[0]
    width = dm.d
    hw = 2 * HEAD_DIM
    heads = width // hw
    kern = functools.partial(_diff_kernel, tq=tq, tk=tk, ctx_len=dm.ctx, rows=rows, lambda_init=lambda_init)

    def head_block(first):
        return pl.BlockSpec((rows, hw), lambda b, h, i: (b, first + h))

    return pl.pallas_call(
        kern,
        grid=(dm.batch, heads, 1 + dm.seq // tq),
        in_specs=[pl.BlockSpec(lam_vecs.shape, lambda b, h, i: (0, 0)),
                  head_block(0), head_block(heads), head_block(2 * heads), head_block(3 * heads),
                  pl.BlockSpec((1, hw), lambda b, h, i: (0, 0))],
        out_specs=head_block(0),
        out_shape=jax.ShapeDtypeStruct((m, width), BF16),
        scratch_shapes=[pltpu.VMEM((rows // tk, 2 * tq, tk), F32),
                        pltpu.VMEM((2, 2 * tq, HEAD_DIM), F32)],
        compiler_params=_params(("parallel", "parallel", "arbitrary")),
        name="diff_attention",
    )(lam_vecs, proj, proj, proj, proj, subln.reshape(1, hw))


CONV_TAPS = (-1, 0, 1, 2)
CONV_STRIP = 256


def _conv_kernel(x_ref, shift_ref, w_ref, b_ref, o_ref, *, tr, ctx_tiles):
    edge = BF16_SUBLANES
    centre = CONV_TAPS.index(0)
    n_tiles = x_ref.shape[0] // tr
    row = lax.broadcasted_iota(jnp.int32, (edge, 1), 0)
    for c0 in range(0, x_ref.shape[1], CONV_STRIP):
        cols = slice(c0, c0 + CONV_STRIP)
        w = [w_ref[t:t + 1, cols] for t in range(SSM_CONV)]
        for r in range(n_tiles):
            r0 = r * tr
            xb = x_ref[r0:r0 + tr, cols]
            y = b_ref[:, cols] + w[centre] * xb.astype(F32)
            for k, tap in enumerate(t for t in range(SSM_CONV) if t != centre):
                y = y + w[tap] * jnp.dot(shift_ref[k], xb, preferred_element_type=F32)
            o_ref[r0 + edge:r0 + tr - edge, cols] = _silu(y[edge:tr - edge]).astype(o_ref.dtype)
            top, bottom = y[:edge], y[tr - edge:]
            if r not in (0, ctx_tiles):
                prev = x_ref[r0 - 1:r0, cols].astype(F32)
                top = top + jnp.where(row == 0, w[0] * prev, 0.0)
            if r not in (ctx_tiles - 1, n_tiles - 1):
                nxt0 = x_ref[r0 + tr:r0 + tr + 1, cols].astype(F32)
                nxt1 = x_ref[r0 + tr + 1:r0 + tr + 2, cols].astype(F32)
                bottom = (bottom + jnp.where(row == edge - 2, w[3] * nxt0, 0.0)
                          + jnp.where(row == edge - 1, w[2] * nxt0 + w[3] * nxt1, 0.0))
            o_ref[r0:r0 + edge, cols] = _silu(top).astype(o_ref.dtype)
            o_ref[r0 + tr - edge:r0 + tr, cols] = _silu(bottom).astype(o_ref.dtype)


def _ssd_conv(proj, conv_w, conv_b, dm, *, col0, tr, tn):
    rows = _rows(dm)
    m = proj.shape[0]
    n = conv_w.shape[1]
    c0 = col0 // tn
    shifts = jnp.stack([jnp.eye(tr, k=off, dtype=BF16) for off in CONV_TAPS if off != 0])
    kern = functools.partial(_conv_kernel, tr=tr, ctx_tiles=dm.ctx // tr)
    return pl.pallas_call(
        kern,
        grid=(dm.batch, n // tn),
        in_specs=[pl.BlockSpec((rows, tn), lambda b, j: (b, c0 + j)),
                  pl.BlockSpec(shifts.shape, lambda b, j: (0, 0, 0)),
                  pl.BlockSpec((SSM_CONV, tn), lambda b, j: (0, j)),
                  pl.BlockSpec((1, tn), lambda b, j: (0, j))],
        out_specs=pl.BlockSpec((rows, tn), lambda b, j: (b, j)),
        out_shape=jax.ShapeDtypeStruct((m, n), BF16),
        compiler_params=_params(("parallel", "parallel")),
        name="ssd_conv",
    )(proj, shifts, conv_w, conv_b.reshape(1, n))


def _split_dot(a, b_hi_exact, dims):
    hi = a.astype(BF16)
    lo = (a - hi.astype(F32)).astype(BF16)
    if dims == "ab":
        return (jnp.dot(hi, b_hi_exact, preferred_element_type=F32)
                + jnp.dot(lo, b_hi_exact, preferred_element_type=F32))
    return (jnp.dot(b_hi_exact, hi, preferred_element_type=F32)
            + jnp.dot(b_hi_exact, lo, preferred_element_type=F32))


def _ssd_kernel(*refs, heads, reverse, finish):
    it = iter(refs)
    (x_ref, b_ref, c_ref, dt_ref, dtT_ref, bias_ref, biasT_ref, alog_ref, alogT_ref,
     expand_ref) = (next(it) for _ in range(10))
    other_ref = z_ref = dskip_ref = gain_ref = None
    if finish:
        other_ref, z_ref, dskip_ref, gain_ref = (next(it) for _ in range(4))
    y_ref, state_ref = next(it), next(it)
    q = SSM_CHUNK
    hpg = heads // SSM_GROUPS
    gw = hpg * SSM_HEAD_DIM
    pw = 2 * SSM_HEAD_DIM
    step = pl.program_id(1)

    @pl.when(step == 0)
    def _():
        state_ref[...] = jnp.zeros(state_ref.shape, F32)

    def softplus(v):
        return jnp.maximum(v, 0.0) + jnp.log(1.0 + jnp.exp(-jnp.abs(v)))

    def chunk(rs):
        dcol = heads if reverse else 0
        dt = softplus(dt_ref[rs, dcol:dcol + heads] + bias_ref[...])
        dtT = softplus(dtT_ref[:, rs] + biasT_ref[...])
        a = dt * -jnp.exp(alog_ref[...])
        aT = dtT * -jnp.exp(alogT_ref[...])
        ri = lax.broadcasted_iota(jnp.int32, (q, q), 0)
        ci = lax.broadcasted_iota(jnp.int32, (q, q), 1)
        before = (ci >= ri) if reverse else (ci <= ri)
        tri = before.astype(BF16)
        triT = ((ri >= ci) if reverse else (ri <= ci)).astype(BF16)
        cum = _split_dot(a, tri, "ba")
        cumT = _split_dot(aT, triT, "ab")
        end = 0 if reverse else q - 1
        total = cum[end:end + 1, :]
        expand = expand_ref[...]
        w_full = jnp.dot((jnp.exp(total - cum) * dt).astype(BF16), expand,
                         preferred_element_type=F32)
        e_full = jnp.dot(jnp.exp(cum).astype(BF16), expand, preferred_element_type=F32)
        cd_full = _split_dot(jnp.broadcast_to(jnp.exp(total), (8, heads)), expand, "ab")[0:1]
        log2e = math.log2(math.e)
        cum2 = cum * log2e
        lcumT2 = (cumT - jnp.log(dtT)) * log2e
        low = lax.broadcasted_iota(jnp.int32, (q, pw), 1) < SSM_HEAD_DIM

        for g in range(SSM_GROUPS):
            gcols = slice(g * gw, (g + 1) * gw)
            ncols = slice(g * SSM_STATE, (g + 1) * SSM_STATE)
            bg = b_ref[rs, ncols]
            cg = c_ref[rs, ncols]
            cb = lax.dot_general(cg, bg, (((1,), (1,)), ((), ())), preferred_element_type=F32)
            st = state_ref[:, gcols]
            y_off = jnp.dot(cg, st.astype(BF16), preferred_element_type=F32) * e_full[:, gcols]
            ys = []
            for j in range(hpg // 2):
                xp = x_ref[rs, g * gw + j * pw:g * gw + (j + 1) * pw].astype(F32)
                xbd = jnp.concatenate([jnp.where(low, xp, 0.0), jnp.where(low, 0.0, xp)], axis=0).astype(BF16)
                mats = []
                for hh in range(2):
                    h = g * hpg + 2 * j + hh
                    diff = cum2[:, h:h + 1] - lcumT2[h:h + 1, :]
                    mats.append((cb * jnp.exp2(jnp.where(before, diff, -jnp.inf))).astype(BF16))
                ys.append(jnp.dot(jnp.concatenate(mats, axis=1), xbd, preferred_element_type=F32)
                          + y_off[:, j * pw:(j + 1) * pw])
            xg = x_ref[rs, gcols].astype(F32)
            if finish:
                y = jnp.concatenate(ys, axis=1) + other_ref[rs, gcols].astype(F32) + dskip_ref[:, gcols] * xg
                v = y * z_ref[rs, gcols].astype(F32)
                v = v * lax.rsqrt(jnp.mean(v * v, axis=-1, keepdims=True) + EPS) * gain_ref[:, gcols]
                y_ref[rs, gcols] = v.astype(y_ref.dtype)
            else:
                for j, y in enumerate(ys):
                    y_ref[rs, g * gw + j * pw:g * gw + (j + 1) * pw] = y.astype(y_ref.dtype)
            xw = (xg * w_full[:, gcols]).astype(BF16)
            bgT = bg.astype(F32).T.astype(BF16)
            state_ref[:, gcols] = st * cd_full[:, gcols] + jnp.dot(bgT, xw, preferred_element_type=F32)

    order = range(x_ref.shape[0] // q)
    for k in (reversed(order) if reverse else order):
        chunk(slice(k * q, (k + 1) * q))


def _ssd_scan(xbc, dt, dtT, dt_bias, a_log, dm, *, inner, reverse, tail=None):
    rows = _rows(dm)
    m = xbc.shape[0]
    q = SSM_CHUNKS_PER_STEP * SSM_CHUNK
    heads = inner // SSM_HEAD_DIM
    nc = rows // q
    ctx_chunks = dm.ctx // q
    sw = SSM_GROUPS * SSM_STATE
    d = 1 if reverse else 0

    def chunk(b, s):
        if reverse:
            s = jnp.where(s < ctx_chunks, ctx_chunks - 1 - s, nc - 1 - (s - ctx_chunks))
        return b * nc + s

    expand = jnp.repeat(jnp.eye(heads, dtype=BF16), SSM_HEAD_DIM, axis=1)
    row_spec = pl.BlockSpec((q, inner), lambda b, s: (chunk(b, s), 0))
    vec_spec = pl.BlockSpec((1, inner), lambda b, s: (0, 0))
    args = [xbc, xbc, xbc, dt, dtT, dt_bias[d][None, :], dt_bias[d][:, None], a_log[d][None, :],
            a_log[d][:, None], expand]
    in_specs = [row_spec,
                pl.BlockSpec((q, sw), lambda b, s: (chunk(b, s), inner // sw)),
                pl.BlockSpec((q, sw), lambda b, s: (chunk(b, s), inner // sw + 1)),
                pl.BlockSpec((q, 2 * heads), lambda b, s: (chunk(b, s), 0)),
                pl.BlockSpec((heads, q), lambda b, s: (d, chunk(b, s))),
                pl.BlockSpec((1, heads), lambda b, s: (0, 0)),
                pl.BlockSpec((heads, 1), lambda b, s: (0, 0)),
                pl.BlockSpec((1, heads), lambda b, s: (0, 0)),
                pl.BlockSpec((heads, 1), lambda b, s: (0, 0)),
                pl.BlockSpec((heads, inner), lambda b, s: (0, 0))]
    if tail is not None:
        y_other, proj, d_skip, gain = tail
        args += [y_other, proj, jnp.repeat(d_skip.astype(F32), SSM_HEAD_DIM).reshape(1, inner),
                 gain.reshape(1, inner)]
        in_specs += [row_spec, row_spec, vec_spec, vec_spec]
    kern = functools.partial(_ssd_kernel, heads=heads, reverse=reverse, finish=tail is not None)
    return pl.pallas_call(
        kern,
        grid=(dm.batch, nc),
        in_specs=in_specs,
        out_specs=row_spec,
        out_shape=jax.ShapeDtypeStruct((m, inner), BF16),
        scratch_shapes=[pltpu.VMEM((SSM_STATE, inner), F32)],
        compiler_params=_params(("parallel", "arbitrary")),
        name="ssd_scan_bwd" if reverse else "ssd_scan_fwd",
    )(*args)


def _final_norm_kernel(x_ref, g_ref, o_ref):
    x = x_ref[...]
    o_ref[...] = x * lax.rsqrt(jnp.mean(x * x, axis=-1, keepdims=True) + EPS) * g_ref[...]


def _final_norm(xc, gain, dm, *, tr):
    rows = _rows(dm)
    tpb = rows // tr
    lat_tiles = dm.seq // tr
    ctx_tiles = dm.ctx // tr
    return pl.pallas_call(
        _final_norm_kernel,
        grid=(dm.batch, lat_tiles),
        in_specs=[pl.BlockSpec((tr, dm.d), lambda b, i: (b * tpb + ctx_tiles + i, 0)),
                  pl.BlockSpec((1, dm.d), lambda b, i: (0, 0))],
        out_specs=pl.BlockSpec((tr, dm.d), lambda b, i: (b * lat_tiles + i, 0)),
        out_shape=jax.ShapeDtypeStruct((dm.batch * dm.seq, dm.d), F32),
        compiler_params=_params(("parallel", "parallel")),
        name="final_norm",
    )(xc, gain.reshape(1, dm.d))


def _rope_tables(dm):
    d_axis = HEAD_DIM // 2
    t = jnp.arange(dm.seq, dtype=jnp.int32)
    inv_freq = jnp.power(ROPE_THETA, -jnp.arange(0, d_axis, 2, dtype=F32) / d_axis)
    ang_r = (t // GRID_W).astype(F32)[:, None] * inv_freq[None, :]
    ang_c = (t % GRID_W).astype(F32)[:, None] * inv_freq[None, :]
    cos = jnp.concatenate([jnp.cos(ang_r), jnp.cos(ang_c), jnp.cos(ang_r), jnp.cos(ang_c)], axis=1)
    sin = jnp.concatenate([-jnp.sin(ang_r), -jnp.sin(ang_c), jnp.sin(ang_r), jnp.sin(ang_c)], axis=1)
    cos = jnp.concatenate([jnp.ones((dm.ctx, HEAD_DIM), F32), cos], axis=0)
    sin = jnp.concatenate([jnp.zeros((dm.ctx, HEAD_DIM), F32), sin], axis=0)
    return cos, sin


def _relayout_kernel(w_ref, o_ref):
    tn = w_ref.shape[1]
    quarter = HEAD_DIM // 4
    r = lax.broadcasted_iota(jnp.int32, (tn, tn), 0)
    c = lax.broadcasted_iota(jnp.int32, (tn, tn), 1)
    blk = (c // quarter) % 4
    src_blk = jnp.where(blk == 1, 2, jnp.where(blk == 2, 1, blk))
    src = (c // HEAD_DIM) * HEAD_DIM + src_blk * quarter + c % quarter
    perm = (r == src).astype(BF16)
    o_ref[...] = jnp.dot(w_ref[...].astype(BF16), perm, preferred_element_type=F32).astype(BF16)


def _qk_weights(w_in, qk_cols, tn):
    d = w_in.shape[0]
    return pl.pallas_call(
        _relayout_kernel,
        grid=(qk_cols // tn,),
        in_specs=[pl.BlockSpec((d, tn), lambda j: (0, j))],
        out_specs=pl.BlockSpec((d, tn), lambda j: (0, j)),
        out_shape=jax.ShapeDtypeStruct((d, qk_cols), BF16),
        compiler_params=_params(("parallel",)),
        name="qk_relayout",
    )(w_in)


def _lambda_init(layer_idx):
    return 0.8 - 0.6 * math.exp(-0.3 * layer_idx)


def _tiles(dm):
    rows = _rows(dm)
    tm = rows // 2 if (rows // 2) % 128 == 0 else rows
    return dict(tm=tm, tn=min(512, dm.d), tq=dm.ctx, tk=dm.ctx)


def _forward(dm, x, c, ctx, c_ctx, layers, final_norm):
    d = dm.d
    til = _tiles(dm)
    tm, tn, tq, tk = til["tm"], til["tn"], til["tq"], til["tk"]
    scale = HEAD_DIM ** -0.5 * math.log2(math.e)
    xc = jnp.concatenate([ctx, x], axis=1).reshape(dm.batch * _rows(dm), d)
    cvec = jnp.concatenate([c, c_ctx[None, :], jnp.zeros((MOD_ROWS - dm.batch - 1, d), F32)], axis=0)
    rope = _rope_tables(dm)
    q_tiles = d // tn

    for i, (kind, (w_mod, b_mod, norm), p) in enumerate(layers):
        mods = _modulation(cvec, w_mod, b_mod)
        if kind == "gqa":
            w_in, q_gain, k_gain, w_out = p
            kv_tiles = GQA_KV_HEADS * HEAD_DIM // tn
            n_tiles = w_in.shape[1] // tn
            segs = ((0, q_tiles, 0, True, scale, False),
                    (q_tiles, q_tiles + kv_tiles, 1, True, None, False),
                    (q_tiles + kv_tiles, n_tiles, None, False, None, False))
            w_qk = _qk_weights(w_in, (q_tiles + kv_tiles) * tn, tn)
            gains = jnp.stack([q_gain, k_gain]).astype(F32)[:, _rope_layout()]
            proj = _in_proj(xc, norm, mods, w_in, dm, tm=tm, tn=tn, segs=segs, rope=rope, w_qk=w_qk,
                            gains=gains)
            g = _gqa_attention(proj, dm, tq=min(2 * tq, dm.seq), tk=tk, ctx_out=i < len(layers) - 1)
        elif kind == "diff":
            w_in, lq1, lk1, lq2, lk2, subln, w_out = p
            n_tiles = w_in.shape[1] // tn
            segs = ((0, q_tiles, None, True, scale, False),
                    (q_tiles, 2 * q_tiles, None, True, None, False),
                    (2 * q_tiles, n_tiles, None, False, None, False))
            proj = _in_proj(xc, norm, mods, w_in, dm, tm=tm, tn=tn, segs=segs, rope=rope,
                            w_qk=_qk_weights(w_in, 2 * q_tiles * tn, tn))
            lam_vecs = jnp.stack([lq1, lk1, lq2, lk2]).astype(F32)
            g = _diff_attention(proj, lam_vecs, subln, dm, tq=min(4 * tq, dm.seq), tk=tk,
                                lambda_init=_lambda_init(i))
        else:
            w_in, conv_w, conv_b, dt_bias, a_log, d_skip, ssm_norm, w_out = p
            inner = w_out.shape[0]
            n_main = 2 * inner + 2 * SSM_GROUPS * SSM_STATE
            tn_ssd = math.gcd(n_main, 2 * tn)
            segs = ((0, inner // tn_ssd, None, False, None, True),
                    (inner // tn_ssd, n_main // tn_ssd, None, False, None, False))
            proj, dt, dtT = _in_proj(xc, norm, mods, w_in, dm, tm=tm, tn=tn_ssd,
                                     segs=segs, w_dt=w_in[:, n_main:].astype(BF16))
            xbc = _ssd_conv(proj, conv_w, conv_b, dm, col0=inner, tr=tq,
                            tn=math.gcd(inner, conv_w.shape[1]))
            y_f = _ssd_scan(xbc, dt, dtT, dt_bias, a_log, dm, inner=inner, reverse=False)
            g = _ssd_scan(xbc, dt, dtT, dt_bias, a_log, dm, inner=inner, reverse=True,
                          tail=(y_f, proj, d_skip, ssm_norm))
        wide = w_out.shape[0] > d
        xc = _out_proj(g, w_out, xc, mods, dm, tm=tm if wide else _rows(dm), tn=tn)

    out = _final_norm(xc, final_norm, dm, tr=tq)
    return out.reshape(dm.batch, dm.seq, d)


def kernel(x, c, ctx, c_ctx, l0_w_mod, l0_b_mod, l0_norm, l0_w_in, l0_q_gain, l0_k_gain, l0_w_out, l1_w_mod, l1_b_mod, l1_norm, l1_w_in, l1_lambda_q1, l1_lambda_k1, l1_lambda_q2, l1_lambda_k2, l1_subln, l1_w_out, l2_w_mod, l2_b_mod, l2_norm, l2_w_in, l2_conv_w, l2_conv_b, l2_dt_bias, l2_A_log, l2_D, l2_ssm_norm, l2_w_out, l3_w_mod, l3_b_mod, l3_norm, l3_w_in, l3_q_gain, l3_k_gain, l3_w_out, final_norm):
    dm = Dims(batch=x.shape[0], seq=x.shape[1], ctx=ctx.shape[1], d=x.shape[2])
    layers = (
        ("gqa", (l0_w_mod, l0_b_mod, l0_norm), (l0_w_in, l0_q_gain, l0_k_gain, l0_w_out)),
        ("diff", (l1_w_mod, l1_b_mod, l1_norm), (l1_w_in, l1_lambda_q1, l1_lambda_k1, l1_lambda_q2,
                                                 l1_lambda_k2, l1_subln, l1_w_out)),
        ("ssd", (l2_w_mod, l2_b_mod, l2_norm), (l2_w_in, l2_conv_w, l2_conv_b, l2_dt_bias, l2_A_log,
                                                l2_D, l2_ssm_norm, l2_w_out)),
        ("gqa", (l3_w_mod, l3_b_mod, l3_norm), (l3_w_in, l3_q_gain, l3_k_gain, l3_w_out)),
    )
    return _forward(dm, x, c, ctx, c_ctx, layers, final_norm)
```

```python
import collections
import functools
import math

import jax
import jax.numpy as jnp
from jax import lax
from jax.experimental import pallas as pl
from jax.experimental.pallas import tpu as pltpu

F32 = jnp.float32
BF16 = jnp.bfloat16

EPS = 1e-6
ROPE_THETA = 10000.0
HEAD_DIM = 128
GQA_KV_HEADS = 4
SSM_HEAD_DIM = 64
SSM_GROUPS = 8
SSM_STATE = 128
SSM_CONV = 4
SSM_CHUNK = 128
SSM_CHUNKS_PER_STEP = 2
GRID_W = 64
N_MODS = 3
MOD_ROWS = 8
MOD_TILE = 1024
ROW_PARTS = 8
NORM_SLAB = 16
BF16_SUBLANES = 16
VMEM_LIMIT = 56 * 1024 * 1024

Dims = collections.namedtuple("Dims", "batch seq ctx d")


def _rows(dm):
    return dm.ctx + dm.seq


def _silu(v):
    return v * (1.0 / (1.0 + jnp.exp2(v * -math.log2(math.e))))


def _params(semantics):
    return pltpu.CompilerParams(dimension_semantics=semantics, vmem_limit_bytes=VMEM_LIMIT)


def _mod_kernel(c_ref, w_ref, b_ref, o_ref):
    s = _silu(c_ref[...]).astype(BF16)
    o_ref[...] = jnp.dot(s, w_ref[...].astype(BF16), preferred_element_type=F32) + b_ref[...]


def _modulation(cvec, w_mod, b_mod):
    d, n = w_mod.shape
    tn = math.gcd(n, MOD_TILE)
    return pl.pallas_call(
        _mod_kernel,
        grid=(n // tn,),
        in_specs=[pl.BlockSpec((MOD_ROWS, d), lambda j: (0, 0)),
                  pl.BlockSpec((d, tn), lambda j: (0, j)),
                  pl.BlockSpec((1, tn), lambda j: (0, j))],
        out_specs=pl.BlockSpec((MOD_ROWS, tn), lambda j: (0, j)),
        out_shape=jax.ShapeDtypeStruct((MOD_ROWS, n), F32),
        compiler_params=_params(("arbitrary",)),
        name="adaln_mod",
    )(cvec, w_mod, b_mod.reshape(1, n))


def _rope_layout():
    quarter = HEAD_DIM // 4
    idx = jnp.arange(HEAD_DIM).reshape(4, quarter)
    return idx[jnp.array([0, 2, 1, 3])].reshape(HEAD_DIM)


def _in_proj_kernel(*refs, tm, tn, tiles_per_batch, ctx_len, segs, has_rope, has_gain, has_dt):
    it = iter(refs)
    x_ref, g_ref, shl_ref, scl_ref, shc_ref, scc_ref, w_ref = (next(it) for _ in range(7))
    cos_ref = sin_ref = wqk_ref = gain_ref = wdt_ref = wdtT_ref = dt_ref = dtT_ref = None
    if has_rope:
        cos_ref, sin_ref, wqk_ref = next(it), next(it), next(it)
    if has_gain:
        gain_ref = next(it)
    if has_dt:
        wdt_ref, wdtT_ref = next(it), next(it)
    o_ref = next(it)
    if has_dt:
        dt_ref, dtT_ref = next(it), next(it)
    h_ref, mul_ref, add_ref = next(it), next(it), next(it)

    i = pl.program_id(0)
    j = pl.program_id(1)

    part = tm // ROW_PARTS
    starts = range(0, tm, part)

    def modulation_tables():
        d = g_ref.shape[1]
        g = g_ref[...]
        for k, (sc_ref, sh_ref) in enumerate(((scl_ref, shl_ref), (scc_ref, shc_ref))):
            mul_ref[k] = jnp.broadcast_to(g * (1.0 + sc_ref[0]), (NORM_SLAB, d))
            add_ref[k] = jnp.broadcast_to(sh_ref[0], (NORM_SLAB, d))

    def normalise(r0):
        for s0 in range(r0, r0 + part, NORM_SLAB):
            rows = slice(s0, s0 + NORM_SLAB)
            x = x_ref[rows, :]
            y = x * lax.rsqrt(jnp.mean(x * x, axis=-1, keepdims=True) + EPS)
            k = ((i % tiles_per_batch) * tm + s0 < ctx_len).astype(jnp.int32)
            h_ref[rows, :] = (y * mul_ref[k] + add_ref[k]).astype(BF16)

    def tile(gain_idx, rope, scale, gate, first):
        w = wqk_ref[...] if rope else w_ref[...].astype(BF16)
        plain = gain_idx is None and not rope and scale is None and not gate
        if plain and not first:
            o_ref[...] = jnp.dot(h_ref[...], w, preferred_element_type=F32).astype(o_ref.dtype)
            return
        if first:
            modulation_tables()
        accs = []
        for r0 in starts:
            if first:
                normalise(r0)
            accs.append(jnp.dot(h_ref[r0:r0 + part, :], w, preferred_element_type=F32))
        for r0, acc in zip(starts, accs):
            rows = slice(r0, r0 + part)
            if plain or gate:
                o_ref[rows, :] = (_silu(acc) if gate else acc).astype(o_ref.dtype)
                continue
            for hh in range(tn // HEAD_DIM):
                cols = slice(hh * HEAD_DIM, (hh + 1) * HEAD_DIM)
                v = acc[:, cols]
                if gain_idx is not None:
                    v = v * lax.rsqrt(jnp.mean(v * v, axis=-1, keepdims=True) + EPS)
                    v = v * gain_ref[gain_idx:gain_idx + 1, :]
                if rope:
                    v = v * cos_ref[rows, :] + pltpu.roll(v, HEAD_DIM // 2, 1) * sin_ref[rows, :]
                if scale is not None:
                    v = v * scale
                o_ref[rows, cols] = v.astype(o_ref.dtype)
        if first and has_dt:
            h = h_ref[...]
            dt_ref[...] = jnp.dot(h, wdt_ref[...], preferred_element_type=F32)
            dtT_ref[...] = lax.dot_general(wdtT_ref[...], h, (((1,), (1,)), ((), ())),
                                           preferred_element_type=F32)

    for (lo, hi, gain_idx, rope, scale, gate) in segs:
        if lo == 0:
            pl.when(j == 0)(functools.partial(tile, gain_idx, rope, scale, gate, True))
            lo = 1
        pl.when((j >= lo) & (j < hi))(functools.partial(tile, gain_idx, rope, scale, gate, False))


def _in_proj(xc, norm_g, mods, w, dm, *, tm, tn, segs, rope=None, w_qk=None, gains=None, w_dt=None):
    m, d = xc.shape
    n = segs[-1][1] * tn
    tpb = _rows(dm) // tm
    mods3 = mods.reshape(MOD_ROWS, 1, N_MODS * d)
    has_rope, has_gain, has_dt = rope is not None, gains is not None, w_dt is not None
    qk_tiles = w_qk.shape[1] // tn if has_rope else 0

    def lat(col):
        return pl.BlockSpec((1, 1, d), lambda i, j: (i // tpb, 0, col))

    def ctx(col):
        return pl.BlockSpec((1, 1, d), lambda i, j: (dm.batch, 0, col))

    args = [xc, norm_g.reshape(1, d), mods3, mods3, mods3, mods3, w]
    in_specs = [pl.BlockSpec((tm, d), lambda i, j: (i, 0)),
                pl.BlockSpec((1, d), lambda i, j: (0, 0)),
                lat(0), lat(1), ctx(0), ctx(1),
                pl.BlockSpec((d, tn), lambda i, j: (0, jnp.maximum(j, qk_tiles)))]
    if has_rope:
        args += list(rope) + [w_qk]
        in_specs += [pl.BlockSpec((tm, HEAD_DIM), lambda i, j: (i % tpb, 0))] * 2
        in_specs.append(pl.BlockSpec((d, tn), lambda i, j: (0, jnp.minimum(j, qk_tiles - 1))))
    if has_gain:
        args.append(gains)
        in_specs.append(pl.BlockSpec(gains.shape, lambda i, j: (0, 0)))
    out_shape = [jax.ShapeDtypeStruct((m, n), BF16)]
    out_specs = [pl.BlockSpec((tm, tn), lambda i, j: (i, j))]
    if has_dt:
        n_dt = w_dt.shape[1]
        args += [w_dt, w_dt.T]
        in_specs += [pl.BlockSpec((d, n_dt), lambda i, j: (0, 0)),
                     pl.BlockSpec((n_dt, d), lambda i, j: (0, 0))]
        out_shape += [jax.ShapeDtypeStruct((m, n_dt), F32), jax.ShapeDtypeStruct((n_dt, m), F32)]
        out_specs += [pl.BlockSpec((tm, n_dt), lambda i, j: (i, 0)),
                      pl.BlockSpec((n_dt, tm), lambda i, j: (0, i))]
    kern = functools.partial(_in_proj_kernel, tm=tm, tn=tn, tiles_per_batch=tpb, ctx_len=dm.ctx,
                             segs=segs, has_rope=has_rope, has_gain=has_gain, has_dt=has_dt)
    out = pl.pallas_call(
        kern,
        grid=(m // tm, n // tn),
        in_specs=in_specs,
        out_specs=out_specs,
        out_shape=out_shape,
        scratch_shapes=[pltpu.VMEM((tm, d), BF16),
                        pltpu.VMEM((2, NORM_SLAB, d), F32),
                        pltpu.VMEM((2, NORM_SLAB, d), F32)],
        compiler_params=_params(("parallel", "arbitrary")),
        name="in_proj",
    )(*args)
    return out if has_dt else out[0]


def _out_proj_kernel(g_ref, w_ref, x_ref, gl_ref, gc_ref, o_ref, *, tm, tiles_per_batch, ctx_len):
    i = pl.program_id(0)
    acc = jnp.dot(g_ref[...], w_ref[...].astype(BF16), preferred_element_type=F32)
    row = (i % tiles_per_batch) * tm + lax.broadcasted_iota(jnp.int32, (tm, 1), 0)
    gate = jnp.where(row < ctx_len, gc_ref[0], gl_ref[0])
    o_ref[...] = x_ref[...] + gate * acc


def _out_proj(g, w, xc, mods, dm, *, tm, tn):
    m, k = g.shape
    d = w.shape[1]
    tpb = _rows(dm) // tm
    tiles_n = d // tn
    mods3 = mods.reshape(MOD_ROWS, 1, N_MODS * d)
    gate_col = 2 * tiles_n
    kern = functools.partial(_out_proj_kernel, tm=tm, tiles_per_batch=tpb, ctx_len=dm.ctx)
    return pl.pallas_call(
        kern,
        grid=(m // tm, tiles_n),
        in_specs=[pl.BlockSpec((tm, k), lambda i, j: (i, 0)),
                  pl.BlockSpec((k, tn), lambda i, j: (0, j)),
                  pl.BlockSpec((tm, tn), lambda i, j: (i, j)),
                  pl.BlockSpec((1, 1, tn), lambda i, j: (i // tpb, 0, gate_col + j)),
                  pl.BlockSpec((1, 1, tn), lambda i, j: (dm.batch, 0, gate_col + j))],
        out_specs=pl.BlockSpec((tm, tn), lambda i, j: (i, j)),
        out_shape=jax.ShapeDtypeStruct((m, d), F32),
        compiler_params=_params(("parallel", "arbitrary")),
        name="out_proj",
    )(g, w, xc, mods3, mods3)


def _lane_tile_reduce(v, op):
    part = v[:, :HEAD_DIM]
    for t in range(1, v.shape[1] // HEAD_DIM):
        part = op(part, v[:, t * HEAD_DIM:(t + 1) * HEAD_DIM])
    return part


def _scores(pairs, k_ref, krows):
    blocks = []
    for q, kcols in pairs:
        half = q.shape[0] // 2
        for qq in (q[:half], q[half:]):
            blocks.append(lax.dot_general(qq, k_ref[krows, kcols], (((1,), (1,)), ((), ())),
                                          preferred_element_type=F32))
    return jnp.concatenate(blocks, axis=0)


def _weighted(p, v_ref, krows, with_sums=False):
    half = p.shape[0] // 2
    pb = p.astype(BF16)
    v = v_ref[krows, :]
    if with_sums:
        v = jnp.concatenate([v, jnp.ones_like(v)], axis=1)
    return jnp.concatenate([jnp.dot(pp, v, preferred_element_type=F32)
                            for pp in (pb[:half], pb[half:])], axis=0)


def _attend_in_place(pairs, k_ref, v_ref, key_rows, tk):
    chunks = [slice(c * tk, (c + 1) * tk) for c in range(key_rows // tk)]
    s = [_scores(pairs, k_ref, krows) for krows in chunks]
    m_part = functools.reduce(jnp.maximum, [_lane_tile_reduce(sc, jnp.maximum) for sc in s])
    m = jnp.max(m_part, axis=1, keepdims=True)
    p = [jnp.exp2(sc - m) for sc in s]
    l_part = functools.reduce(jnp.add, [_lane_tile_reduce(pc, jnp.add) for pc in p])
    acc = functools.reduce(jnp.add, [_weighted(pc, v_ref, krows) for pc, krows in zip(p, chunks)])
    return acc * (1.0 / jnp.sum(l_part, axis=1, keepdims=True))


def _attend_pipelined(pairs_next, s_next_ref, s_cur_ref, finish_cur, k_ref, v_ref, m_ref, key_rows, tk):
    n_chunks = key_rows // tk
    have_cur = s_cur_ref is not None
    dv = v_ref.shape[1]
    mxu_sums = dv == HEAD_DIM
    chunks = [slice(c * tk, (c + 1) * tk) for c in range(n_chunks)]
    acc = None
    for c, krows in enumerate(chunks if have_cur else ()):
        p = jnp.concatenate(
            [jnp.exp2(s_cur_ref[c, :, t * HEAD_DIM:(t + 1) * HEAD_DIM] - m_ref[0])
             for t in range(tk // HEAD_DIM)], axis=1)
        if not mxu_sums:
            part = _lane_tile_reduce(p, jnp.add)
            m_ref[1] = part if c == 0 else m_ref[1] + part
        pv = _weighted(p, v_ref, krows, with_sums=mxu_sums)
        acc = pv if acc is None else acc + pv
    if have_cur:
        if mxu_sums:
            finish_cur(acc[:, :dv] * (1.0 / acc[:, dv:]))
        else:
            finish_cur(acc * (1.0 / jnp.sum(m_ref[1], axis=1, keepdims=True)))
    m_part = None
    for c, krows in enumerate(chunks if pairs_next is not None else ()):
        s = _scores(pairs_next, k_ref, krows)
        s_next_ref[c] = s
        part = _lane_tile_reduce(s, jnp.maximum)
        m_part = part if m_part is None else jnp.maximum(m_part, part)
    if pairs_next is not None:
        m_ref[0] = jnp.broadcast_to(jnp.max(m_part, axis=1, keepdims=True), m_ref.shape[1:])


def _attention_steps(step, queries, finish, k_ref, v_ref, o_ref, s_ref, m_ref, *, tq, tk, ctx_len, rows,
                     ctx_out):
    n_lat = (rows - ctx_len) // tq

    def lat_row0(t):
        return pl.multiple_of(ctx_len + (t - 1) * tq, math.gcd(ctx_len, tq))

    def pipe(t_next, t_cur):
        pairs = None if t_next is None else queries(lat_row0(t_next), tq)
        _attend_pipelined(pairs, None if t_next is None else s_ref,
                          None if t_cur is None else s_ref,
                          None if t_cur is None else functools.partial(finish, lat_row0(t_cur), tq),
                          k_ref, v_ref, m_ref, rows, tk)

    @pl.when(step == 0)
    def _():
        if ctx_out:
            finish(0, ctx_len, _attend_in_place(queries(0, ctx_len), k_ref, v_ref, ctx_len, tk))
        else:
            o_ref[:ctx_len, :] = jnp.zeros((ctx_len, o_ref.shape[1]), o_ref.dtype)
        pipe(1, None)

    if n_lat > 1:
        pl.when((step >= 1) & (step < n_lat))(functools.partial(pipe, step + 1, step))
    pl.when(step == n_lat)(functools.partial(pipe, None, step))


def _gqa_kernel(q_ref, k_ref, v_ref, z_ref, o_ref, s_ref, m_ref, *, tq, tk, group, ctx_len, rows,
                ctx_out):
    def queries(row0, nq):
        q = jnp.concatenate([q_ref[pl.ds(row0, nq), g * HEAD_DIM:(g + 1) * HEAD_DIM]
                             for g in range(group)], axis=0)
        return [(q, slice(0, HEAD_DIM))]

    def finish(row0, nq, o):
        qrows = pl.ds(row0, nq)
        for g in range(group):
            cols = slice(g * HEAD_DIM, (g + 1) * HEAD_DIM)
            gate = z_ref[qrows, cols].astype(F32)
            o_ref[qrows, cols] = (o[g * nq:(g + 1) * nq] * gate).astype(o_ref.dtype)

    _attention_steps(pl.program_id(2), queries, finish, k_ref, v_ref, o_ref, s_ref, m_ref,
                     tq=tq, tk=tk, ctx_len=ctx_len, rows=rows, ctx_out=ctx_out)


def _gqa_attention(proj, dm, *, tq, tk, ctx_out):
    rows = _rows(dm)
    m = proj.shape[0]
    width = dm.d
    heads = width // HEAD_DIM
    group = heads // GQA_KV_HEADS
    gw = group * HEAD_DIM
    k_col0 = width // HEAD_DIM
    v_col0 = k_col0 + GQA_KV_HEADS
    z_col0 = (width + 2 * GQA_KV_HEADS * HEAD_DIM) // gw
    kern = functools.partial(_gqa_kernel, tq=tq, tk=tk, group=group, ctx_len=dm.ctx, rows=rows,
                             ctx_out=ctx_out)
    return pl.pallas_call(
        kern,
        grid=(dm.batch, GQA_KV_HEADS, 1 + dm.seq // tq),
        in_specs=[pl.BlockSpec((rows, gw), lambda b, h, i: (b, h)),
                  pl.BlockSpec((rows, HEAD_DIM), lambda b, h, i: (b, k_col0 + h)),
                  pl.BlockSpec((rows, HEAD_DIM), lambda b, h, i: (b, v_col0 + h)),
                  pl.BlockSpec((rows, gw), lambda b, h, i: (b, z_col0 + h))],
        out_specs=pl.BlockSpec((rows, gw), lambda b, h, i: (b, h)),
        out_shape=jax.ShapeDtypeStruct((m, width), BF16),
        scratch_shapes=[pltpu.VMEM((rows // tk, group * tq, tk), F32),
                        pltpu.VMEM((2, group * tq, HEAD_DIM), F32)],
        compiler_params=_params(("parallel", "parallel", "arbitrary")),
        name="gqa_attention",
    )(proj, proj, proj, proj)


def _diff_kernel(lam_ref, q_ref, k_ref, v_ref, z_ref, subln_ref, o_ref, s_ref, m_ref,
                 *, tq, tk, ctx_len, rows, lambda_init):
    lv = lam_ref[...]
    lam = (jnp.exp(jnp.sum(lv[0:1] * lv[1:2], axis=1, keepdims=True))
           - jnp.exp(jnp.sum(lv[2:3] * lv[3:4], axis=1, keepdims=True)) + lambda_init)

    def queries(row0, nq):
        halves = [slice(t * HEAD_DIM, (t + 1) * HEAD_DIM) for t in range(2)]
        return [(q_ref[pl.ds(row0, nq), cols], cols) for cols in halves]

    def finish(row0, nq, o):
        qrows = pl.ds(row0, nq)
        o = o[:nq] - lam * o[nq:]
        o = o * lax.rsqrt(jnp.mean(o * o, axis=-1, keepdims=True) + EPS) * subln_ref[...]
        o = o * (1.0 - lambda_init)
        o_ref[qrows, :] = (o * z_ref[qrows, :].astype(F32)).astype(o_ref.dtype)

    _attention_steps(pl.program_id(2), queries, finish, k_ref, v_ref, o_ref, s_ref, m_ref,
                     tq=tq, tk=tk, ctx_len=ctx_len, rows=rows, ctx_out=True)


def _diff_attention(proj, lam_vecs, subln, dm, *, tq, tk, lambda_init):
    rows = _rows(dm)
    m = proj.shape[0]
    width = dm.d
    hw = 2 * HEAD_DIM
    heads = width // hw
    kern = functools.partial(_diff_kernel, tq=tq, tk=tk, ctx_len=dm.ctx, rows=rows, lambda_init=lambda_init)

    def head_block(first):
        return pl.BlockSpec((rows, hw), lambda b, h, i: (b, first + h))

    return pl.pallas_call(
        kern,
        grid=(dm.batch, heads, 1 + dm.seq // tq),
        in_specs=[pl.BlockSpec(lam_vecs.shape, lambda b, h, i: (0, 0)),
                  head_block(0), head_block(heads), head_block(2 * heads), head_block(3 * heads),
                  pl.BlockSpec((1, hw), lambda b, h, i: (0, 0))],
        out_specs=head_block(0),
        out_shape=jax.ShapeDtypeStruct((m, width), BF16),
        scratch_shapes=[pltpu.VMEM((rows // tk, 2 * tq, tk), F32),
                        pltpu.VMEM((2, 2 * tq, HEAD_DIM), F32)],
        compiler_params=_params(("parallel", "parallel", "arbitrary")),
        name="diff_attention",
    )(lam_vecs, proj, proj, proj, proj, subln.reshape(1, hw))


CONV_TAPS = (-1, 0, 1, 2)
CONV_STRIP = 256


def _conv_kernel(x_ref, shift_ref, w_ref, b_ref, o_ref, *, tr, ctx_tiles):
    edge = BF16_SUBLANES
    centre = CONV_TAPS.index(0)
    n_tiles = x_ref.shape[0] // tr
    row = lax.broadcasted_iota(jnp.int32, (edge, 1), 0)
    for c0 in range(0, x_ref.shape[1], CONV_STRIP):
        cols = slice(c0, c0 + CONV_STRIP)
        w = [w_ref[t:t + 1, cols] for t in range(SSM_CONV)]
        for r in range(n_tiles):
            r0 = r * tr
            xb = x_ref[r0:r0 + tr, cols]
            y = b_ref[:, cols] + w[centre] * xb.astype(F32)
            for k, tap in enumerate(t for t in range(SSM_CONV) if t != centre):
                y = y + w[tap] * jnp.dot(shift_ref[k], xb, preferred_element_type=F32)
            o_ref[r0 + edge:r0 + tr - edge, cols] = _silu(y[edge:tr - edge]).astype(o_ref.dtype)
            top, bottom = y[:edge], y[tr - edge:]
            if r not in (0, ctx_tiles):
                prev = x_ref[r0 - 1:r0, cols].astype(F32)
                top = top + jnp.where(row == 0, w[0] * prev, 0.0)
            if r not in (ctx_tiles - 1, n_tiles - 1):
                nxt0 = x_ref[r0 + tr:r0 + tr + 1, cols].astype(F32)
                nxt1 = x_ref[r0 + tr + 1:r0 + tr + 2, cols].astype(F32)
                bottom = (bottom + jnp.where(row == edge - 2, w[3] * nxt0, 0.0)
                          + jnp.where(row == edge - 1, w[2] * nxt0 + w[3] * nxt1, 0.0))
            o_ref[r0:r0 + edge, cols] = _silu(top).astype(o_ref.dtype)
            o_ref[r0 + tr - edge:r0 + tr, cols] = _silu(bottom).astype(o_ref.dtype)


def _ssd_conv(proj, conv_w, conv_b, dm, *, col0, tr, tn):
    rows = _rows(dm)
    m = proj.shape[0]
    n = conv_w.shape[1]
    c0 = col0 // tn
    shifts = jnp.stack([jnp.eye(tr, k=off, dtype=BF16) for off in CONV_TAPS if off != 0])
    kern = functools.partial(_conv_kernel, tr=tr, ctx_tiles=dm.ctx // tr)
    return pl.pallas_call(
        kern,
        grid=(dm.batch, n // tn),
        in_specs=[pl.BlockSpec((rows, tn), lambda b, j: (b, c0 + j)),
                  pl.BlockSpec(shifts.shape, lambda b, j: (0, 0, 0)),
                  pl.BlockSpec((SSM_CONV, tn), lambda b, j: (0, j)),
                  pl.BlockSpec((1, tn), lambda b, j: (0, j))],
        out_specs=pl.BlockSpec((rows, tn), lambda b, j: (b, j)),
        out_shape=jax.ShapeDtypeStruct((m, n), BF16),
        compiler_params=_params(("parallel", "parallel")),
        name="ssd_conv",
    )(proj, shifts, conv_w, conv_b.reshape(1, n))


def _split_dot(a, b_hi_exact, dims):
    hi = a.astype(BF16)
    lo = (a - hi.astype(F32)).astype(BF16)
    if dims == "ab":
        return (jnp.dot(hi, b_hi_exact, preferred_element_type=F32)
                + jnp.dot(lo, b_hi_exact, preferred_element_type=F32))
    return (jnp.dot(b_hi_exact, hi, preferred_element_type=F32)
            + jnp.dot(b_hi_exact, lo, preferred_element_type=F32))


def _ssd_kernel(*refs, heads, reverse, finish):
    it = iter(refs)
    (x_ref, b_ref, c_ref, dt_ref, dtT_ref, bias_ref, biasT_ref, alog_ref, alogT_ref,
     expand_ref) = (next(it) for _ in range(10))
    other_ref = z_ref = dskip_ref = gain_ref = None
    if finish:
        other_ref, z_ref, dskip_ref, gain_ref = (next(it) for _ in range(4))
    y_ref, state_ref = next(it), next(it)
    q = SSM_CHUNK
    hpg = heads // SSM_GROUPS
    gw = hpg * SSM_HEAD_DIM
    pw = 2 * SSM_HEAD_DIM
    step = pl.program_id(1)

    @pl.when(step == 0)
    def _():
        state_ref[...] = jnp.zeros(state_ref.shape, F32)

    def softplus(v):
        return jnp.maximum(v, 0.0) + jnp.log(1.0 + jnp.exp(-jnp.abs(v)))

    def chunk(rs):
        dcol = heads if reverse else 0
        dt = softplus(dt_ref[rs, dcol:dcol + heads] + bias_ref[...])
        dtT = softplus(dtT_ref[:, rs] + biasT_ref[...])
        a = dt * -jnp.exp(alog_ref[...])
        aT = dtT * -jnp.exp(alogT_ref[...])
        ri = lax.broadcasted_iota(jnp.int32, (q, q), 0)
        ci = lax.broadcasted_iota(jnp.int32, (q, q), 1)
        before = (ci >= ri) if reverse else (ci <= ri)
        tri = before.astype(BF16)
        triT = ((ri >= ci) if reverse else (ri <= ci)).astype(BF16)
        cum = _split_dot(a, tri, "ba")
        cumT = _split_dot(aT, triT, "ab")
        end = 0 if reverse else q - 1
        total = cum[end:end + 1, :]
        expand = expand_ref[...]
        w_full = jnp.dot((jnp.exp(total - cum) * dt).astype(BF16), expand,
                         preferred_element_type=F32)
        e_full = jnp.dot(jnp.exp(cum).astype(BF16), expand, preferred_element_type=F32)
        cd_full = _split_dot(jnp.broadcast_to(jnp.exp(total), (8, heads)), expand, "ab")[0:1]
        log2e = math.log2(math.e)
        cum2 = cum * log2e
        lcumT2 = (cumT - jnp.log(dtT)) * log2e
        low = lax.broadcasted_iota(jnp.int32, (q, pw), 1) < SSM_HEAD_DIM

        for g in range(SSM_GROUPS):
            gcols = slice(g * gw, (g + 1) * gw)
            ncols = slice(g * SSM_STATE, (g + 1) * SSM_STATE)
            bg = b_ref[rs, ncols]
            cg = c_ref[rs, ncols]
            cb = lax.dot_general(cg, bg, (((1,), (1,)), ((), ())), preferred_element_type=F32)
            st = state_ref[:, gcols]
            y_off = jnp.dot(cg, st.astype(BF16), preferred_element_type=F32) * e_full[:, gcols]
            ys = []
            for j in range(hpg // 2):
                xp = x_ref[rs, g * gw + j * pw:g * gw + (j + 1) * pw].astype(F32)
                xbd = jnp.concatenate([jnp.where(low, xp, 0.0), jnp.where(low, 0.0, xp)], axis=0).astype(BF16)
                mats = []
                for hh in range(2):
                    h = g * hpg + 2 * j + hh
                    diff = cum2[:, h:h + 1] - lcumT2[h:h + 1, :]
                    mats.append((cb * jnp.exp2(jnp.where(before, diff, -jnp.inf))).astype(BF16))
                ys.append(jnp.dot(jnp.concatenate(mats, axis=1), xbd, preferred_element_type=F32)
                          + y_off[:, j * pw:(j + 1) * pw])
            xg = x_ref[rs, gcols].astype(F32)
            if finish:
                y = jnp.concatenate(ys, axis=1) + other_ref[rs, gcols].astype(F32) + dskip_ref[:, gcols] * xg
                v = y * z_ref[rs, gcols].astype(F32)
                v = v * lax.rsqrt(jnp.mean(v * v, axis=-1, keepdims=True) + EPS) * gain_ref[:, gcols]
                y_ref[rs, gcols] = v.astype(y_ref.dtype)
            else:
                for j, y in enumerate(ys):
                    y_ref[rs, g * gw + j * pw:g * gw + (j + 1) * pw] = y.astype(y_ref.dtype)
            xw = (xg * w_full[:, gcols]).astype(BF16)
            bgT = bg.astype(F32).T.astype(BF16)
            state_ref[:, gcols] = st * cd_full[:, gcols] + jnp.dot(bgT, xw, preferred_element_type=F32)

    order = range(x_ref.shape[0] // q)
    for k in (reversed(order) if reverse else order):
        chunk(slice(k * q, (k + 1) * q))


def _ssd_scan(xbc, dt, dtT, dt_bias, a_log, dm, *, inner, reverse, tail=None):
    rows = _rows(dm)
    m = xbc.shape[0]
    q = SSM_CHUNKS_PER_STEP * SSM_CHUNK
    heads = inner // SSM_HEAD_DIM
    nc = rows // q
    ctx_chunks = dm.ctx // q
    sw = SSM_GROUPS * SSM_STATE
    d = 1 if reverse else 0

    def chunk(b, s):
        if reverse:
            s = jnp.where(s < ctx_chunks, ctx_chunks - 1 - s, nc - 1 - (s - ctx_chunks))
        return b * nc + s

    expand = jnp.repeat(jnp.eye(heads, dtype=BF16), SSM_HEAD_DIM, axis=1)
    row_spec = pl.BlockSpec((q, inner), lambda b, s: (chunk(b, s), 0))
    vec_spec = pl.BlockSpec((1, inner), lambda b, s: (0, 0))
    args = [xbc, xbc, xbc, dt, dtT, dt_bias[d][None, :], dt_bias[d][:, None], a_log[d][None, :],
            a_log[d][:, None], expand]
    in_specs = [row_spec,
                pl.BlockSpec((q, sw), lambda b, s: (chunk(b, s), inner // sw)),
                pl.BlockSpec((q, sw), lambda b, s: (chunk(b, s), inner // sw + 1)),
                pl.BlockSpec((q, 2 * heads), lambda b, s: (chunk(b, s), 0)),
                pl.BlockSpec((heads, q), lambda b, s: (d, chunk(b, s))),
                pl.BlockSpec((1, heads), lambda b, s: (0, 0)),
                pl.BlockSpec((heads, 1), lambda b, s: (0, 0)),
                pl.BlockSpec((1, heads), lambda b, s: (0, 0)),
                pl.BlockSpec((heads, 1), lambda b, s: (0, 0)),
                pl.BlockSpec((heads, inner), lambda b, s: (0, 0))]
    if tail is not None:
        y_other, proj, d_skip, gain = tail
        args += [y_other, proj, jnp.repeat(d_skip.astype(F32), SSM_HEAD_DIM).reshape(1, inner),
                 gain.reshape(1, inner)]
        in_specs += [row_spec, row_spec, vec_spec, vec_spec]
    kern = functools.partial(_ssd_kernel, heads=heads, reverse=reverse, finish=tail is not None)
    return pl.pallas_call(
        kern,
        grid=(dm.batch, nc),
        in_specs=in_specs,
        out_specs=row_spec,
        out_shape=jax.ShapeDtypeStruct((m, inner), BF16),
        scratch_shapes=[pltpu.VMEM((SSM_STATE, inner), F32)],
        compiler_params=_params(("parallel", "arbitrary")),
        name="ssd_scan_bwd" if reverse else "ssd_scan_fwd",
    )(*args)


def _final_norm_kernel(x_ref, g_ref, o_ref):
    x = x_ref[...]
    o_ref[...] = x * lax.rsqrt(jnp.mean(x * x, axis=-1, keepdims=True) + EPS) * g_ref[...]


def _final_norm(xc, gain, dm, *, tr):
    rows = _rows(dm)
    tpb = rows // tr
    lat_tiles = dm.seq // tr
    ctx_tiles = dm.ctx // tr
    return pl.pallas_call(
        _final_norm_kernel,
        grid=(dm.batch, lat_tiles),
        in_specs=[pl.BlockSpec((tr, dm.d), lambda b, i: (b * tpb + ctx_tiles + i, 0)),
                  pl.BlockSpec((1, dm.d), lambda b, i: (0, 0))],
        out_specs=pl.BlockSpec((tr, dm.d), lambda b, i: (b * lat_tiles + i, 0)),
        out_shape=jax.ShapeDtypeStruct((dm.batch * dm.seq, dm.d), F32),
        compiler_params=_params(("parallel", "parallel")),
        name="final_norm",
    )(xc, gain.reshape(1, dm.d))


def _rope_tables(dm):
    d_axis = HEAD_DIM // 2
    t = jnp.arange(dm.seq, dtype=jnp.int32)
    inv_freq = jnp.power(ROPE_THETA, -jnp.arange(0, d_axis, 2, dtype=F32) / d_axis)
    ang_r = (t // GRID_W).astype(F32)[:, None] * inv_freq[None, :]
    ang_c = (t % GRID_W).astype(F32)[:, None] * inv_freq[None, :]
    cos = jnp.concatenate([jnp.cos(ang_r), jnp.cos(ang_c), jnp.cos(ang_r), jnp.cos(ang_c)], axis=1)
    sin = jnp.concatenate([-jnp.sin(ang_r), -jnp.sin(ang_c), jnp.sin(ang_r), jnp.sin(ang_c)], axis=1)
    cos = jnp.concatenate([jnp.ones((dm.ctx, HEAD_DIM), F32), cos], axis=0)
    sin = jnp.concatenate([jnp.zeros((dm.ctx, HEAD_DIM), F32), sin], axis=0)
    return cos, sin


def _relayout_kernel(w_ref, o_ref):
    tn = w_ref.shape[1]
    quarter = HEAD_DIM // 4
    r = lax.broadcasted_iota(jnp.int32, (tn, tn), 0)
    c = lax.broadcasted_iota(jnp.int32, (tn, tn), 1)
    blk = (c // quarter) % 4
    src_blk = jnp.where(blk == 1, 2, jnp.where(blk == 2, 1, blk))
    src = (c // HEAD_DIM) * HEAD_DIM + src_blk * quarter + c % quarter
    perm = (r == src).astype(BF16)
    o_ref[...] = jnp.dot(w_ref[...].astype(BF16), perm, preferred_element_type=F32).astype(BF16)


def _qk_weights(w_in, qk_cols, tn):
    d = w_in.shape[0]
    return pl.pallas_call(
        _relayout_kernel,
        grid=(qk_cols // tn,),
        in_specs=[pl.BlockSpec((d, tn), lambda j: (0, j))],
        out_specs=pl.BlockSpec((d, tn), lambda j: (0, j)),
        out_shape=jax.ShapeDtypeStruct((d, qk_cols), BF16),
        compiler_params=_params(("parallel",)),
        name="qk_relayout",
    )(w_in)


def _lambda_init(layer_idx):
    return 0.8 - 0.6 * math.exp(-0.3 * layer_idx)


def _tiles(dm):
    rows = _rows(dm)
    tm = rows // 2 if (rows // 2) % 128 == 0 else rows
    return dict(tm=tm, tn=min(512, dm.d), tq=dm.ctx, tk=dm.ctx)


def _forward(dm, x, c, ctx, c_ctx, layers, final_norm):
    d = dm.d
    til = _tiles(dm)
    tm, tn, tq, tk = til["tm"], til["tn"], til["tq"], til["tk"]
    scale = HEAD_DIM ** -0.5 * math.log2(math.e)
    xc = jnp.concatenate([ctx, x], axis=1).reshape(dm.batch * _rows(dm), d)
    cvec = jnp.concatenate([c, c_ctx[None, :], jnp.zeros((MOD_ROWS - dm.batch - 1, d), F32)], axis=0)
    rope = _rope_tables(dm)
    q_tiles = d // tn

    for i, (kind, (w_mod, b_mod, norm), p) in enumerate(layers):
        mods = _modulation(cvec, w_mod, b_mod)
        if kind == "gqa":
            w_in, q_gain, k_gain, w_out = p
            kv_tiles = GQA_KV_HEADS * HEAD_DIM // tn
            n_tiles = w_in.shape[1] // tn
            segs = ((0, q_tiles, 0, True, scale, False),
                    (q_tiles, q_tiles + kv_tiles, 1, True, None, False),
                    (q_tiles + kv_tiles, q_tiles + 2 * kv_tiles, None, False, None, False),
                    (q_tiles + 2 * kv_tiles, n_tiles, None, False, None, True))
            w_qk = _qk_weights(w_in, (q_tiles + kv_tiles) * tn, tn)
            gains = jnp.stack([q_gain, k_gain]).astype(F32)[:, _rope_layout()]
            proj = _in_proj(xc, norm, mods, w_in, dm, tm=tm, tn=tn, segs=segs, rope=rope, w_qk=w_qk,
                            gains=gains)
            g = _gqa_attention(proj, dm, tq=min(2 * tq, dm.seq), tk=tk, ctx_out=i < len(layers) - 1)
        elif kind == "diff":
            w_in, lq1, lk1, lq2, lk2, subln, w_out = p
            n_tiles = w_in.shape[1] // tn
            segs = ((0, q_tiles, None, True, scale, False),
                    (q_tiles, 2 * q_tiles, None, True, None, False),
                    (2 * q_tiles, 3 * q_tiles, None, False, None, False),
                    (3 * q_tiles, n_tiles, None, False, None, True))
            proj = _in_proj(xc, norm, mods, w_in, dm, tm=tm, tn=tn, segs=segs, rope=rope,
                            w_qk=_qk_weights(w_in, 2 * q_tiles * tn, tn))
            lam_vecs = jnp.stack([lq1, lk1, lq2, lk2]).astype(F32)
            g = _diff_attention(proj, lam_vecs, subln, dm, tq=min(4 * tq, dm.seq), tk=tk,
                                lambda_init=_lambda_init(i))
        else:
            w_in, conv_w, conv_b, dt_bias, a_log, d_skip, ssm_norm, w_out = p
            inner = w_out.shape[0]
            n_main = 2 * inner + 2 * SSM_GROUPS * SSM_STATE
            tn_ssd = math.gcd(n_main, 2 * tn)
            segs = ((0, inner // tn_ssd, None, False, None, True),
                    (inner // tn_ssd, n_main // tn_ssd, None, False, None, False))
            proj, dt, dtT = _in_proj(xc, norm, mods, w_in, dm, tm=tm, tn=tn_ssd,
                                     segs=segs, w_dt=w_in[:, n_main:].astype(BF16))
            xbc = _ssd_conv(proj, conv_w, conv_b, dm, col0=inner, tr=tq,
                            tn=math.gcd(inner, conv_w.shape[1]))
            y_f = _ssd_scan(xbc, dt, dtT, dt_bias, a_log, dm, inner=inner, reverse=False)
            g = _ssd_scan(xbc, dt, dtT, dt_bias, a_log, dm, inner=inner, reverse=True,
                          tail=(y_f, proj, d_skip, ssm_norm))
        wide = w_out.shape[0] > d
        xc = _out_proj(g, w_out, xc, mods, dm, tm=tm if wide else _rows(dm), tn=tn)

    out = _final_norm(xc, final_norm, dm, tr=tq)
    return out.reshape(dm.batch, dm.seq, d)


def kernel(x, c, ctx, c_ctx, l0_w_mod, l0_b_mod, l0_norm, l0_w_in, l0_q_gain, l0_k_gain, l0_w_out, l1_w_mod, l1_b_mod, l1_norm, l1_w_in, l1_lambda_q1, l1_lambda_k1, l1_lambda_q2, l1_lambda_k2, l1_subln, l1_w_out, l2_w_mod, l2_b_mod, l2_norm, l2_w_in, l2_conv_w, l2_conv_b, l2_dt_bias, l2_A_log, l2_D, l2_ssm_norm, l2_w_out, l3_w_mod, l3_b_mod, l3_norm, l3_w_in, l3_q_gain, l3_k_gain, l3_w_out, final_norm):
    dm = Dims(batch=x.shape[0], seq=x.shape[1], ctx=ctx.shape[1], d=x.shape[2])
    layers = (
        ("gqa", (l0_w_mod, l0_b_mod, l0_norm), (l0_w_in, l0_q_gain, l0_k_gain, l0_w_out)),
        ("diff", (l1_w_mod, l1_b_mod, l1_norm), (l1_w_in, l1_lambda_q1, l1_lambda_k1, l1_lambda_q2,
                                                 l1_lambda_k2, l1_subln, l1_w_out)),
        ("ssd", (l2_w_mod, l2_b_mod, l2_norm), (l2_w_in, l2_conv_w, l2_conv_b, l2_dt_bias, l2_A_log,
                                                l2_D, l2_ssm_norm, l2_w_out)),
        ("gqa", (l3_w_mod, l3_b_mod, l3_norm), (l3_w_in, l3_q_gain, l3_k_gain, l3_w_out)),
    )
    return _forward(dm, x, c, ctx, c_ctx, layers, final_norm)
```
